```python
import math
import jax
import jax.numpy as jnp
from jax import lax
import numpy as np

D_MODEL = 1024
BATCH = 8
SEQ = 4096
DEPTH = 1

GRID_W = 64
CTX_LEN = 256
LRU_WIDTH = 512
LRU_BLOCKS = 8
LRU_BLOCK_W = LRU_WIDTH // LRU_BLOCKS
LRU_C = 8.0
CONV_W = 4
CONV_PAD_LO = 1
ATT_HEADS = 4
HEAD_DIM = 64
V_DIM = 2 * HEAD_DIM
QK_WIDTH = ATT_HEADS * 2 * HEAD_DIM
ATT_WIDTH = ATT_HEADS * V_DIM
MIX_WIDTH = LRU_WIDTH + ATT_WIDTH
IN_WIDTH = 2 * LRU_WIDTH + 2 * QK_WIDTH + ATT_WIDTH
ROPE_AXIS_DIM = HEAD_DIM // 2
ROPE_PAIRS = ROPE_AXIS_DIM // 2
ROPE_BASE = 10000.0
Q_BLOCK = 128
N_EXPERTS = 16
EC_FACTOR = 2
D_EXPERT = 2816
N_MOD = 6
EPS = 1e-6

kernel_name = "hybrid_rglru_diffattn_ecmoe_dit"


def rmsnorm(x, g):
    xf = x.astype(jnp.float32)
    y = xf * lax.rsqrt(jnp.mean(xf * xf, axis=-1, keepdims=True) + EPS)
    return (y * g.astype(jnp.float32)).astype(x.dtype)


def adaln(cvec, w, b, n_chunks):
    cols = n_chunks * D_MODEL
    m = jax.nn.silu(cvec) @ w[:, :cols] + b[:cols]
    return m.reshape(cvec.shape[0], n_chunks, D_MODEL)


def modulate(h, shift, scale):
    return h * (1 + scale[:, None]) + shift[:, None]


def split_cols(p):
    o1 = LRU_WIDTH
    o2 = 2 * LRU_WIDTH
    o3 = o2 + QK_WIDTH
    o4 = o3 + QK_WIDTH
    return p[..., :o1], p[..., o1:o2], p[..., o2:o3], p[..., o3:o4], p[..., o4:]


def axial_rope_tables(n):
    rows = n // GRID_W
    row = jnp.repeat(jnp.arange(rows), GRID_W).astype(jnp.float32)
    col = jnp.tile(jnp.arange(GRID_W), rows).astype(jnp.float32)
    inv = ROPE_BASE ** (-jnp.arange(ROPE_PAIRS, dtype=jnp.float32) / ROPE_PAIRS)
    ang_r = row[:, None] * inv
    ang_c = col[:, None] * inv
    return jnp.cos(ang_r), jnp.sin(ang_r), jnp.cos(ang_c), jnp.sin(ang_c)


def rope_axis(x, cos, sin):
    cos = cos[:, None, None, :].astype(x.dtype)
    sin = sin[:, None, None, :].astype(x.dtype)
    x1, x2 = x[..., :ROPE_PAIRS], x[..., ROPE_PAIRS:]
    return jnp.concatenate([x1 * cos - x2 * sin, x1 * sin + x2 * cos], axis=-1)


def apply_axial_rope(x, tabs):
    cr, sr, cc, sc = tabs
    return jnp.concatenate([rope_axis(x[..., :ROPE_AXIS_DIM], cr, sr),
                            rope_axis(x[..., ROPE_AXIS_DIM:], cc, sc)], axis=-1)


def centred_depthwise_conv(x, w, b):
    n = x.shape[1]
    xp = jnp.pad(x, ((0, 0), (CONV_PAD_LO, CONV_W - 1 - CONV_PAD_LO), (0, 0)))
    return sum(xp[:, k:k + n] * w[k] for k in range(CONV_W)) + b


def block_diag(x, w, b):
    xb = x.reshape(x.shape[:-1] + (LRU_BLOCKS, LRU_BLOCK_W))
    y = jnp.einsum("bsnw,nwv->bsnv", xb, w) + b
    return y.reshape(x.shape)


def rglru_gates(xc, wa, ba, wi, bi, lam, reset_first):
    r = jax.nn.sigmoid(block_diag(xc, wa, ba)).astype(jnp.float32)
    i = jax.nn.sigmoid(block_diag(xc, wi, bi)).astype(jnp.float32)
    log_a = -LRU_C * r * jax.nn.softplus(-lam.astype(jnp.float32))
    a = jnp.exp(log_a)
    mult = jnp.sqrt(-jnp.expm1(2.0 * log_a))
    if reset_first:
        mult = mult.at[:, 0].set(1.0)
    return a, mult * i * xc.astype(jnp.float32)


def _combine(e1, e2):
    a1, b1 = e1
    a2, b2 = e2
    return a1 * a2, a2 * b1 + b2


def linear_scan(a, u, h0=None):
    if h0 is not None:
        u = u.at[:, 0].add(a[:, 0] * h0)
    _, h = lax.associative_scan(_combine, (a, u), axis=1)
    return h


def rglru_direction(xc_ctx, xc_lat, wa, ba, wi, bi, lam, reverse):
    if reverse:
        xc_ctx = xc_ctx[:, ::-1]
        xc_lat = xc_lat[:, ::-1]
    a_c, u_c = rglru_gates(xc_ctx, wa, ba, wi, bi, lam, True)
    h_c = linear_scan(a_c, u_c)
    a_l, u_l = rglru_gates(xc_lat, wa, ba, wi, bi, lam, False)
    h_l = linear_scan(a_l, u_l, h_c[:, -1])
    if reverse:
        return h_c[:, ::-1], h_l[:, ::-1]
    return h_c, h_l


def diff_attention(q, k, v, lam):
    B, n = q.shape[:2]
    nb = n // Q_BLOCK
    scale = HEAD_DIM ** -0.5
    qb = jnp.moveaxis(q.reshape((B, nb, Q_BLOCK) + q.shape[2:]), 1, 0)

    def block(qblk):
        s = jnp.einsum("bqhcd,bkhcd->bhcqk", qblk, k).astype(jnp.float32) * scale
        p = jax.nn.softmax(s, axis=-1)
        w = (p[:, :, 0] - lam * p[:, :, 1]).astype(v.dtype)
        return jnp.einsum("bhqk,bkhe->bqhe", w, v)

    o = lax.map(block, qb)
    return jnp.moveaxis(o, 0, 1).reshape(B, n, ATT_HEADS, V_DIM)


def hybrid_mixer(hl, hc, rope, w_in, conv_w, conv_b, lru_wa, lru_ba, lru_wi, lru_bi, lru_lambda,
                 q_g, k_g, lq1, lk1, lq2, lk2, subln_g, lam_init, ctx_out):
    B, n, _ = hl.shape
    L = hc.shape[1]
    w_x, w_gate, w_q, w_k, w_v = split_cols(w_in)
    x_l, gate_l, q_l, k_l, v_l = split_cols(hl @ w_in)
    x_c, k_c, v_c = hc @ w_x, hc @ w_k, hc @ w_v

    xl_conv = centred_depthwise_conv(x_l, conv_w, conv_b)
    xc_conv = centred_depthwise_conv(x_c, conv_w, conv_b)
    hf_c, hf_l = rglru_direction(xc_conv, xl_conv, lru_wa[0], lru_ba[0], lru_wi[0], lru_bi[0],
                                 lru_lambda[0], False)
    hb_c, hb_l = rglru_direction(xc_conv, xl_conv, lru_wa[1], lru_ba[1], lru_wi[1], lru_bi[1],
                                 lru_lambda[1], True)
    lru_l = (hf_l + hb_l).astype(hl.dtype) * jax.nn.gelu(gate_l)

    f32 = jnp.float32
    lam = (jnp.exp(jnp.sum(lq1.astype(f32) * lk1.astype(f32)))
           - jnp.exp(jnp.sum(lq2.astype(f32) * lk2.astype(f32))) + lam_init)
    q_l = apply_axial_rope(rmsnorm(q_l.reshape(B, n, ATT_HEADS, 2, HEAD_DIM), q_g), rope)
    k_l = apply_axial_rope(rmsnorm(k_l.reshape(B, n, ATT_HEADS, 2, HEAD_DIM), k_g), rope)
    k_c = rmsnorm(k_c.reshape(B, L, ATT_HEADS, 2, HEAD_DIM), k_g)
    v_l = v_l.reshape(B, n, ATT_HEADS, V_DIM)
    v_c = v_c.reshape(B, L, ATT_HEADS, V_DIM)
    k_all = jnp.concatenate([k_c, k_l], axis=1)
    v_all = jnp.concatenate([v_c, v_l], axis=1)
    att_l = diff_attention(q_l, k_all, v_all, lam)
    att_l = (rmsnorm(att_l, subln_g) * (1 - lam_init)).reshape(B, n, ATT_WIDTH)
    out_l = jnp.concatenate([lru_l, att_l], axis=-1)
    if not ctx_out:
        return out_l, None

    lru_c = (hf_c + hb_c).astype(hc.dtype) * jax.nn.gelu(hc @ w_gate)
    q_c = rmsnorm((hc @ w_q).reshape(B, L, ATT_HEADS, 2, HEAD_DIM), q_g)
    att_c = diff_attention(q_c, k_c, v_c, lam)
    att_c = (rmsnorm(att_c, subln_g) * (1 - lam_init)).reshape(B, L, ATT_WIDTH)
    return out_l, jnp.concatenate([lru_c, att_c], axis=-1)


def expert_choice_ffn(h, w_router, w_gate, w_up, w_down):
    B, n, D = h.shape
    cap = EC_FACTOR * n // N_EXPERTS
    aff = jax.nn.softmax((h @ w_router).astype(jnp.float32), axis=-1)
    g, idx = lax.top_k(jnp.swapaxes(aff, 1, 2), cap)
    xe = jax.vmap(lambda hb, ib: hb[ib])(h, idx)

    def expert(args):
        xk, wg, wu, wd = args
        return (jax.nn.silu(xk @ wg) * (xk @ wu)) @ wd

    ye = lax.map(expert, (jnp.swapaxes(xe, 0, 1), w_gate, w_up, w_down))
    ye = jnp.swapaxes(ye, 0, 1) * g[..., None].astype(h.dtype)
    return jax.vmap(lambda yb, ib: jnp.zeros((n, D), yb.dtype).at[ib.reshape(-1)].add(yb.reshape(-1, D)))(ye, idx)


def setup_inputs(seed: int = 0) -> dict:
    key = jax.random.key(seed)
    ks = jax.random.split(key, 28)
    f32 = jnp.float32

    def nrm(k, shape, s):
        return jax.random.normal(k, shape, f32) * s

    a0 = jax.random.uniform(ks[14], (DEPTH, 2, LRU_WIDTH), f32, 0.9, 0.999)
    s0 = a0 ** (1.0 / LRU_C)
    return {
        "x": nrm(ks[0], (BATCH, SEQ, D_MODEL), 1.0),
        "c": nrm(ks[1], (BATCH, D_MODEL), 1.0),
        "ctx": nrm(ks[2], (BATCH, CTX_LEN, D_MODEL), 1.0),
        "c_ctx": nrm(ks[3], (D_MODEL,), 1.0),
        "w_ada": nrm(ks[4], (DEPTH, D_MODEL, N_MOD * D_MODEL), 0.5 * D_MODEL ** -0.5),
        "b_ada": nrm(ks[5], (DEPTH, N_MOD * D_MODEL), 0.02),
        "norm1_g": 1.0 + nrm(ks[6], (DEPTH, D_MODEL), 0.02),
        "norm2_g": 1.0 + nrm(ks[7], (DEPTH, D_MODEL), 0.02),
        "w_in": nrm(ks[8], (DEPTH, D_MODEL, IN_WIDTH), D_MODEL ** -0.5),
        "conv_w": nrm(ks[9], (DEPTH, CONV_W, LRU_WIDTH), CONV_W ** -0.5),
        "conv_b": nrm(ks[10], (DEPTH, LRU_WIDTH), 0.02),
        "lru_wa": nrm(ks[11], (DEPTH, 2, LRU_BLOCKS, LRU_BLOCK_W, LRU_BLOCK_W), LRU_BLOCK_W ** -0.5),
        "lru_ba": nrm(ks[12], (DEPTH, 2, LRU_BLOCKS, LRU_BLOCK_W), 0.02),
        "lru_wi": nrm(ks[13], (DEPTH, 2, LRU_BLOCKS, LRU_BLOCK_W, LRU_BLOCK_W), LRU_BLOCK_W ** -0.5),
        "lru_bi": nrm(ks[15], (DEPTH, 2, LRU_BLOCKS, LRU_BLOCK_W), 0.02),
        "lru_lambda": jnp.log(s0) - jnp.log1p(-s0),
        "q_norm_g": 1.0 + nrm(ks[16], (DEPTH, HEAD_DIM), 0.02),
        "k_norm_g": 1.0 + nrm(ks[17], (DEPTH, HEAD_DIM), 0.02),
        "lambda_q1": nrm(ks[18], (DEPTH, HEAD_DIM), 0.1),
        "lambda_k1": nrm(ks[19], (DEPTH, HEAD_DIM), 0.1),
        "lambda_q2": nrm(ks[20], (DEPTH, HEAD_DIM), 0.1),
        "lambda_k2": nrm(ks[21], (DEPTH, HEAD_DIM), 0.1),
        "subln_g": 1.0 + nrm(ks[22], (DEPTH, V_DIM), 0.02),
        "w_out": nrm(ks[23], (DEPTH, MIX_WIDTH, D_MODEL), MIX_WIDTH ** -0.5),
        "w_router": nrm(ks[24], (DEPTH, D_MODEL, N_EXPERTS), D_MODEL ** -0.5),
        "w_gate": nrm(ks[25], (DEPTH, N_EXPERTS, D_MODEL, D_EXPERT), D_MODEL ** -0.5),
        "w_up": nrm(ks[26], (DEPTH, N_EXPERTS, D_MODEL, D_EXPERT), D_MODEL ** -0.5),
        "w_down": nrm(ks[27], (DEPTH, N_EXPERTS, D_EXPERT, D_MODEL), D_EXPERT ** -0.5),
    }


def reference(x, c, ctx, c_ctx, w_ada, b_ada, norm1_g, norm2_g, w_in, conv_w, conv_b,
              lru_wa, lru_ba, lru_wi, lru_bi, lru_lambda, q_norm_g, k_norm_g,
              lambda_q1, lambda_k1, lambda_q2, lambda_k2, subln_g, w_out,
              w_router, w_gate, w_up, w_down):
    n = x.shape[1]
    rope = axial_rope_tables(n)
    for layer in range(DEPTH):
        last = layer == DEPTH - 1
        lam_init = 0.8 - 0.6 * math.exp(-0.3 * layer)
        mod_l = adaln(c, w_ada[layer], b_ada[layer], N_MOD)
        mod_c = adaln(c_ctx[None], w_ada[layer], b_ada[layer], 2 if last else N_MOD)
        hl = modulate(rmsnorm(x, norm1_g[layer]), mod_l[:, 0], mod_l[:, 1])
        hc = modulate(rmsnorm(ctx, norm1_g[layer]), mod_c[:, 0], mod_c[:, 1])
        mix_l, mix_c = hybrid_mixer(hl, hc, rope, w_in[layer], conv_w[layer], conv_b[layer],
                                    lru_wa[layer], lru_ba[layer], lru_wi[layer], lru_bi[layer],
                                    lru_lambda[layer], q_norm_g[layer], k_norm_g[layer],
                                    lambda_q1[layer], lambda_k1[layer], lambda_q2[layer],
                                    lambda_k2[layer], subln_g[layer], lam_init, not last)
        x = x + mod_l[:, 2][:, None] * (mix_l @ w_out[layer])
        h2 = modulate(rmsnorm(x, norm2_g[layer]), mod_l[:, 3], mod_l[:, 4])
        x = x + mod_l[:, 5][:, None] * expert_choice_ffn(h2, w_router[layer], w_gate[layer],
                                                         w_up[layer], w_down[layer])
        if not last:
            ctx = ctx + mod_c[:, 2][:, None] * (mix_c @ w_out[layer])
            h2c = modulate(rmsnorm(ctx, norm2_g[layer]), mod_c[:, 3], mod_c[:, 4])
            ctx = ctx + mod_c[:, 5][:, None] * expert_choice_ffn(h2c, w_router[layer], w_gate[layer],
                                                                 w_up[layer], w_down[layer])
    return x
```

```python
import functools

import jax
import jax.numpy as jnp
from jax import lax
from jax.experimental import pallas as pl
from jax.experimental.pallas import tpu as pltpu

F32 = jnp.float32
BF16 = jnp.bfloat16

EPS = 1e-6
GRID_W = 64
LRU_WIDTH = 512
LRU_BLOCKS = 8
LRU_C = 8.0
CONV_W = 4
ATT_HEADS = 4
HEAD_DIM = 64
V_DIM = 2 * HEAD_DIM
QK_WIDTH = ATT_HEADS * 2 * HEAD_DIM
ATT_WIDTH = ATT_HEADS * V_DIM
ROPE_PAIRS = HEAD_DIM // 4
ROPE_BASE = 10000.0
N_EXPERTS = 16
EC_FACTOR = 2
N_MOD = 6
LAM_INIT = 0.2

LANES = 128
SUBLANES = 8
MXU_DIM = 256
VMEM_LIMIT = 56 * 1024 * 1024


def _cparams(sem):
    return pltpu.CompilerParams(dimension_semantics=sem, vmem_limit_bytes=VMEM_LIMIT)


def _dot(a, b):
    return jnp.dot(a, b, preferred_element_type=F32)


def _dot_nt(a, b):
    return lax.dot_general(a, b, (((1,), (1,)), ((), ())), preferred_element_type=F32)


def _split_bf16(x):
    hi = x.astype(BF16)
    lo = (x - hi.astype(F32)).astype(BF16)
    return hi, lo


def _adaln_kernel(c_ref, w_ref, b_ref, o_ref):
    c = c_ref[...]
    s = c * jax.nn.sigmoid(c)
    s_hi, s_lo = _split_bf16(s)
    w_hi, w_lo = _split_bf16(w_ref[...])
    o_ref[...] = _dot(s_hi, w_hi) + _dot(s_hi, w_lo) + _dot(s_lo, w_hi) + b_ref[...]


def _adaln(cvec, w, b):
    rows, d = cvec.shape
    cols = w.shape[1]
    tn = cols // 4
    return pl.pallas_call(
        _adaln_kernel,
        grid=(cols // tn,),
        in_specs=[pl.BlockSpec((rows, d), lambda j: (0, 0)),
                  pl.BlockSpec((d, tn), lambda j: (0, j)),
                  pl.BlockSpec((1, tn), lambda j: (0, j))],
        out_specs=pl.BlockSpec((rows, tn), lambda j: (0, j)),
        out_shape=jax.ShapeDtypeStruct((rows, cols), F32),
        compiler_params=_cparams(("arbitrary",)),
        name="adaln",
    )(cvec, w, b)


def _swap_halves16(x):
    lane = lax.broadcasted_iota(jnp.int32, x.shape, 1)
    first = (lane % 32) < 16
    return jnp.where(first, pltpu.roll(x, LANES - 16, 1), pltpu.roll(x, 16, 1))


def _qk_norm(t, g, ones_bd):
    outs = []
    for c in range(t.shape[1] // MXU_DIM):
        tc = t[:, c * MXU_DIM:(c + 1) * MXU_DIM]
        hi, lo = _split_bf16(tc * tc)
        ssum = _dot(hi, ones_bd) + _dot(lo, ones_bd)
        outs.append(tc * lax.rsqrt(ssum * (1.0 / HEAD_DIM) + EPS) * g[:, c * MXU_DIM:(c + 1) * MXU_DIM])
    return outs


def _inproj_kernel(x_ref, shift_ref, scale_ref, g1_ref, w_ref, qg_ref, kg_ref, cos_ref, sin_ref, ones_ref,
                   xl_ref, gate_ref, q_ref, k_ref, v_ref, *, use_rope):
    x = x_ref[0]
    ms = jnp.mean(x * x, axis=-1, keepdims=True)
    h = x * lax.rsqrt(ms + EPS) * g1_ref[...]
    h = h * (1.0 + scale_ref[0]) + shift_ref[0]
    p = _dot(h.astype(BF16), w_ref[...])
    o1, o2, o3, o4 = LRU_WIDTH, 2 * LRU_WIDTH, 2 * LRU_WIDTH + QK_WIDTH, 2 * LRU_WIDTH + 2 * QK_WIDTH
    xl_ref[0] = p[:, :o1]
    gate_ref[0] = p[:, o1:o2]
    v_ref[0] = p[:, o4:].astype(BF16)
    ones_bd = ones_ref[...]
    qn = _qk_norm(p[:, o2:o3], qg_ref[...], ones_bd)
    kn = _qk_norm(p[:, o3:o4], kg_ref[...], ones_bd)
    scale = HEAD_DIM ** -0.5
    for src, dst, mul in ((qn, q_ref, scale), (kn, k_ref, 1.0)):
        for c, tc in enumerate(src):
            for hh in range(MXU_DIM // LANES):
                th = tc[:, hh * LANES:(hh + 1) * LANES]
                if use_rope:
                    th = th * cos_ref[...] + _swap_halves16(th) * sin_ref[...]
                col = c * MXU_DIM + hh * LANES
                dst[0, :, col:col + LANES] = (th * mul).astype(BF16)


def _inproj(x, shift, scale, g1, w_bf, qg, kg, cos_t, sin_t, ones_bd, use_rope):
    bsz, n, d = x.shape
    tm = min(512, n)
    wid = w_bf.shape[1]
    per_b = shift.shape[0] > 1
    mod_map = (lambda b, i: (b, 0, 0)) if per_b else (lambda b, i: (0, 0, 0))
    full = lambda b, i: (0, 0)
    tok = lambda b, i: (b, i, 0)
    out_shapes = (jax.ShapeDtypeStruct((bsz, n, LRU_WIDTH), F32),
                  jax.ShapeDtypeStruct((bsz, n, LRU_WIDTH), F32),
                  jax.ShapeDtypeStruct((bsz, n, QK_WIDTH), BF16),
                  jax.ShapeDtypeStruct((bsz, n, QK_WIDTH), BF16),
                  jax.ShapeDtypeStruct((bsz, n, ATT_WIDTH), BF16))
    return pl.pallas_call(
        functools.partial(_inproj_kernel, use_rope=use_rope),
        grid=(bsz, n // tm),
        in_specs=[pl.BlockSpec((1, tm, d), tok),
                  pl.BlockSpec((1, 1, d), mod_map),
                  pl.BlockSpec((1, 1, d), mod_map),
                  pl.BlockSpec((1, d), full),
                  pl.BlockSpec((d, wid), full),
                  pl.BlockSpec((1, QK_WIDTH), full),
                  pl.BlockSpec((1, QK_WIDTH), full),
                  pl.BlockSpec((tm, LANES), lambda b, i: (i, 0)),
                  pl.BlockSpec((tm, LANES), lambda b, i: (i, 0)),
                  pl.BlockSpec((MXU_DIM, MXU_DIM), full)],
        out_specs=[pl.BlockSpec((1, tm, LRU_WIDTH), tok),
                   pl.BlockSpec((1, tm, LRU_WIDTH), tok),
                   pl.BlockSpec((1, tm, QK_WIDTH), tok),
                   pl.BlockSpec((1, tm, QK_WIDTH), tok),
                   pl.BlockSpec((1, tm, ATT_WIDTH), tok)],
        out_shape=out_shapes,
        compiler_params=_cparams(("parallel", "parallel")),
        name="inproj_rope" if use_rope else "inproj_ctx",
    )(x, shift, scale, g1, w_bf, qg, kg, cos_t, sin_t, ones_bd)


LRU_HALF = LRU_WIDTH // 2
LRU_CHUNK = 128


def _conv_chunk(x_ref, t0, rows, total, cw, cb):
    x = x_ref[0, pl.ds(t0, rows), :]
    prev_start = pl.multiple_of(jnp.maximum(t0 - SUBLANES, 0), SUBLANES)
    next_start = pl.multiple_of(jnp.minimum(t0 + rows, total - SUBLANES), SUBLANES)
    prev = jnp.where(t0 > 0, x_ref[0, pl.ds(prev_start, SUBLANES), :], 0.0)
    nxt = jnp.where(t0 + rows < total, x_ref[0, pl.ds(next_start, SUBLANES), :], 0.0)
    xe = jnp.concatenate([prev, x, nxt], axis=0)
    acc = cb
    for k in range(CONV_W):
        off = SUBLANES - 1 + k
        acc = acc + xe[off:off + rows] * cw[k:k + 1]
    return acc


def _lru_gates(xc, w, bias, sp):
    pre = _dot(xc.astype(BF16), w) + bias
    r = jax.nn.sigmoid(pre[:, :LRU_HALF])
    i = jax.nn.sigmoid(pre[:, LRU_HALF:])
    log_a = (-LRU_C * r) * sp
    a = jnp.exp(log_a)
    mult = jnp.sqrt(-jnp.tanh(log_a) * (a * a + 1.0))
    return a, mult, i * xc


def _scan_chunk(a, u, h, reverse):
    rows = a.shape[0]
    sub = lax.broadcasted_iota(jnp.int32, a.shape, 0) % SUBLANES
    s = 1
    while s < SUBLANES:
        if reverse:
            m = sub < SUBLANES - s
            a_sh = pltpu.roll(a, rows - s, 0)
            u_sh = pltpu.roll(u, rows - s, 0)
        else:
            m = sub >= s
            a_sh = pltpu.roll(a, s, 0)
            u_sh = pltpu.roll(u, s, 0)
        u = jnp.where(m, a * u_sh + u, u)
        a = jnp.where(m, a * a_sh, a)
        s *= 2
    groups = rows // SUBLANES
    outs = [None] * groups
    order = range(groups - 1, -1, -1) if reverse else range(groups)
    for g in order:
        sl = slice(g * SUBLANES, (g + 1) * SUBLANES)
        hg = a[sl] * h + u[sl]
        h = hg[0:1] if reverse else hg[SUBLANES - 1:SUBLANES]
        outs[g] = hg
    return jnp.concatenate(outs, axis=0), h


def _rglru_kernel(xl_ref, xc_ref, gate_ref, cw_ref, cb_ref, wf_ref, wb_ref, bf_ref, bb_ref, lam_ref,
                  o_ref, hf_ref, *, n, ctx_len):
    cw = cw_ref[...]
    cb = cb_ref[...]
    tc = min(LRU_CHUNK, ctx_len)
    tl = min(LRU_CHUNK, n)

    def direction(d, w_ref, b_ref):
        reverse = d == 1
        w = w_ref[0]
        bias = b_ref[0]
        z = -lam_ref[0, pl.ds(d, 1), :]
        sp = jnp.maximum(z, 0.0) + jnp.log1p(jnp.exp(-jnp.abs(z)))
        first_row = ctx_len - 1 if reverse else 0

        def ctx_step(c, h):
            cc = (ctx_len // tc - 1 - c) if reverse else c
            t0 = pl.multiple_of(cc * tc, SUBLANES)
            xc = _conv_chunk(xc_ref, t0, tc, ctx_len, cw, cb)
            a, mult, ix = _lru_gates(xc, w, bias, sp)
            row = lax.broadcasted_iota(jnp.int32, a.shape, 0) + t0
            mult = jnp.where(row == first_row, 1.0, mult)
            _, h = _scan_chunk(a, mult * ix, h, reverse)
            return h

        h = lax.fori_loop(0, ctx_len // tc, ctx_step, jnp.zeros((1, LRU_HALF), F32))

        def lat_step(c, h):
            cc = (n // tl - 1 - c) if reverse else c
            t0 = pl.multiple_of(cc * tl, SUBLANES)
            xc = _conv_chunk(xl_ref, t0, tl, n, cw, cb)
            a, mult, ix = _lru_gates(xc, w, bias, sp)
            hs, h = _scan_chunk(a, mult * ix, h, reverse)
            if reverse:
                y = (hf_ref[pl.ds(t0, tl), :] + hs) * jax.nn.gelu(gate_ref[0, pl.ds(t0, tl), :])
                o_ref[0, pl.ds(t0, tl), :] = y.astype(BF16)
            else:
                hf_ref[pl.ds(t0, tl), :] = hs
            return h

        lax.fori_loop(0, n // tl, lat_step, h)

    direction(0, wf_ref, bf_ref)
    direction(1, wb_ref, bb_ref)


def _rglru(xl, xc, gate, conv_w, conv_b, w_f, w_b, b_f, b_b, lam):
    bsz, n, _ = xl.shape
    ctx_len = xc.shape[1]
    halves = LRU_WIDTH // LRU_HALF
    tokh = lambda b, hf: (b, 0, hf)
    return pl.pallas_call(
        functools.partial(_rglru_kernel, n=n, ctx_len=ctx_len),
        grid=(bsz, halves),
        in_specs=[pl.BlockSpec((1, n, LRU_HALF), tokh),
                  pl.BlockSpec((1, ctx_len, LRU_HALF), tokh),
                  pl.BlockSpec((1, n, LRU_HALF), tokh),
                  pl.BlockSpec((CONV_W, LRU_HALF), lambda b, hf: (0, hf)),
                  pl.BlockSpec((1, LRU_HALF), lambda b, hf: (0, hf)),
                  pl.BlockSpec((1, LRU_HALF, 2 * LRU_HALF), lambda b, hf: (hf, 0, 0)),
                  pl.BlockSpec((1, LRU_HALF, 2 * LRU_HALF), lambda b, hf: (hf, 0, 0)),
                  pl.BlockSpec((1, 1, 2 * LRU_HALF), lambda b, hf: (hf, 0, 0)),
                  pl.BlockSpec((1, 1, 2 * LRU_HALF), lambda b, hf: (hf, 0, 0)),
                  pl.BlockSpec((1, 2, LRU_HALF), lambda b, hf: (hf, 0, 0))],
        out_specs=pl.BlockSpec((1, n, LRU_HALF), tokh),
        out_shape=jax.ShapeDtypeStruct((bsz, n, LRU_WIDTH), BF16),
        scratch_shapes=[pltpu.VMEM((n, LRU_HALF), F32)],
        compiler_params=_cparams(("parallel", "parallel")),
        name="rglru",
    )(xl, xc, gate, conv_w, conv_b, w_f, w_b, b_f, b_b, lam)


def _lru_gate_weights(wa, ba, wi, bi):
    halves = LRU_WIDTH // LRU_HALF
    per = LRU_BLOCKS // halves
    bw = LRU_WIDTH // LRU_BLOCKS

    def dense(w):
        w = w.reshape(halves, per, bw, bw)
        eye = jnp.eye(per, dtype=w.dtype)
        return jnp.einsum("hpij,pq->hpiqj", w, eye).reshape(halves, per * bw, per * bw)

    w_cat = jnp.concatenate([dense(wa), dense(wi)], axis=-1).astype(BF16)
    b_cat = jnp.concatenate([ba.reshape(halves, 1, LRU_HALF), bi.reshape(halves, 1, LRU_HALF)], axis=-1)
    return w_cat, b_cat


ATT_TQ = 256
ATT_TK = 512


def _attn_kernel(q_ref, kc_ref, vc_ref, kl_ref, vl_ref, lq1_ref, lk1_ref, lq2_ref, lk2_ref, sg_ref, o_ref,
                 *, n, tk):
    q = q_ref[0]
    tq = q.shape[0]
    lane = lax.broadcasted_iota(jnp.int32, q.shape, 1)
    zero = jnp.zeros_like(q)
    q2 = jnp.concatenate([jnp.where(lane < HEAD_DIM, q, zero), jnp.where(lane >= HEAD_DIM, q, zero)], axis=0)

    def ext(v):
        ln = lax.broadcasted_iota(jnp.int32, v.shape, 1)
        return jnp.concatenate([v, jnp.where(ln == 0, 1.0, 0.0).astype(BF16)], axis=1)

    def step(k, v, carry):
        m, acc = carry
        s = _dot_nt(q2, k)
        m_new = jnp.maximum(m, jnp.max(s, axis=1, keepdims=True))
        p = jnp.exp(s - m_new).astype(BF16)
        acc = jnp.exp(m - m_new) * acc + _dot(p, ext(v))
        return m_new, acc

    carry = (jnp.full((2 * tq, 1), -1e30, F32), jnp.zeros((2 * tq, 2 * V_DIM), F32))
    carry = step(kc_ref[0], vc_ref[0], carry)

    def body(j, carry):
        t0 = pl.multiple_of(j * tk, tk)
        return step(kl_ref[0, pl.ds(t0, tk), :], vl_ref[0, pl.ds(t0, tk), :], carry)

    _, acc = lax.fori_loop(0, n // tk, body, carry)
    o = acc[:, :V_DIM] / acc[:, V_DIM:V_DIM + 1]
    lam = (jnp.exp(jnp.sum(lq1_ref[...] * lk1_ref[...], keepdims=True))
           - jnp.exp(jnp.sum(lq2_ref[...] * lk2_ref[...], keepdims=True)) + LAM_INIT)
    att = o[:tq] - lam * o[tq:]
    ms = jnp.mean(att * att, axis=-1, keepdims=True)
    y = att * lax.rsqrt(ms + EPS) * sg_ref[...]
    o_ref[0] = (y * (1.0 - LAM_INIT)).astype(BF16)


def _attention(q, kc, vc, kl, vl, lq1, lk1, lq2, lk2, sg):
    bsz, n, _ = q.shape
    ctx_len = kc.shape[1]
    tq = min(ATT_TQ, n)
    tk = min(ATT_TK, n)
    vec = lambda b, h, i: (0, 0)
    return pl.pallas_call(
        functools.partial(_attn_kernel, n=n, tk=tk),
        grid=(bsz, ATT_HEADS, n // tq),
        in_specs=[pl.BlockSpec((1, tq, V_DIM), lambda b, h, i: (b, i, h)),
                  pl.BlockSpec((1, ctx_len, V_DIM), lambda b, h, i: (b, 0, h)),
                  pl.BlockSpec((1, ctx_len, V_DIM), lambda b, h, i: (b, 0, h)),
                  pl.BlockSpec((1, n, V_DIM), lambda b, h, i: (b, 0, h)),
                  pl.BlockSpec((1, n, V_DIM), lambda b, h, i: (b, 0, h)),
                  pl.BlockSpec((1, HEAD_DIM), vec),
                  pl.BlockSpec((1, HEAD_DIM), vec),
                  pl.BlockSpec((1, HEAD_DIM), vec),
                  pl.BlockSpec((1, HEAD_DIM), vec),
                  pl.BlockSpec((1, V_DIM), vec)],
        out_specs=pl.BlockSpec((1, tq, V_DIM), lambda b, h, i: (b, i, h)),
        out_shape=jax.ShapeDtypeStruct((bsz, n, ATT_WIDTH), BF16),
        compiler_params=_cparams(("parallel", "parallel", "arbitrary")),
        name="diff_attention",
    )(q, kc, vc, kl, vl, lq1, lk1, lq2, lk2, sg)


def _outproj_kernel(lru_ref, att_ref, x_ref, g1_ref, shift_ref, scale_ref, n2_ref, wo_ref, wr_ref,
                    x1_ref, h2_ref, lg_ref):
    mix = _dot(lru_ref[0], wo_ref[:LRU_WIDTH, :]) + _dot(att_ref[0], wo_ref[LRU_WIDTH:, :])
    x1 = x_ref[0] + g1_ref[0] * mix
    x1_ref[0] = x1
    ms = jnp.mean(x1 * x1, axis=-1, keepdims=True)
    h2 = x1 * lax.rsqrt(ms + EPS) * n2_ref[...]
    h2 = h2 * (1.0 + scale_ref[0]) + shift_ref[0]
    h2_ref[0] = h2
    lg_ref[0] = _dot(h2.astype(BF16), wr_ref[...])[:, :N_EXPERTS]


def _outproj(lru, att, x, g1, shift2, scale2, n2g, wo_bf, wr_bf):
    bsz, n, d = x.shape
    tm = min(512, n)
    tok = lambda b, i: (b, i, 0)
    mod = lambda b, i: (b, 0, 0)
    full = lambda b, i: (0, 0)
    return pl.pallas_call(
        _outproj_kernel,
        grid=(bsz, n // tm),
        in_specs=[pl.BlockSpec((1, tm, LRU_WIDTH), tok),
                  pl.BlockSpec((1, tm, ATT_WIDTH), tok),
                  pl.BlockSpec((1, tm, d), tok),
                  pl.BlockSpec((1, 1, d), mod),
                  pl.BlockSpec((1, 1, d), mod),
                  pl.BlockSpec((1, 1, d), mod),
                  pl.BlockSpec((1, d), full),
                  pl.BlockSpec((d, d), full),
                  pl.BlockSpec((d, LANES), full)],
        out_specs=[pl.BlockSpec((1, tm, d), tok),
                   pl.BlockSpec((1, tm, d), tok),
                   pl.BlockSpec((1, tm, N_EXPERTS), tok)],
        out_shape=(jax.ShapeDtypeStruct((bsz, n, d), F32),
                   jax.ShapeDtypeStruct((bsz, n, d), F32),
                   jax.ShapeDtypeStruct((bsz, n, N_EXPERTS), F32)),
        compiler_params=_cparams(("parallel", "parallel")),
        name="outproj_router",
    )(lru, att, x, g1, shift2, scale2, n2g, wo_bf, wr_bf)


def _cumsum_lanes(x):
    rows, n = x.shape
    blk = min(LANES, n)
    tri = (lax.broadcasted_iota(jnp.int32, (blk, blk), 0)
           <= lax.broadcasted_iota(jnp.int32, (blk, blk), 1)).astype(BF16)
    off = jnp.zeros((rows, 1), F32)
    outs = []
    for kb in range(n // blk):
        c = _dot(x[:, kb * blk:(kb + 1) * blk].astype(BF16), tri)
        outs.append(c + off)
        off = off + c[:, blk - 1:blk]
    return jnp.concatenate(outs, axis=1)


def _topk_kernel(lg_ref, idx_ref, g_ref, aff_ref, cs_ref, *, n, cap):
    lg = lg_ref[0]
    ex = jnp.exp(lg - jnp.max(lg, axis=0, keepdims=True))
    aff = ex / jnp.sum(ex, axis=0, keepdims=True)
    capf = float(cap)

    def bisect(_, c):
        lo, hi = c
        mid = lo + ((hi - lo) >> 1)
        cnt = jnp.sum(jnp.where(aff >= pltpu.bitcast(mid, F32), 1.0, 0.0), axis=1, keepdims=True)
        ge = cnt >= capf
        return jnp.where(ge, mid, lo), jnp.where(ge, hi, mid)

    lo0 = jnp.zeros((N_EXPERTS, 1), jnp.int32)
    hi0 = jnp.full((N_EXPERTS, 1), 0x3F800001, jnp.int32)
    thr, nxt = lax.fori_loop(0, 31, bisect, (lo0, hi0))
    gt = aff >= pltpu.bitcast(nxt, F32)
    eq = jnp.logical_and(aff >= pltpu.bitcast(thr, F32), jnp.logical_not(gt))
    need = capf - jnp.sum(jnp.where(gt, 1.0, 0.0), axis=1, keepdims=True)
    ceq = _cumsum_lanes(jnp.where(eq, 1.0, 0.0))
    sel = jnp.where(gt, 1.0, jnp.where(eq, jnp.where(ceq <= need, 1.0, 0.0), 0.0))
    cs = _cumsum_lanes(sel)
    cs_ref[...] = cs * sel
    aff_ref[...] = aff

    jt = min(LANES, cap)
    blk = min(LANES, n)

    def per_expert(e, carry):
        crow = cs_ref[pl.ds(e, 1), :]
        arow = aff_ref[pl.ds(e, 1), :]
        for jc in range(cap // jt):
            rank = (lax.broadcasted_iota(jnp.int32, (jt, 1), 0) + (jc * jt + 1)).astype(F32)
            acc_i = jnp.zeros((jt, blk), F32)
            acc_g = jnp.zeros((jt, blk), F32)
            for tb in range(n // blk):
                hit = crow[:, tb * blk:(tb + 1) * blk] == rank
                tval = (lax.broadcasted_iota(jnp.int32, (1, blk), 1) + tb * blk).astype(F32)
                acc_i = acc_i + jnp.where(hit, tval, 0.0)
                acc_g = acc_g + jnp.where(hit, arow[:, tb * blk:(tb + 1) * blk], 0.0)
            row_i = jnp.sum(acc_i.T, axis=0, keepdims=True)
            row_g = jnp.sum(acc_g.T, axis=0, keepdims=True)
            idx_ref[0, e, :, jc * jt:(jc + 1) * jt] = row_i.astype(jnp.int32)
            g_ref[0, e, :, jc * jt:(jc + 1) * jt] = row_g
        return carry

    lax.fori_loop(0, N_EXPERTS, per_expert, 0)


def _topk(logits_t, cap):
    bsz, _, n = logits_t.shape
    return pl.pallas_call(
        functools.partial(_topk_kernel, n=n, cap=cap),
        grid=(bsz,),
        in_specs=[pl.BlockSpec((1, N_EXPERTS, n), lambda b: (b, 0, 0))],
        out_specs=[pl.BlockSpec((1, N_EXPERTS, 1, cap), lambda b: (b, 0, 0, 0)),
                   pl.BlockSpec((1, N_EXPERTS, 1, cap), lambda b: (b, 0, 0, 0))],
        out_shape=(jax.ShapeDtypeStruct((bsz, N_EXPERTS, 1, cap), jnp.int32),
                   jax.ShapeDtypeStruct((bsz, N_EXPERTS, 1, cap), F32)),
        scratch_shapes=[pltpu.VMEM((N_EXPERTS, n), F32), pltpu.VMEM((N_EXPERTS, n), F32)],
        compiler_params=_cparams(("parallel",)),
        name="expert_topk",
    )(logits_t)


def _gather_kernel(idx_ref, h_ref, o_ref, buf_ref, *, cap):
    def body(jj, carry):
        j0 = pl.multiple_of(jj * SUBLANES, SUBLANES)
        rows = [h_ref[0, pl.ds(idx_ref[0, 0, j0 + r], 1), :] for r in range(SUBLANES)]
        buf_ref[pl.ds(j0, SUBLANES), :] = jnp.concatenate(rows, axis=0)
        return carry

    lax.fori_loop(0, cap // SUBLANES, body, 0)
    o_ref[0, 0] = buf_ref[...].astype(BF16)


def _gather(idx3, h2, cap):
    bsz, n, d = h2.shape
    return pl.pallas_call(
        functools.partial(_gather_kernel, cap=cap),
        grid=(bsz, N_EXPERTS),
        in_specs=[pl.BlockSpec((1, 1, cap), lambda b, e: (b * N_EXPERTS + e, 0, 0), memory_space=pltpu.SMEM),
                  pl.BlockSpec((1, n, d), lambda b, e: (b, 0, 0))],
        out_specs=pl.BlockSpec((1, 1, cap, d), lambda b, e: (e, b, 0, 0)),
        out_shape=jax.ShapeDtypeStruct((N_EXPERTS, bsz, cap, d), BF16),
        scratch_shapes=[pltpu.VMEM((cap, d), F32)],
        compiler_params=_cparams(("parallel", "arbitrary")),
        name="expert_gather",
    )(idx3, h2)


FFN_TF = 256


def _ffn_kernel(x_ref, wg_ref, wu_ref, wd_ref, o_ref):
    f = pl.program_id(2)
    x = x_ref[0]
    hg = _dot(x, wg_ref[0].astype(BF16))
    hu = _dot(x, wu_ref[0].astype(BF16))
    h = (hg * jax.nn.sigmoid(hg)) * hu
    y = _dot(h.astype(BF16), wd_ref[0].astype(BF16))

    @pl.when(f == 0)
    def _():
        o_ref[0] = y

    @pl.when(f > 0)
    def _():
        o_ref[0] += y


def _ffn(xe, w_gate, w_up, w_down):
    ne, m, d = xe.shape
    dff = w_gate.shape[2]
    tm = min(2048, m)
    tf = FFN_TF
    return pl.pallas_call(
        _ffn_kernel,
        grid=(ne, m // tm, dff // tf),
        in_specs=[pl.BlockSpec((1, tm, d), lambda e, i, f: (e, i, 0)),
                  pl.BlockSpec((1, d, tf), lambda e, i, f: (e, 0, f)),
                  pl.BlockSpec((1, d, tf), lambda e, i, f: (e, 0, f)),
                  pl.BlockSpec((1, tf, d), lambda e, i, f: (e, f, 0))],
        out_specs=pl.BlockSpec((1, tm, d), lambda e, i, f: (e, i, 0)),
        out_shape=jax.ShapeDtypeStruct((ne, m, d), F32),
        compiler_params=_cparams(("parallel", "parallel", "arbitrary")),
        name="expert_ffn",
    )(xe, w_gate, w_up, w_down)


def _combine_kernel(idx_ref, g_ref, x1_ref, gate_ref, y_ref, o_ref, *, cap):
    e = pl.program_id(1)

    @pl.when(e == 0)
    def _():
        o_ref[...] = x1_ref[...]

    gate = gate_ref[0]

    sub = lax.broadcasted_iota(jnp.int32, (SUBLANES, gate.shape[1]), 0)

    def body(jj, carry):
        j0 = pl.multiple_of(jj * SUBLANES, SUBLANES)
        ys = y_ref[0, 0, pl.ds(j0, SUBLANES), :]
        for r in range(SUBLANES):
            t = idx_ref[0, 0, j0 + r]
            w = g_ref[0, 0, j0 + r]
            base = pl.multiple_of((t >> 3) << 3, SUBLANES)
            upd = jnp.where(sub == (t & (SUBLANES - 1)), (w * gate) * ys[r:r + 1], 0.0)
            o_ref[0, pl.ds(base, SUBLANES), :] += upd
        return carry

    lax.fori_loop(0, cap // SUBLANES, body, 0)


def _combine(idx3, g3, x1, gate2, ye, cap):
    bsz, n, d = x1.shape
    smem = lambda b, e: (b * N_EXPERTS + e, 0, 0)
    resident = lambda b, e: (b, 0, 0)
    return pl.pallas_call(
        functools.partial(_combine_kernel, cap=cap),
        grid=(bsz, N_EXPERTS),
        in_specs=[pl.BlockSpec((1, 1, cap), smem, memory_space=pltpu.SMEM),
                  pl.BlockSpec((1, 1, cap), smem, memory_space=pltpu.SMEM),
                  pl.BlockSpec((1, n, d), resident, pipeline_mode=pl.Buffered(1)),
                  pl.BlockSpec((1, 1, d), resident),
                  pl.BlockSpec((1, 1, cap, d), lambda b, e: (e, b, 0, 0))],
        out_specs=pl.BlockSpec((1, n, d), resident, pipeline_mode=pl.Buffered(1)),
        out_shape=jax.ShapeDtypeStruct((bsz, n, d), F32),
        compiler_params=_cparams(("parallel", "arbitrary")),
        name="expert_combine",
    )(idx3, g3, x1, gate2, ye)


def _rope_tables(n):
    rows = n // GRID_W
    row = jnp.repeat(jnp.arange(rows), GRID_W).astype(F32)
    col = jnp.tile(jnp.arange(GRID_W), rows).astype(F32)
    inv = ROPE_BASE ** (-jnp.arange(ROPE_PAIRS, dtype=F32) / ROPE_PAIRS)
    ang_r = row[:, None] * inv
    ang_c = col[:, None] * inv
    cos64 = jnp.concatenate([jnp.cos(ang_r), jnp.cos(ang_r), jnp.cos(ang_c), jnp.cos(ang_c)], axis=1)
    sin64 = jnp.concatenate([-jnp.sin(ang_r), jnp.sin(ang_r), -jnp.sin(ang_c), jnp.sin(ang_c)], axis=1)
    return jnp.tile(cos64, (1, 2)), jnp.tile(sin64, (1, 2))


def kernel(x, c, ctx, c_ctx, w_ada, b_ada, norm1_g, norm2_g, w_in, conv_w, conv_b, lru_wa, lru_ba, lru_wi,
           lru_bi, lru_lambda, q_norm_g, k_norm_g, lambda_q1, lambda_k1, lambda_q2, lambda_k2, subln_g, w_out,
           w_router, w_gate, w_up, w_down):
    assert w_ada.shape[0] == 1, "single-layer configuration"
    bsz, n, d = x.shape
    ctx_len = ctx.shape[1]
    cap = EC_FACTOR * n // N_EXPERTS

    rows = ((bsz + 1 + SUBLANES - 1) // SUBLANES) * SUBLANES
    cvec = jnp.zeros((rows, d), F32).at[:bsz].set(c).at[bsz].set(c_ctx)
    mod = _adaln(cvec, w_ada[0], b_ada[0][None]).reshape(rows, N_MOD, 1, d)
    mod_l = mod[:bsz]
    mod_c = mod[bsz:bsz + 1]

    w_in_bf = w_in[0].astype(BF16)
    w_out_bf = w_out[0].astype(BF16)
    w_r_bf = jnp.zeros((d, LANES), BF16).at[:, :N_EXPERTS].set(w_router[0].astype(BF16))
    qg = jnp.tile(q_norm_g[0], QK_WIDTH // HEAD_DIM)[None]
    kg = jnp.tile(k_norm_g[0], QK_WIDTH // HEAD_DIM)[None]
    seg = jnp.arange(MXU_DIM) // HEAD_DIM
    ones_bd = (seg[:, None] == seg[None, :]).astype(BF16)
    cos_t, sin_t = _rope_tables(n)
    cos_c = jnp.ones((ctx_len, LANES), F32)
    sin_c = jnp.zeros((ctx_len, LANES), F32)
    g1 = norm1_g[0][None]

    xl, gate_l, q_l, k_l, v_l = _inproj(x, mod_l[:, 0], mod_l[:, 1], g1, w_in_bf, qg, kg, cos_t, sin_t,
                                        ones_bd, True)
    xc, _, _, k_c, v_c = _inproj(ctx, mod_c[:, 0], mod_c[:, 1], g1, w_in_bf, qg, kg, cos_c, sin_c,
                                 ones_bd, False)

    w_f, b_f = _lru_gate_weights(lru_wa[0, 0], lru_ba[0, 0], lru_wi[0, 0], lru_bi[0, 0])
    w_b, b_b = _lru_gate_weights(lru_wa[0, 1], lru_ba[0, 1], lru_wi[0, 1], lru_bi[0, 1])
    halves = LRU_WIDTH // LRU_HALF
    lam_h = lru_lambda[0].reshape(2, halves, LRU_HALF).transpose(1, 0, 2)
    lru = _rglru(xl, xc, gate_l, conv_w[0], conv_b[0][None], w_f, w_b, b_f, b_b, lam_h)

    att = _attention(q_l, k_c, v_c, k_l, v_l, lambda_q1, lambda_k1, lambda_q2, lambda_k2, subln_g)

    x1, h2, logits = _outproj(lru, att, x, mod_l[:, 2], mod_l[:, 3], mod_l[:, 4], norm2_g[0][None],
                              w_out_bf, w_r_bf)

    idx, gates = _topk(jnp.swapaxes(logits, 1, 2), cap)
    idx3 = idx.reshape(bsz * N_EXPERTS, 1, cap)
    g3 = gates.reshape(bsz * N_EXPERTS, 1, cap)
    xe = _gather(idx3, h2, cap)
    ye = _ffn(xe.reshape(N_EXPERTS, bsz * cap, d), w_gate[0], w_up[0], w_down[0])
    return _combine(idx3, g3, x1, mod_l[:, 5], ye.reshape(N_EXPERTS, bsz, cap, d), cap)
```

```python
import functools

import jax
import jax.numpy as jnp
from jax import lax
from jax.experimental import pallas as pl
from jax.experimental.pallas import tpu as pltpu

F32 = jnp.float32
BF16 = jnp.bfloat16

EPS = 1e-6
GRID_W = 64
LRU_WIDTH = 512
LRU_BLOCKS = 8
LRU_C = 8.0
CONV_W = 4
ATT_HEADS = 4
HEAD_DIM = 64
V_DIM = 2 * HEAD_DIM
QK_WIDTH = ATT_HEADS * 2 * HEAD_DIM
ATT_WIDTH = ATT_HEADS * V_DIM
ROPE_PAIRS = HEAD_DIM // 4
ROPE_BASE = 10000.0
N_EXPERTS = 16
EC_FACTOR = 2
N_MOD = 6
LAM_INIT = 0.2
LOG2_E = 1.4426950408889634

LANES = 128
SUBLANES = 8
MXU_DIM = 256
VMEM_LIMIT = 56 * 1024 * 1024


def _cparams(sem):
    return pltpu.CompilerParams(dimension_semantics=sem, vmem_limit_bytes=VMEM_LIMIT)


def _dot(a, b):
    return jnp.dot(a, b, preferred_element_type=F32)


def _dot_nt(a, b):
    return lax.dot_general(a, b, (((1,), (1,)), ((), ())), preferred_element_type=F32)


def _split_bf16(x):
    hi = x.astype(BF16)
    lo = (x - hi.astype(F32)).astype(BF16)
    return hi, lo


def _adaln_kernel(c_ref, w_ref, b_ref, o_ref):
    c = c_ref[...]
    s = c * jax.nn.sigmoid(c)
    s_hi, s_lo = _split_bf16(s)
    w_hi, w_lo = _split_bf16(w_ref[...])
    o_ref[...] = _dot(s_hi, w_hi) + _dot(s_hi, w_lo) + _dot(s_lo, w_hi) + b_ref[...]


def _adaln(cvec, w, b):
    rows, d = cvec.shape
    cols = w.shape[1]
    tn = cols // 4
    return pl.pallas_call(
        _adaln_kernel,
        grid=(cols // tn,),
        in_specs=[pl.BlockSpec((rows, d), lambda j: (0, 0)),
                  pl.BlockSpec((d, tn), lambda j: (0, j)),
                  pl.BlockSpec((1, tn), lambda j: (0, j))],
        out_specs=pl.BlockSpec((rows, tn), lambda j: (0, j)),
        out_shape=jax.ShapeDtypeStruct((rows, cols), F32),
        compiler_params=_cparams(("arbitrary",)),
        name="adaln",
    )(cvec, w, b)


def _swap_halves16(x):
    lane = lax.broadcasted_iota(jnp.int32, x.shape, 1)
    first = (lane % 32) < 16
    return jnp.where(first, pltpu.roll(x, LANES - 16, 1), pltpu.roll(x, 16, 1))


def _qk_norm(t, g, ones_bd):
    outs = []
    for c in range(t.shape[1] // MXU_DIM):
        tc = t[:, c * MXU_DIM:(c + 1) * MXU_DIM]
        hi, lo = _split_bf16(tc * tc)
        ssum = _dot(hi, ones_bd) + _dot(lo, ones_bd)
        outs.append(tc * lax.rsqrt(ssum * (1.0 / HEAD_DIM) + EPS) * g[:, c * MXU_DIM:(c + 1) * MXU_DIM])
    return outs


def _inproj_kernel(x_ref, shift_ref, scale_ref, g1_ref, w_ref, qg_ref, kg_ref, cos_ref, sin_ref, ones_ref,
                   xl_ref, gate_ref, q_ref, k_ref, v_ref, *, use_rope):
    x = x_ref[0]
    ms = jnp.mean(x * x, axis=-1, keepdims=True)
    h = x * lax.rsqrt(ms + EPS) * g1_ref[...]
    h = h * (1.0 + scale_ref[0]) + shift_ref[0]
    p = _dot(h.astype(BF16), w_ref[...])
    o1, o2, o3, o4 = LRU_WIDTH, 2 * LRU_WIDTH, 2 * LRU_WIDTH + QK_WIDTH, 2 * LRU_WIDTH + 2 * QK_WIDTH
    xl_ref[0] = p[:, :o1]
    gate_ref[0] = p[:, o1:o2]
    v_ref[0] = p[:, o4:].astype(BF16)
    ones_bd = ones_ref[...]
    qn = _qk_norm(p[:, o2:o3], qg_ref[...], ones_bd)
    kn = _qk_norm(p[:, o3:o4], kg_ref[...], ones_bd)
    scale = HEAD_DIM ** -0.5 * LOG2_E
    for src, dst, mul in ((qn, q_ref, scale), (kn, k_ref, 1.0)):
        for c, tc in enumerate(src):
            for hh in range(MXU_DIM // LANES):
                th = tc[:, hh * LANES:(hh + 1) * LANES]
                if use_rope:
                    th = th * cos_ref[...] + _swap_halves16(th) * sin_ref[...]
                col = c * MXU_DIM + hh * LANES
                dst[0, :, col:col + LANES] = (th * mul).astype(BF16)


def _inproj(x, shift, scale, g1, w_bf, qg, kg, cos_t, sin_t, ones_bd, use_rope):
    bsz, n, d = x.shape
    tm = min(512, n)
    wid = w_bf.shape[1]
    per_b = shift.shape[0] > 1
    mod_map = (lambda b, i: (b, 0, 0)) if per_b else (lambda b, i: (0, 0, 0))
    full = lambda b, i: (0, 0)
    tok = lambda b, i: (b, i, 0)
    out_shapes = (jax.ShapeDtypeStruct((bsz, n, LRU_WIDTH), F32),
                  jax.ShapeDtypeStruct((bsz, n, LRU_WIDTH), F32),
                  jax.ShapeDtypeStruct((bsz, n, QK_WIDTH), BF16),
                  jax.ShapeDtypeStruct((bsz, n, QK_WIDTH), BF16),
                  jax.ShapeDtypeStruct((bsz, n, ATT_WIDTH), BF16))
    return pl.pallas_call(
        functools.partial(_inproj_kernel, use_rope=use_rope),
        grid=(bsz, n // tm),
        in_specs=[pl.BlockSpec((1, tm, d), tok),
                  pl.BlockSpec((1, 1, d), mod_map),
                  pl.BlockSpec((1, 1, d), mod_map),
                  pl.BlockSpec((1, d), full),
                  pl.BlockSpec((d, wid), full),
                  pl.BlockSpec((1, QK_WIDTH), full),
                  pl.BlockSpec((1, QK_WIDTH), full),
                  pl.BlockSpec((tm, LANES), lambda b, i: (i, 0)),
                  pl.BlockSpec((tm, LANES), lambda b, i: (i, 0)),
                  pl.BlockSpec((MXU_DIM, MXU_DIM), full)],
        out_specs=[pl.BlockSpec((1, tm, LRU_WIDTH), tok),
                   pl.BlockSpec((1, tm, LRU_WIDTH), tok),
                   pl.BlockSpec((1, tm, QK_WIDTH), tok),
                   pl.BlockSpec((1, tm, QK_WIDTH), tok),
                   pl.BlockSpec((1, tm, ATT_WIDTH), tok)],
        out_shape=out_shapes,
        compiler_params=_cparams(("parallel", "parallel")),
        name="inproj_rope" if use_rope else "inproj_ctx",
    )(x, shift, scale, g1, w_bf, qg, kg, cos_t, sin_t, ones_bd)


LRU_HALF = LRU_WIDTH // 2
LRU_CHUNK = 512


def _conv_chunk(x_ref, t0, rows, total, cw, cb):
    x = x_ref[0, pl.ds(t0, rows), :]
    prev_start = pl.multiple_of(jnp.maximum(t0 - SUBLANES, 0), SUBLANES)
    next_start = pl.multiple_of(jnp.minimum(t0 + rows, total - SUBLANES), SUBLANES)
    prev = jnp.where(t0 > 0, x_ref[0, pl.ds(prev_start, SUBLANES), :], 0.0)
    nxt = jnp.where(t0 + rows < total, x_ref[0, pl.ds(next_start, SUBLANES), :], 0.0)
    xe = jnp.concatenate([prev, x, nxt], axis=0)
    acc = cb
    for k in range(CONV_W):
        off = SUBLANES - 1 + k
        acc = acc + xe[off:off + rows] * cw[k:k + 1]
    return acc


def _sigmoid(x):
    return 0.5 * jnp.tanh(0.5 * x) + 0.5


def _lru_gates(xc, w, bias, sp):
    pre = _dot(xc.astype(BF16), w) + bias
    r = _sigmoid(pre[:, :LRU_HALF])
    i = _sigmoid(pre[:, LRU_HALF:])
    log_a = (-LRU_C * r) * sp
    a = jnp.exp(log_a)
    mult = jnp.sqrt(-jnp.tanh(log_a) * (a * a + 1.0))
    return a, mult, i * xc


def _scan_chunk(a, u, h, reverse):
    rows, width = a.shape
    groups = rows // SUBLANES
    a = a.reshape(groups, SUBLANES, width)
    u = u.reshape(groups, SUBLANES, width)
    sub = lax.broadcasted_iota(jnp.int32, a.shape, 1)
    s = 1
    while s < SUBLANES:
        if reverse:
            m = sub < SUBLANES - s
            a_sh = pltpu.roll(a, SUBLANES - s, 1)
            u_sh = pltpu.roll(u, SUBLANES - s, 1)
        else:
            m = sub >= s
            a_sh = pltpu.roll(a, s, 1)
            u_sh = pltpu.roll(u, s, 1)
        u = jnp.where(m, a * u_sh + u, u)
        a = jnp.where(m, a * a_sh, a)
        s *= 2
    outs = [None] * groups
    order = range(groups - 1, -1, -1) if reverse else range(groups)
    for g in order:
        hg = a[g] * h + u[g]
        h = hg[0:1] if reverse else hg[SUBLANES - 1:SUBLANES]
        outs[g] = hg
    return jnp.concatenate(outs, axis=0), h


def _rglru_kernel(xl_ref, xc_ref, gate_ref, cw_ref, cb_ref, wf_ref, wb_ref, bf_ref, bb_ref, lam_ref,
                  o_ref, hf_ref, cl_ref, cc_ref, *, n, ctx_len):
    cw = cw_ref[...]
    cb = cb_ref[...]
    tc = min(LRU_CHUNK, ctx_len)
    tl = min(LRU_CHUNK, n)

    def direction(d, w_ref, b_ref):
        reverse = d == 1
        w = w_ref[0]
        bias = b_ref[0]
        z = -lam_ref[0, pl.ds(d, 1), :]
        sp = jnp.maximum(z, 0.0) + jnp.log1p(jnp.exp(-jnp.abs(z)))
        first_row = ctx_len - 1 if reverse else 0

        def conv(src_ref, cache_ref, t0, rows, total):
            if reverse:
                return cache_ref[pl.ds(t0, rows), :]
            xc = _conv_chunk(src_ref, t0, rows, total, cw, cb)
            cache_ref[pl.ds(t0, rows), :] = xc
            return xc

        def ctx_step(c, h):
            cc = (ctx_len // tc - 1 - c) if reverse else c
            t0 = pl.multiple_of(cc * tc, SUBLANES)
            xc = conv(xc_ref, cc_ref, t0, tc, ctx_len)
            a, mult, ix = _lru_gates(xc, w, bias, sp)
            row = lax.broadcasted_iota(jnp.int32, a.shape, 0) + t0
            mult = jnp.where(row == first_row, 1.0, mult)
            _, h = _scan_chunk(a, mult * ix, h, reverse)
            return h

        h = lax.fori_loop(0, ctx_len // tc, ctx_step, jnp.zeros((1, LRU_HALF), F32))

        def lat_step(c, h):
            cc = (n // tl - 1 - c) if reverse else c
            t0 = pl.multiple_of(cc * tl, SUBLANES)
            xc = conv(xl_ref, cl_ref, t0, tl, n)
            a, mult, ix = _lru_gates(xc, w, bias, sp)
            hs, h = _scan_chunk(a, mult * ix, h, reverse)
            if reverse:
                y = (hf_ref[pl.ds(t0, tl), :] + hs) * jax.nn.gelu(gate_ref[0, pl.ds(t0, tl), :])
                o_ref[0, pl.ds(t0, tl), :] = y.astype(BF16)
            else:
                hf_ref[pl.ds(t0, tl), :] = hs
            return h

        lax.fori_loop(0, n // tl, lat_step, h)

    direction(0, wf_ref, bf_ref)
    direction(1, wb_ref, bb_ref)


def _rglru(xl, xc, gate, conv_w, conv_b, w_f, w_b, b_f, b_b, lam):
    bsz, n, _ = xl.shape
    ctx_len = xc.shape[1]
    halves = LRU_WIDTH // LRU_HALF
    tokh = lambda b, hf: (b, 0, hf)
    return pl.pallas_call(
        functools.partial(_rglru_kernel, n=n, ctx_len=ctx_len),
        grid=(bsz, halves),
        in_specs=[pl.BlockSpec((1, n, LRU_HALF), tokh),
                  pl.BlockSpec((1, ctx_len, LRU_HALF), tokh),
                  pl.BlockSpec((1, n, LRU_HALF), tokh),
                  pl.BlockSpec((CONV_W, LRU_HALF), lambda b, hf: (0, hf)),
                  pl.BlockSpec((1, LRU_HALF), lambda b, hf: (0, hf)),
                  pl.BlockSpec((1, LRU_HALF, 2 * LRU_HALF), lambda b, hf: (hf, 0, 0)),
                  pl.BlockSpec((1, LRU_HALF, 2 * LRU_HALF), lambda b, hf: (hf, 0, 0)),
                  pl.BlockSpec((1, 1, 2 * LRU_HALF), lambda b, hf: (hf, 0, 0)),
                  pl.BlockSpec((1, 1, 2 * LRU_HALF), lambda b, hf: (hf, 0, 0)),
                  pl.BlockSpec((1, 2, LRU_HALF), lambda b, hf: (hf, 0, 0))],
        out_specs=pl.BlockSpec((1, n, LRU_HALF), tokh),
        out_shape=jax.ShapeDtypeStruct((bsz, n, LRU_WIDTH), BF16),
        scratch_shapes=[pltpu.VMEM((n, LRU_HALF), F32),
                        pltpu.VMEM((n, LRU_HALF), F32),
                        pltpu.VMEM((ctx_len, LRU_HALF), F32)],
        compiler_params=_cparams(("parallel", "parallel")),
        name="rglru",
    )(xl, xc, gate, conv_w, conv_b, w_f, w_b, b_f, b_b, lam)


def _lru_gate_weights(wa, ba, wi, bi):
    halves = LRU_WIDTH // LRU_HALF
    per = LRU_BLOCKS // halves
    bw = LRU_WIDTH // LRU_BLOCKS

    def dense(w):
        w = w.reshape(halves, per, bw, bw)
        eye = jnp.eye(per, dtype=w.dtype)
        return jnp.einsum("hpij,pq->hpiqj", w, eye).reshape(halves, per * bw, per * bw)

    w_cat = jnp.concatenate([dense(wa), dense(wi)], axis=-1).astype(BF16)
    b_cat = jnp.concatenate([ba.reshape(halves, 1, LRU_HALF), bi.reshape(halves, 1, LRU_HALF)], axis=-1)
    return w_cat, b_cat


ATT_TQ = 256
ATT_TK = 512


def _attn_kernel(q_ref, kc_ref, vc_ref, kl_ref, vl_ref, lq1_ref, lk1_ref, lq2_ref, lk2_ref, sg_ref, o_ref,
                 *, n, tk):
    q = q_ref[0]
    tq = q.shape[0]
    lane = lax.broadcasted_iota(jnp.int32, q.shape, 1)
    zero = jnp.zeros_like(q)
    q2 = jnp.concatenate([jnp.where(lane < HEAD_DIM, q, zero), jnp.where(lane >= HEAD_DIM, q, zero)], axis=0)

    def ext(v):
        ln = lax.broadcasted_iota(jnp.int32, v.shape, 1)
        return jnp.concatenate([v, jnp.where(ln == 0, 1.0, 0.0).astype(BF16)], axis=1)

    def step(k, v, carry):
        m, acc = carry
        s = _dot_nt(q2, k)
        m_new = jnp.maximum(m, jnp.max(s, axis=1, keepdims=True))
        p = jnp.exp2(s - m_new).astype(BF16)
        acc = jnp.exp2(m - m_new) * acc + _dot(p, ext(v))
        return m_new, acc

    carry = (jnp.full((2 * tq, 1), -1e30, F32), jnp.zeros((2 * tq, 2 * V_DIM), F32))
    carry = step(kc_ref[0], vc_ref[0], carry)
    for j in range(n // tk):
        carry = step(kl_ref[0, j * tk:(j + 1) * tk, :], vl_ref[0, j * tk:(j + 1) * tk, :], carry)
    _, acc = carry
    o = acc[:, :V_DIM] / acc[:, V_DIM:V_DIM + 1]
    lam = (jnp.exp(jnp.sum(lq1_ref[...] * lk1_ref[...], keepdims=True))
           - jnp.exp(jnp.sum(lq2_ref[...] * lk2_ref[...], keepdims=True)) + LAM_INIT)
    att = o[:tq] - lam * o[tq:]
    ms = jnp.mean(att * att, axis=-1, keepdims=True)
    y = att * lax.rsqrt(ms + EPS) * sg_ref[...]
    o_ref[0] = (y * (1.0 - LAM_INIT)).astype(BF16)


def _attention(q, kc, vc, kl, vl, lq1, lk1, lq2, lk2, sg):
    bsz, n, _ = q.shape
    ctx_len = kc.shape[1]
    tq = min(ATT_TQ, n)
    tk = min(ATT_TK, n)
    vec = lambda b, h, i: (0, 0)
    return pl.pallas_call(
        functools.partial(_attn_kernel, n=n, tk=tk),
        grid=(bsz, ATT_HEADS, n // tq),
        in_specs=[pl.BlockSpec((1, tq, V_DIM), lambda b, h, i: (b, i, h)),
                  pl.BlockSpec((1, ctx_len, V_DIM), lambda b, h, i: (b, 0, h)),
                  pl.BlockSpec((1, ctx_len, V_DIM), lambda b, h, i: (b, 0, h)),
                  pl.BlockSpec((1, n, V_DIM), lambda b, h, i: (b, 0, h)),
                  pl.BlockSpec((1, n, V_DIM), lambda b, h, i: (b, 0, h)),
                  pl.BlockSpec((1, HEAD_DIM), vec),
                  pl.BlockSpec((1, HEAD_DIM), vec),
                  pl.BlockSpec((1, HEAD_DIM), vec),
                  pl.BlockSpec((1, HEAD_DIM), vec),
                  pl.BlockSpec((1, V_DIM), vec)],
        out_specs=pl.BlockSpec((1, tq, V_DIM), lambda b, h, i: (b, i, h)),
        out_shape=jax.ShapeDtypeStruct((bsz, n, ATT_WIDTH), BF16),
        compiler_params=_cparams(("parallel", "parallel", "arbitrary")),
        name="diff_attention",
    )(q, kc, vc, kl, vl, lq1, lk1, lq2, lk2, sg)


def _outproj_kernel(lru_ref, att_ref, x_ref, g1_ref, shift_ref, scale_ref, n2_ref, wo_ref, wr_ref,
                    x1_ref, h2_ref, lg_ref):
    mix = _dot(lru_ref[0], wo_ref[:LRU_WIDTH, :]) + _dot(att_ref[0], wo_ref[LRU_WIDTH:, :])
    x1 = x_ref[0] + g1_ref[0] * mix
    x1_ref[0] = x1
    ms = jnp.mean(x1 * x1, axis=-1, keepdims=True)
    h2 = x1 * lax.rsqrt(ms + EPS) * n2_ref[...]
    h2 = h2 * (1.0 + scale_ref[0]) + shift_ref[0]
    h2_ref[0] = h2
    lg_ref[0] = _dot(h2.astype(BF16), wr_ref[...])[:, :N_EXPERTS]


def _outproj(lru, att, x, g1, shift2, scale2, n2g, wo_bf, wr_bf):
    bsz, n, d = x.shape
    tm = min(512, n)
    tok = lambda b, i: (b, i, 0)
    mod = lambda b, i: (b, 0, 0)
    full = lambda b, i: (0, 0)
    return pl.pallas_call(
        _outproj_kernel,
        grid=(bsz, n // tm),
        in_specs=[pl.BlockSpec((1, tm, LRU_WIDTH), tok),
                  pl.BlockSpec((1, tm, ATT_WIDTH), tok),
                  pl.BlockSpec((1, tm, d), tok),
                  pl.BlockSpec((1, 1, d), mod),
                  pl.BlockSpec((1, 1, d), mod),
                  pl.BlockSpec((1, 1, d), mod),
                  pl.BlockSpec((1, d), full),
                  pl.BlockSpec((d, d), full),
                  pl.BlockSpec((d, LANES), full)],
        out_specs=[pl.BlockSpec((1, tm, d), tok),
                   pl.BlockSpec((1, tm, d), tok),
                   pl.BlockSpec((1, tm, N_EXPERTS), tok)],
        out_shape=(jax.ShapeDtypeStruct((bsz, n, d), F32),
                   jax.ShapeDtypeStruct((bsz, n, d), F32),
                   jax.ShapeDtypeStruct((bsz, n, N_EXPERTS), F32)),
        compiler_params=_cparams(("parallel", "parallel")),
        name="outproj_router",
    )(lru, att, x, g1, shift2, scale2, n2g, wo_bf, wr_bf)


def _cumsum_lanes(x):
    rows, n = x.shape
    blk = min(LANES, n)
    tri = (lax.broadcasted_iota(jnp.int32, (blk, blk), 0)
           <= lax.broadcasted_iota(jnp.int32, (blk, blk), 1)).astype(BF16)
    off = jnp.zeros((rows, 1), F32)
    outs = []
    for kb in range(n // blk):
        c = _dot(x[:, kb * blk:(kb + 1) * blk].astype(BF16), tri)
        outs.append(c + off)
        off = off + c[:, blk - 1:blk]
    return jnp.concatenate(outs, axis=1)


def _topk_kernel(lg_ref, idx_ref, g_ref, aff_ref, cs_ref, *, n, cap):
    lg = lg_ref[0]
    ex = jnp.exp(lg - jnp.max(lg, axis=0, keepdims=True))
    aff = ex / jnp.sum(ex, axis=0, keepdims=True)
    capf = float(cap)

    def bisect(_, c):
        lo, hi = c
        mid = lo + ((hi - lo) >> 1)
        cnt = jnp.sum(jnp.where(aff >= pltpu.bitcast(mid, F32), 1.0, 0.0), axis=1, keepdims=True)
        ge = cnt >= capf
        return jnp.where(ge, mid, lo), jnp.where(ge, hi, mid)

    lo0 = jnp.zeros((N_EXPERTS, 1), jnp.int32)
    hi0 = jnp.full((N_EXPERTS, 1), 0x3F800001, jnp.int32)
    thr, nxt = lax.fori_loop(0, 31, bisect, (lo0, hi0))
    gt = aff >= pltpu.bitcast(nxt, F32)
    eq = jnp.logical_and(aff >= pltpu.bitcast(thr, F32), jnp.logical_not(gt))
    need = capf - jnp.sum(jnp.where(gt, 1.0, 0.0), axis=1, keepdims=True)
    ceq = _cumsum_lanes(jnp.where(eq, 1.0, 0.0))
    sel = jnp.where(gt, 1.0, jnp.where(eq, jnp.where(ceq <= need, 1.0, 0.0), 0.0))
    cs = _cumsum_lanes(sel)
    cs_ref[...] = cs * sel
    aff_ref[...] = aff

    jt = min(LANES, cap)
    blk = min(LANES, n)

    def per_expert(e, carry):
        crow = cs_ref[pl.ds(e, 1), :]
        arow = aff_ref[pl.ds(e, 1), :]
        for jc in range(cap // jt):
            rank = (lax.broadcasted_iota(jnp.int32, (jt, 1), 0) + (jc * jt + 1)).astype(F32)
            acc_i = jnp.zeros((jt, blk), F32)
            acc_g = jnp.zeros((jt, blk), F32)
            for tb in range(n // blk):
                hit = crow[:, tb * blk:(tb + 1) * blk] == rank
                tval = (lax.broadcasted_iota(jnp.int32, (1, blk), 1) + tb * blk).astype(F32)
                acc_i = acc_i + jnp.where(hit, tval, 0.0)
                acc_g = acc_g + jnp.where(hit, arow[:, tb * blk:(tb + 1) * blk], 0.0)
            row_i = jnp.sum(acc_i.T, axis=0, keepdims=True)
            row_g = jnp.sum(acc_g.T, axis=0, keepdims=True)
            idx_ref[0, e, :, jc * jt:(jc + 1) * jt] = row_i.astype(jnp.int32)
            g_ref[0, e, :, jc * jt:(jc + 1) * jt] = row_g
        return carry

    lax.fori_loop(0, N_EXPERTS, per_expert, 0)


def _topk(logits_t, cap):
    bsz, _, n = logits_t.shape
    return pl.pallas_call(
        functools.partial(_topk_kernel, n=n, cap=cap),
        grid=(bsz,),
        in_specs=[pl.BlockSpec((1, N_EXPERTS, n), lambda b: (b, 0, 0))],
        out_specs=[pl.BlockSpec((1, N_EXPERTS, 1, cap), lambda b: (b, 0, 0, 0)),
                   pl.BlockSpec((1, N_EXPERTS, 1, cap), lambda b: (b, 0, 0, 0))],
        out_shape=(jax.ShapeDtypeStruct((bsz, N_EXPERTS, 1, cap), jnp.int32),
                   jax.ShapeDtypeStruct((bsz, N_EXPERTS, 1, cap), F32)),
        scratch_shapes=[pltpu.VMEM((N_EXPERTS, n), F32), pltpu.VMEM((N_EXPERTS, n), F32)],
        compiler_params=_cparams(("parallel",)),
        name="expert_topk",
    )(logits_t)


def _gather_kernel(idx_ref, h_ref, o_ref, buf_ref, *, cap):
    def body(jj, carry):
        j0 = pl.multiple_of(jj * SUBLANES, SUBLANES)
        rows = [h_ref[0, pl.ds(idx_ref[0, 0, j0 + r], 1), :] for r in range(SUBLANES)]
        buf_ref[pl.ds(j0, SUBLANES), :] = jnp.concatenate(rows, axis=0)
        return carry

    lax.fori_loop(0, cap // SUBLANES, body, 0)
    o_ref[0, 0] = buf_ref[...].astype(BF16)


def _gather(idx3, h2, cap):
    bsz, n, d = h2.shape
    return pl.pallas_call(
        functools.partial(_gather_kernel, cap=cap),
        grid=(bsz, N_EXPERTS),
        in_specs=[pl.BlockSpec((1, 1, cap), lambda b, e: (b * N_EXPERTS + e, 0, 0), memory_space=pltpu.SMEM),
                  pl.BlockSpec((1, n, d), lambda b, e: (b, 0, 0))],
        out_specs=pl.BlockSpec((1, 1, cap, d), lambda b, e: (e, b, 0, 0)),
        out_shape=jax.ShapeDtypeStruct((N_EXPERTS, bsz, cap, d), BF16),
        scratch_shapes=[pltpu.VMEM((cap, d), F32)],
        compiler_params=_cparams(("parallel", "arbitrary")),
        name="expert_gather",
    )(idx3, h2)


FFN_TF = 256


FFN_SUB = 512


def _ffn_kernel(x_ref, wg_ref, wu_ref, wd_ref, o_ref):
    f = pl.program_id(2)

    def body(first):
        wgu = jnp.concatenate([wg_ref[0].astype(BF16), wu_ref[0].astype(BF16)], axis=1)
        wd = wd_ref[0].astype(BF16)
        tf = wd.shape[0]
        sub = min(FFN_SUB, x_ref.shape[1])
        for r in range(x_ref.shape[1] // sub):
            rows = slice(r * sub, (r + 1) * sub)
            gu = _dot(x_ref[0, rows, :], wgu)
            hg = gu[:, :tf]
            h = (hg * jax.nn.sigmoid(hg)) * gu[:, tf:]
            y = _dot(h.astype(BF16), wd)
            if first:
                o_ref[0, rows, :] = y
            else:
                o_ref[0, rows, :] += y

    @pl.when(f == 0)
    def _():
        body(True)

    @pl.when(f > 0)
    def _():
        body(False)


def _ffn(xe, w_gate, w_up, w_down):
    ne, m, d = xe.shape
    dff = w_gate.shape[2]
    tm = min(2048, m)
    tf = FFN_TF
    return pl.pallas_call(
        _ffn_kernel,
        grid=(ne, m // tm, dff // tf),
        in_specs=[pl.BlockSpec((1, tm, d), lambda e, i, f: (e, i, 0)),
                  pl.BlockSpec((1, d, tf), lambda e, i, f: (e, 0, f)),
                  pl.BlockSpec((1, d, tf), lambda e, i, f: (e, 0, f)),
                  pl.BlockSpec((1, tf, d), lambda e, i, f: (e, f, 0))],
        out_specs=pl.BlockSpec((1, tm, d), lambda e, i, f: (e, i, 0)),
        out_shape=jax.ShapeDtypeStruct((ne, m, d), F32),
        compiler_params=_cparams(("parallel", "parallel", "arbitrary")),
        name="expert_ffn",
    )(xe, w_gate, w_up, w_down)


def _combine_kernel(idx_ref, g_ref, x1_ref, gate_ref, y_ref, o_ref, *, cap):
    e = pl.program_id(1)

    @pl.when(e == 0)
    def _():
        o_ref[...] = x1_ref[...]

    gate = gate_ref[0]

    sub = lax.broadcasted_iota(jnp.int32, (SUBLANES, gate.shape[1]), 0)

    def body(jj, carry):
        j0 = pl.multiple_of(jj * SUBLANES, SUBLANES)
        ys = y_ref[0, 0, pl.ds(j0, SUBLANES), :]
        for r in range(SUBLANES):
            t = idx_ref[0, 0, j0 + r]
            w = g_ref[0, 0, j0 + r]
            base = pl.multiple_of((t >> 3) << 3, SUBLANES)
            upd = jnp.where(sub == (t & (SUBLANES - 1)), (w * gate) * ys[r:r + 1], 0.0)
            o_ref[0, pl.ds(base, SUBLANES), :] += upd
        return carry

    lax.fori_loop(0, cap // SUBLANES, body, 0)


def _combine(idx3, g3, x1, gate2, ye, cap):
    bsz, n, d = x1.shape
    smem = lambda b, e: (b * N_EXPERTS + e, 0, 0)
    resident = lambda b, e: (b, 0, 0)
    return pl.pallas_call(
        functools.partial(_combine_kernel, cap=cap),
        grid=(bsz, N_EXPERTS),
        in_specs=[pl.BlockSpec((1, 1, cap), smem, memory_space=pltpu.SMEM),
                  pl.BlockSpec((1, 1, cap), smem, memory_space=pltpu.SMEM),
                  pl.BlockSpec((1, n, d), resident, pipeline_mode=pl.Buffered(1)),
                  pl.BlockSpec((1, 1, d), resident),
                  pl.BlockSpec((1, 1, cap, d), lambda b, e: (e, b, 0, 0))],
        out_specs=pl.BlockSpec((1, n, d), resident, pipeline_mode=pl.Buffered(1)),
        out_shape=jax.ShapeDtypeStruct((bsz, n, d), F32),
        compiler_params=_cparams(("parallel", "arbitrary")),
        name="expert_combine",
    )(idx3, g3, x1, gate2, ye)


def _rope_tables(n):
    rows = n // GRID_W
    row = jnp.repeat(jnp.arange(rows), GRID_W).astype(F32)
    col = jnp.tile(jnp.arange(GRID_W), rows).astype(F32)
    inv = ROPE_BASE ** (-jnp.arange(ROPE_PAIRS, dtype=F32) / ROPE_PAIRS)
    ang_r = row[:, None] * inv
    ang_c = col[:, None] * inv
    cos64 = jnp.concatenate([jnp.cos(ang_r), jnp.cos(ang_r), jnp.cos(ang_c), jnp.cos(ang_c)], axis=1)
    sin64 = jnp.concatenate([-jnp.sin(ang_r), jnp.sin(ang_r), -jnp.sin(ang_c), jnp.sin(ang_c)], axis=1)
    return jnp.tile(cos64, (1, 2)), jnp.tile(sin64, (1, 2))


def kernel(x, c, ctx, c_ctx, w_ada, b_ada, norm1_g, norm2_g, w_in, conv_w, conv_b, lru_wa, lru_ba, lru_wi,
           lru_bi, lru_lambda, q_norm_g, k_norm_g, lambda_q1, lambda_k1, lambda_q2, lambda_k2, subln_g, w_out,
           w_router, w_gate, w_up, w_down):
    assert w_ada.shape[0] == 1, "single-layer configuration"
    bsz, n, d = x.shape
    ctx_len = ctx.shape[1]
    cap = EC_FACTOR * n // N_EXPERTS

    rows = ((bsz + 1 + SUBLANES - 1) // SUBLANES) * SUBLANES
    cvec = jnp.zeros((rows, d), F32).at[:bsz].set(c).at[bsz].set(c_ctx)
    mod = _adaln(cvec, w_ada[0], b_ada[0][None]).reshape(rows, N_MOD, 1, d)
    mod_l = mod[:bsz]
    mod_c = mod[bsz:bsz + 1]

    w_in_bf = w_in[0].astype(BF16)
    w_out_bf = w_out[0].astype(BF16)
    w_r_bf = jnp.zeros((d, LANES), BF16).at[:, :N_EXPERTS].set(w_router[0].astype(BF16))
    qg = jnp.tile(q_norm_g[0], QK_WIDTH // HEAD_DIM)[None]
    kg = jnp.tile(k_norm_g[0], QK_WIDTH // HEAD_DIM)[None]
    seg = jnp.arange(MXU_DIM) // HEAD_DIM
    ones_bd = (seg[:, None] == seg[None, :]).astype(BF16)
    cos_t, sin_t = _rope_tables(n)
    cos_c = jnp.ones((ctx_len, LANES), F32)
    sin_c = jnp.zeros((ctx_len, LANES), F32)
    g1 = norm1_g[0][None]

    xl, gate_l, q_l, k_l, v_l = _inproj(x, mod_l[:, 0], mod_l[:, 1], g1, w_in_bf, qg, kg, cos_t, sin_t,
                                        ones_bd, True)
    xc, _, _, k_c, v_c = _inproj(ctx, mod_c[:, 0], mod_c[:, 1], g1, w_in_bf, qg, kg, cos_c, sin_c,
                                 ones_bd, False)

    w_f, b_f = _lru_gate_weights(lru_wa[0, 0], lru_ba[0, 0], lru_wi[0, 0], lru_bi[0, 0])
    w_b, b_b = _lru_gate_weights(lru_wa[0, 1], lru_ba[0, 1], lru_wi[0, 1], lru_bi[0, 1])
    halves = LRU_WIDTH // LRU_HALF
    lam_h = lru_lambda[0].reshape(2, halves, LRU_HALF).transpose(1, 0, 2)
    lru = _rglru(xl, xc, gate_l, conv_w[0], conv_b[0][None], w_f, w_b, b_f, b_b, lam_h)

    att = _attention(q_l, k_c, v_c, k_l, v_l, lambda_q1, lambda_k1, lambda_q2, lambda_k2, subln_g)

    x1, h2, logits = _outproj(lru, att, x, mod_l[:, 2], mod_l[:, 3], mod_l[:, 4], norm2_g[0][None],
                              w_out_bf, w_r_bf)

    idx, gates = _topk(jnp.swapaxes(logits, 1, 2), cap)
    idx3 = idx.reshape(bsz * N_EXPERTS, 1, cap)
    g3 = gates.reshape(bsz * N_EXPERTS, 1, cap)
    xe = _gather(idx3, h2, cap)
    ye = _ffn(xe.reshape(N_EXPERTS, bsz * cap, d), w_gate[0], w_up[0], w_down[0])
    return _combine(idx3, g3, x1, mod_l[:, 5], ye.reshape(N_EXPERTS, bsz, cap, d), cap)
```

```python
import functools

import jax
import jax.numpy as jnp
from jax import lax
from jax.experimental import pallas as pl
from jax.experimental.pallas import tpu as pltpu

F32 = jnp.float32
BF16 = jnp.bfloat16

EPS = 1e-6
GRID_W = 64
LRU_WIDTH = 512
LRU_BLOCKS = 8
LRU_C = 8.0
CONV_W = 4
ATT_HEADS = 4
HEAD_DIM = 64
V_DIM = 2 * HEAD_DIM
QK_WIDTH = ATT_HEADS * 2 * HEAD_DIM
ATT_WIDTH = ATT_HEADS * V_DIM
ROPE_PAIRS = HEAD_DIM // 4
ROPE_BASE = 10000.0
N_EXPERTS = 16
EC_FACTOR = 2
N_MOD = 6
LAM_INIT = 0.2
LOG2_E = 1.4426950408889634

LANES = 128
SUBLANES = 8
MXU_DIM = 256
VMEM_LIMIT = 56 * 1024 * 1024


def _cparams(sem):
    return pltpu.CompilerParams(dimension_semantics=sem, vmem_limit_bytes=VMEM_LIMIT)


def _dot(a, b):
    return jnp.dot(a, b, preferred_element_type=F32)


def _dot_nt(a, b):
    return lax.dot_general(a, b, (((1,), (1,)), ((), ())), preferred_element_type=F32)


def _split_bf16(x):
    hi = x.astype(BF16)
    lo = (x - hi.astype(F32)).astype(BF16)
    return hi, lo


def _adaln_kernel(c_ref, w_ref, b_ref, o_ref):
    c = c_ref[...]
    s = c * jax.nn.sigmoid(c)
    s_hi, s_lo = _split_bf16(s)
    w_hi, w_lo = _split_bf16(w_ref[...])
    o_ref[...] = _dot(s_hi, w_hi) + _dot(s_hi, w_lo) + _dot(s_lo, w_hi) + b_ref[...]


def _adaln(cvec, w, b):
    rows, d = cvec.shape
    cols = w.shape[1]
    tn = cols // 4
    return pl.pallas_call(
        _adaln_kernel,
        grid=(cols // tn,),
        in_specs=[pl.BlockSpec((rows, d), lambda j: (0, 0)),
                  pl.BlockSpec((d, tn), lambda j: (0, j)),
                  pl.BlockSpec((1, tn), lambda j: (0, j))],
        out_specs=pl.BlockSpec((rows, tn), lambda j: (0, j)),
        out_shape=jax.ShapeDtypeStruct((rows, cols), F32),
        compiler_params=_cparams(("arbitrary",)),
        name="adaln",
    )(cvec, w, b)


def _swap_halves16(x):
    lane = lax.broadcasted_iota(jnp.int32, x.shape, 1)
    first = (lane % 32) < 16
    return jnp.where(first, pltpu.roll(x, LANES - 16, 1), pltpu.roll(x, 16, 1))


def _qk_norm(t, g, ones_bd):
    outs = []
    for c in range(t.shape[1] // MXU_DIM):
        tc = t[:, c * MXU_DIM:(c + 1) * MXU_DIM]
        hi, lo = _split_bf16(tc * tc)
        ssum = _dot(hi, ones_bd) + _dot(lo, ones_bd)
        outs.append(tc * lax.rsqrt(ssum * (1.0 / HEAD_DIM) + EPS) * g[:, c * MXU_DIM:(c + 1) * MXU_DIM])
    return outs


def _inproj_kernel(x_ref, shift_ref, scale_ref, g1_ref, w_ref, qg_ref, kg_ref, cos_ref, sin_ref, ones_ref,
                   xl_ref, gate_ref, q_ref, k_ref, v_ref, *, use_rope):
    x = x_ref[0]
    ms = jnp.mean(x * x, axis=-1, keepdims=True)
    h = x * lax.rsqrt(ms + EPS) * g1_ref[...]
    h = h * (1.0 + scale_ref[0]) + shift_ref[0]
    p = _dot(h.astype(BF16), w_ref[...])
    o1, o2, o3, o4 = LRU_WIDTH, 2 * LRU_WIDTH, 2 * LRU_WIDTH + QK_WIDTH, 2 * LRU_WIDTH + 2 * QK_WIDTH
    xl_ref[0] = p[:, :o1]
    gate_ref[0] = p[:, o1:o2]
    v_ref[0] = p[:, o4:].astype(BF16)
    ones_bd = ones_ref[...]
    qn = _qk_norm(p[:, o2:o3], qg_ref[...], ones_bd)
    kn = _qk_norm(p[:, o3:o4], kg_ref[...], ones_bd)
    scale = HEAD_DIM ** -0.5 * LOG2_E
    for src, dst, mul in ((qn, q_ref, scale), (kn, k_ref, 1.0)):
        for c, tc in enumerate(src):
            for hh in range(MXU_DIM // LANES):
                th = tc[:, hh * LANES:(hh + 1) * LANES]
                if use_rope:
                    th = th * cos_ref[...] + _swap_halves16(th) * sin_ref[...]
                col = c * MXU_DIM + hh * LANES
                dst[0, :, col:col + LANES] = (th * mul).astype(BF16)


def _inproj(x, shift, scale, g1, w_bf, qg, kg, cos_t, sin_t, ones_bd, use_rope):
    bsz, n, d = x.shape
    tm = min(512, n)
    wid = w_bf.shape[1]
    per_b = shift.shape[0] > 1
    mod_map = (lambda b, i: (b, 0, 0)) if per_b else (lambda b, i: (0, 0, 0))
    full = lambda b, i: (0, 0)
    tok = lambda b, i: (b, i, 0)
    out_shapes = (jax.ShapeDtypeStruct((bsz, n, LRU_WIDTH), F32),
                  jax.ShapeDtypeStruct((bsz, n, LRU_WIDTH), F32),
                  jax.ShapeDtypeStruct((bsz, n, QK_WIDTH), BF16),
                  jax.ShapeDtypeStruct((bsz, n, QK_WIDTH), BF16),
                  jax.ShapeDtypeStruct((bsz, n, ATT_WIDTH), BF16))
    return pl.pallas_call(
        functools.partial(_inproj_kernel, use_rope=use_rope),
        grid=(bsz, n // tm),
        in_specs=[pl.BlockSpec((1, tm, d), tok),
                  pl.BlockSpec((1, 1, d), mod_map),
                  pl.BlockSpec((1, 1, d), mod_map),
                  pl.BlockSpec((1, d), full),
                  pl.BlockSpec((d, wid), full),
                  pl.BlockSpec((1, QK_WIDTH), full),
                  pl.BlockSpec((1, QK_WIDTH), full),
                  pl.BlockSpec((tm, LANES), lambda b, i: (i, 0)),
                  pl.BlockSpec((tm, LANES), lambda b, i: (i, 0)),
                  pl.BlockSpec((MXU_DIM, MXU_DIM), full)],
        out_specs=[pl.BlockSpec((1, tm, LRU_WIDTH), tok),
                   pl.BlockSpec((1, tm, LRU_WIDTH), tok),
                   pl.BlockSpec((1, tm, QK_WIDTH), tok),
                   pl.BlockSpec((1, tm, QK_WIDTH), tok),
                   pl.BlockSpec((1, tm, ATT_WIDTH), tok)],
        out_shape=out_shapes,
        compiler_params=_cparams(("parallel", "parallel")),
        name="inproj_rope" if use_rope else "inproj_ctx",
    )(x, shift, scale, g1, w_bf, qg, kg, cos_t, sin_t, ones_bd)


LRU_HALF = LRU_WIDTH // 2
LRU_CHUNK = 512


def _conv_chunk(x_ref, t0, rows, total, cw, cb):
    x = x_ref[0, pl.ds(t0, rows), :]
    prev_start = pl.multiple_of(jnp.maximum(t0 - SUBLANES, 0), SUBLANES)
    next_start = pl.multiple_of(jnp.minimum(t0 + rows, total - SUBLANES), SUBLANES)
    prev = jnp.where(t0 > 0, x_ref[0, pl.ds(prev_start, SUBLANES), :], 0.0)
    nxt = jnp.where(t0 + rows < total, x_ref[0, pl.ds(next_start, SUBLANES), :], 0.0)
    xe = jnp.concatenate([prev, x, nxt], axis=0)
    acc = cb
    for k in range(CONV_W):
        off = SUBLANES - 1 + k
        acc = acc + xe[off:off + rows] * cw[k:k + 1]
    return acc


def _sigmoid(x):
    return 0.5 * jnp.tanh(0.5 * x) + 0.5


def _lru_gates(xc, w, bias, sp):
    pre = _dot(xc.astype(BF16), w) + bias
    r = _sigmoid(pre[:, :LRU_HALF])
    i = _sigmoid(pre[:, LRU_HALF:])
    log_a = (-LRU_C * r) * sp
    a = jnp.exp(log_a)
    mult = jnp.sqrt(-jnp.tanh(log_a) * (a * a + 1.0))
    return a, mult, i * xc


def _scan_chunk(a, u, h, reverse):
    rows, width = a.shape
    groups = rows // SUBLANES
    a = a.reshape(groups, SUBLANES, width)
    u = u.reshape(groups, SUBLANES, width)
    sub = lax.broadcasted_iota(jnp.int32, a.shape, 1)
    s = 1
    while s < SUBLANES:
        if reverse:
            m = sub < SUBLANES - s
            a_sh = pltpu.roll(a, SUBLANES - s, 1)
            u_sh = pltpu.roll(u, SUBLANES - s, 1)
        else:
            m = sub >= s
            a_sh = pltpu.roll(a, s, 1)
            u_sh = pltpu.roll(u, s, 1)
        u = jnp.where(m, a * u_sh + u, u)
        a = jnp.where(m, a * a_sh, a)
        s *= 2
    outs = [None] * groups
    order = range(groups - 1, -1, -1) if reverse else range(groups)
    for g in order:
        hg = a[g] * h + u[g]
        h = hg[0:1] if reverse else hg[SUBLANES - 1:SUBLANES]
        outs[g] = hg
    return jnp.concatenate(outs, axis=0), h


def _rglru_kernel(xl_ref, xc_ref, gate_ref, cw_ref, cb_ref, wf_ref, wb_ref, bf_ref, bb_ref, lam_ref,
                  o_ref, hf_ref, cl_ref, cc_ref, *, n, ctx_len):
    cw = cw_ref[...]
    cb = cb_ref[...]
    tc = min(LRU_CHUNK, ctx_len)
    tl = min(LRU_CHUNK, n)

    def direction(d, w_ref, b_ref):
        reverse = d == 1
        w = w_ref[0]
        bias = b_ref[0]
        z = -lam_ref[0, pl.ds(d, 1), :]
        sp = jnp.maximum(z, 0.0) + jnp.log1p(jnp.exp(-jnp.abs(z)))
        first_row = ctx_len - 1 if reverse else 0

        def conv(src_ref, cache_ref, t0, rows, total):
            if reverse:
                return cache_ref[pl.ds(t0, rows), :]
            xc = _conv_chunk(src_ref, t0, rows, total, cw, cb)
            cache_ref[pl.ds(t0, rows), :] = xc
            return xc

        def ctx_step(c, h):
            cc = (ctx_len // tc - 1 - c) if reverse else c
            t0 = pl.multiple_of(cc * tc, SUBLANES)
            xc = conv(xc_ref, cc_ref, t0, tc, ctx_len)
            a, mult, ix = _lru_gates(xc, w, bias, sp)
            row = lax.broadcasted_iota(jnp.int32, a.shape, 0) + t0
            mult = jnp.where(row == first_row, 1.0, mult)
            _, h = _scan_chunk(a, mult * ix, h, reverse)
            return h

        h = lax.fori_loop(0, ctx_len // tc, ctx_step, jnp.zeros((1, LRU_HALF), F32))

        def lat_step(c, h):
            cc = (n // tl - 1 - c) if reverse else c
            t0 = pl.multiple_of(cc * tl, SUBLANES)
            xc = conv(xl_ref, cl_ref, t0, tl, n)
            a, mult, ix = _lru_gates(xc, w, bias, sp)
            hs, h = _scan_chunk(a, mult * ix, h, reverse)
            if reverse:
                y = (hf_ref[pl.ds(t0, tl), :] + hs) * jax.nn.gelu(gate_ref[0, pl.ds(t0, tl), :])
                o_ref[0, pl.ds(t0, tl), :] = y.astype(BF16)
            else:
                hf_ref[pl.ds(t0, tl), :] = hs
            return h

        lax.fori_loop(0, n // tl, lat_step, h)

    direction(0, wf_ref, bf_ref)
    direction(1, wb_ref, bb_ref)


def _rglru(xl, xc, gate, conv_w, conv_b, w_f, w_b, b_f, b_b, lam):
    bsz, n, _ = xl.shape
    ctx_len = xc.shape[1]
    halves = LRU_WIDTH // LRU_HALF
    tokh = lambda b, hf: (b, 0, hf)
    return pl.pallas_call(
        functools.partial(_rglru_kernel, n=n, ctx_len=ctx_len),
        grid=(bsz, halves),
        in_specs=[pl.BlockSpec((1, n, LRU_HALF), tokh),
                  pl.BlockSpec((1, ctx_len, LRU_HALF), tokh),
                  pl.BlockSpec((1, n, LRU_HALF), tokh),
                  pl.BlockSpec((CONV_W, LRU_HALF), lambda b, hf: (0, hf)),
                  pl.BlockSpec((1, LRU_HALF), lambda b, hf: (0, hf)),
                  pl.BlockSpec((1, LRU_HALF, 2 * LRU_HALF), lambda b, hf: (hf, 0, 0)),
                  pl.BlockSpec((1, LRU_HALF, 2 * LRU_HALF), lambda b, hf: (hf, 0, 0)),
                  pl.BlockSpec((1, 1, 2 * LRU_HALF), lambda b, hf: (hf, 0, 0)),
                  pl.BlockSpec((1, 1, 2 * LRU_HALF), lambda b, hf: (hf, 0, 0)),
                  pl.BlockSpec((1, 2, LRU_HALF), lambda b, hf: (hf, 0, 0))],
        out_specs=pl.BlockSpec((1, n, LRU_HALF), tokh),
        out_shape=jax.ShapeDtypeStruct((bsz, n, LRU_WIDTH), BF16),
        scratch_shapes=[pltpu.VMEM((n, LRU_HALF), F32),
                        pltpu.VMEM((n, LRU_HALF), F32),
                        pltpu.VMEM((ctx_len, LRU_HALF), F32)],
        compiler_params=_cparams(("parallel", "parallel")),
        name="rglru",
    )(xl, xc, gate, conv_w, conv_b, w_f, w_b, b_f, b_b, lam)


def _lru_gate_weights(wa, ba, wi, bi):
    halves = LRU_WIDTH // LRU_HALF
    per = LRU_BLOCKS // halves
    bw = LRU_WIDTH // LRU_BLOCKS

    def dense(w):
        w = w.reshape(halves, per, bw, bw)
        eye = jnp.eye(per, dtype=w.dtype)
        return jnp.einsum("hpij,pq->hpiqj", w, eye).reshape(halves, per * bw, per * bw)

    w_cat = jnp.concatenate([dense(wa), dense(wi)], axis=-1).astype(BF16)
    b_cat = jnp.concatenate([ba.reshape(halves, 1, LRU_HALF), bi.reshape(halves, 1, LRU_HALF)], axis=-1)
    return w_cat, b_cat


ATT_TQ = 256
ATT_TK = 256


def _attn_kernel(q_ref, kc_ref, vc_ref, kl_ref, vl_ref, lq1_ref, lk1_ref, lq2_ref, lk2_ref, sg_ref, o_ref,
                 *, n, tk):
    q = q_ref[0]
    tq = q.shape[0]
    lane = lax.broadcasted_iota(jnp.int32, q.shape, 1)
    zero = jnp.zeros_like(q)
    q2 = jnp.concatenate([jnp.where(lane < HEAD_DIM, q, zero), jnp.where(lane >= HEAD_DIM, q, zero)], axis=0)

    def ext(v):
        ln = lax.broadcasted_iota(jnp.int32, v.shape, 1)
        return jnp.concatenate([v, jnp.where(ln == 0, 1.0, 0.0).astype(BF16)], axis=1)

    def step(k, v, carry):
        m, acc = carry
        s = _dot_nt(q2, k)
        m_new = jnp.maximum(m, jnp.max(s, axis=1, keepdims=True))
        p = jnp.exp2(s - m_new).astype(BF16)
        acc = jnp.exp2(m - m_new) * acc + _dot(p, ext(v))
        return m_new, acc

    carry = (jnp.full((2 * tq, 1), -1e30, F32), jnp.zeros((2 * tq, 2 * V_DIM), F32))
    carry = step(kc_ref[0], vc_ref[0], carry)
    for j in range(n // tk):
        carry = step(kl_ref[0, j * tk:(j + 1) * tk, :], vl_ref[0, j * tk:(j + 1) * tk, :], carry)
    _, acc = carry
    o = acc[:, :V_DIM] / acc[:, V_DIM:V_DIM + 1]
    lam = (jnp.exp(jnp.sum(lq1_ref[...] * lk1_ref[...], keepdims=True))
           - jnp.exp(jnp.sum(lq2_ref[...] * lk2_ref[...], keepdims=True)) + LAM_INIT)
    att = o[:tq] - lam * o[tq:]
    ms = jnp.mean(att * att, axis=-1, keepdims=True)
    y = att * lax.rsqrt(ms + EPS) * sg_ref[...]
    o_ref[0] = (y * (1.0 - LAM_INIT)).astype(BF16)


def _attention(q, kc, vc, kl, vl, lq1, lk1, lq2, lk2, sg):
    bsz, n, _ = q.shape
    ctx_len = kc.shape[1]
    tq = min(ATT_TQ, n)
    tk = min(ATT_TK, n)
    vec = lambda b, h, i: (0, 0)
    return pl.pallas_call(
        functools.partial(_attn_kernel, n=n, tk=tk),
        grid=(bsz, ATT_HEADS, n // tq),
        in_specs=[pl.BlockSpec((1, tq, V_DIM), lambda b, h, i: (b, i, h)),
                  pl.BlockSpec((1, ctx_len, V_DIM), lambda b, h, i: (b, 0, h)),
                  pl.BlockSpec((1, ctx_len, V_DIM), lambda b, h, i: (b, 0, h)),
                  pl.BlockSpec((1, n, V_DIM), lambda b, h, i: (b, 0, h)),
                  pl.BlockSpec((1, n, V_DIM), lambda b, h, i: (b, 0, h)),
                  pl.BlockSpec((1, HEAD_DIM), vec),
                  pl.BlockSpec((1, HEAD_DIM), vec),
                  pl.BlockSpec((1, HEAD_DIM), vec),
                  pl.BlockSpec((1, HEAD_DIM), vec),
                  pl.BlockSpec((1, V_DIM), vec)],
        out_specs=pl.BlockSpec((1, tq, V_DIM), lambda b, h, i: (b, i, h)),
        out_shape=jax.ShapeDtypeStruct((bsz, n, ATT_WIDTH), BF16),
        compiler_params=_cparams(("parallel", "parallel", "arbitrary")),
        name="diff_attention",
    )(q, kc, vc, kl, vl, lq1, lk1, lq2, lk2, sg)


def _outproj_kernel(lru_ref, att_ref, x_ref, g1_ref, shift_ref, scale_ref, n2_ref, wo_ref, wr_ref,
                    x1_ref, h2_ref, lg_ref):
    mix = _dot(lru_ref[0], wo_ref[:LRU_WIDTH, :]) + _dot(att_ref[0], wo_ref[LRU_WIDTH:, :])
    x1 = x_ref[0] + g1_ref[0] * mix
    x1_ref[0] = x1
    ms = jnp.mean(x1 * x1, axis=-1, keepdims=True)
    h2 = x1 * lax.rsqrt(ms + EPS) * n2_ref[...]
    h2 = h2 * (1.0 + scale_ref[0]) + shift_ref[0]
    h2_ref[0] = h2
    lg_ref[0] = _dot(h2.astype(BF16), wr_ref[...])[:, :N_EXPERTS]


def _outproj(lru, att, x, g1, shift2, scale2, n2g, wo_bf, wr_bf):
    bsz, n, d = x.shape
    tm = min(512, n)
    tok = lambda b, i: (b, i, 0)
    mod = lambda b, i: (b, 0, 0)
    full = lambda b, i: (0, 0)
    return pl.pallas_call(
        _outproj_kernel,
        grid=(bsz, n // tm),
        in_specs=[pl.BlockSpec((1, tm, LRU_WIDTH), tok),
                  pl.BlockSpec((1, tm, ATT_WIDTH), tok),
                  pl.BlockSpec((1, tm, d), tok),
                  pl.BlockSpec((1, 1, d), mod),
                  pl.BlockSpec((1, 1, d), mod),
                  pl.BlockSpec((1, 1, d), mod),
                  pl.BlockSpec((1, d), full),
                  pl.BlockSpec((d, d), full),
                  pl.BlockSpec((d, LANES), full)],
        out_specs=[pl.BlockSpec((1, tm, d), tok),
                   pl.BlockSpec((1, tm, d), tok),
                   pl.BlockSpec((1, tm, N_EXPERTS), tok)],
        out_shape=(jax.ShapeDtypeStruct((bsz, n, d), F32),
                   jax.ShapeDtypeStruct((bsz, n, d), F32),
                   jax.ShapeDtypeStruct((bsz, n, N_EXPERTS), F32)),
        compiler_params=_cparams(("parallel", "parallel")),
        name="outproj_router",
    )(lru, att, x, g1, shift2, scale2, n2g, wo_bf, wr_bf)


TOPK_UNROLL = 4


def _topk_kernel(lg_ref, idx_ref, g_ref, aff_ref, rank_ref, start_ref, tot_ref, *, cap):
    lg = lg_ref[0]
    ne, nb, blk = lg.shape
    ex = jnp.exp(lg - jnp.max(lg, axis=0, keepdims=True))
    aff = ex / jnp.sum(ex, axis=0, keepdims=True)
    aff_ref[...] = aff
    capf = float(cap)

    def bisect(_, c):
        lo, hi = c
        mid = lo + ((hi - lo) >> 1)
        cnt = jnp.sum(jnp.where(aff >= pltpu.bitcast(mid, F32), 1.0, 0.0), axis=(1, 2), keepdims=True)
        ge = cnt >= capf
        return jnp.where(ge, mid, lo), jnp.where(ge, hi, mid)

    lo0 = jnp.zeros((ne, 1, 1), jnp.int32)
    hi0 = jnp.full((ne, 1, 1), 0x3F800001, jnp.int32)
    thr, nxt = lax.fori_loop(0, 31, bisect, (lo0, hi0))
    gt = aff >= pltpu.bitcast(nxt, F32)
    gtf = jnp.where(gt, 1.0, 0.0)
    eqf = jnp.where(jnp.logical_and(aff >= pltpu.bitcast(thr, F32), jnp.logical_not(gt)), 1.0, 0.0)
    need = capf - jnp.sum(gtf, axis=(1, 2), keepdims=True)

    rows = ne * nb
    tri = (lax.broadcasted_iota(jnp.int32, (blk, blk), 0)
           <= lax.broadcasted_iota(jnp.int32, (blk, blk), 1)).astype(BF16)
    r_i = lax.broadcasted_iota(jnp.int32, (rows, rows), 0)
    c_i = lax.broadcasted_iota(jnp.int32, (rows, rows), 1)
    earlier = jnp.logical_and(c_i < r_i, c_i >= (r_i // nb) * nb).astype(BF16)

    def prefix(x3):
        cin = _dot(x3.reshape(rows, blk).astype(BF16), tri)
        tot = jnp.broadcast_to(cin[:, blk - 1:blk], cin.shape)
        start = _dot(earlier, tot.astype(BF16))
        return cin.reshape(x3.shape), start.reshape(x3.shape), tot.reshape(x3.shape)

    cin, start, _ = prefix(eqf)
    sel = gtf + eqf * jnp.where(cin + start <= need, 1.0, 0.0)
    cin, start, tot = prefix(sel)
    rank_ref[...] = cin * sel
    start_ref[...] = start
    tot_ref[...] = tot

    slot = (lax.broadcasted_iota(jnp.int32, (1, cap), 1) + 1).astype(F32)
    row_id = lax.broadcasted_iota(jnp.int32, (nb, 1), 0).astype(F32)
    lane_id = lax.broadcasted_iota(jnp.int32, (blk, 1), 0).astype(F32)

    def per_expert(e):
        a = aff_ref[e]
        st = start_ref[e][:, 0:1]
        in_row = jnp.logical_and(st < slot, slot <= st + tot_ref[e][:, 0:1])
        in_row_bf = jnp.where(in_row, 1.0, 0.0).astype(BF16)
        row_of = jnp.sum(jnp.where(in_row, row_id, 0.0), axis=0, keepdims=True)
        rank_need = slot - jnp.sum(jnp.where(in_row, st, 0.0), axis=0, keepdims=True)

        def pick(x_bf):
            return lax.dot_general(x_bf, in_row_bf, (((0,), (0,)), ((), ())), preferred_element_type=F32)

        hit = pick(rank_ref[e].astype(BF16)) == rank_need
        lane_of = jnp.sum(jnp.where(hit, lane_id, 0.0), axis=0, keepdims=True)
        a_hi = a.astype(BF16)
        a_mid = (a - a_hi.astype(F32)).astype(BF16)
        a_lo = (a - a_hi.astype(F32) - a_mid.astype(F32)).astype(BF16)
        a_sel = (pick(a_hi) + pick(a_mid)) + pick(a_lo)
        idx_ref[0, e] = (row_of * float(blk) + lane_of).astype(jnp.int32)
        g_ref[0, e] = jnp.sum(jnp.where(hit, a_sel, 0.0), axis=0, keepdims=True)

    def expert_group(gi, carry):
        for r in range(TOPK_UNROLL):
            per_expert(gi * TOPK_UNROLL + r)
        return carry

    lax.fori_loop(0, ne // TOPK_UNROLL, expert_group, 0)


def _topk(logits_t, cap):
    bsz, _, n = logits_t.shape
    blk = min(LANES, n)
    nb = n // blk
    slab = pltpu.VMEM((N_EXPERTS, nb, blk), F32)
    return pl.pallas_call(
        functools.partial(_topk_kernel, cap=cap),
        grid=(bsz,),
        in_specs=[pl.BlockSpec((1, N_EXPERTS, nb, blk), lambda b: (b, 0, 0, 0))],
        out_specs=[pl.BlockSpec((1, N_EXPERTS, 1, cap), lambda b: (b, 0, 0, 0)),
                   pl.BlockSpec((1, N_EXPERTS, 1, cap), lambda b: (b, 0, 0, 0))],
        out_shape=(jax.ShapeDtypeStruct((bsz, N_EXPERTS, 1, cap), jnp.int32),
                   jax.ShapeDtypeStruct((bsz, N_EXPERTS, 1, cap), F32)),
        scratch_shapes=[slab, slab, slab, slab],
        compiler_params=_cparams(("parallel",)),
        name="expert_topk",
    )(logits_t.reshape(bsz, N_EXPERTS, nb, blk))


def _gather_kernel(idx_ref, h_ref, o_ref, buf_ref, *, cap):
    def body(jj, carry):
        j0 = pl.multiple_of(jj * SUBLANES, SUBLANES)
        rows = [h_ref[0, pl.ds(idx_ref[0, 0, j0 + r], 1), :] for r in range(SUBLANES)]
        buf_ref[pl.ds(j0, SUBLANES), :] = jnp.concatenate(rows, axis=0)
        return carry

    lax.fori_loop(0, cap // SUBLANES, body, 0)
    o_ref[0, 0] = buf_ref[...].astype(BF16)


def _gather(idx3, h2, cap):
    bsz, n, d = h2.shape
    return pl.pallas_call(
        functools.partial(_gather_kernel, cap=cap),
        grid=(bsz, N_EXPERTS),
        in_specs=[pl.BlockSpec((1, 1, cap), lambda b, e: (b * N_EXPERTS + e, 0, 0), memory_space=pltpu.SMEM),
                  pl.BlockSpec((1, n, d), lambda b, e: (b, 0, 0))],
        out_specs=pl.BlockSpec((1, 1, cap, d), lambda b, e: (e, b, 0, 0)),
        out_shape=jax.ShapeDtypeStruct((N_EXPERTS, bsz, cap, d), BF16),
        scratch_shapes=[pltpu.VMEM((cap, d), F32)],
        compiler_params=_cparams(("parallel", "arbitrary")),
        name="expert_gather",
    )(idx3, h2)


FFN_TF = 256


FFN_SUB = 512


def _ffn_kernel(x_ref, wg_ref, wu_ref, wd_ref, g_ref, gate_ref, o_ref, *, cap):
    i = pl.program_id(1)
    f = pl.program_id(2)
    nf = pl.num_programs(2)
    tm = x_ref.shape[1]
    sub = min(FFN_SUB, tm, cap)

    def body(first, last):
        wgu = jnp.concatenate([wg_ref[0].astype(BF16), wu_ref[0].astype(BF16)], axis=1)
        wd = wd_ref[0].astype(BF16)
        tf = wd.shape[0]
        for r in range(tm // sub):
            rows = slice(r * sub, (r + 1) * sub)
            gu = _dot(x_ref[0, rows, :], wgu)
            hg = gu[:, :tf]
            h = (hg * jax.nn.sigmoid(hg)) * gu[:, tf:]
            y = _dot(h.astype(BF16), wd)
            if not first:
                y = o_ref[0, rows, :] + y
            if last:
                sample = (i * tm + r * sub) // cap
                y = y * (g_ref[0, rows, :] * gate_ref[sample])
            o_ref[0, rows, :] = y

    @pl.when(f == 0)
    def _():
        body(True, False)

    @pl.when(jnp.logical_and(f > 0, f < nf - 1))
    def _():
        body(False, False)

    @pl.when(f == nf - 1)
    def _():
        body(False, True)


def _ffn(xe, w_gate, w_up, w_down, g_col, gate2, cap):
    ne, m, d = xe.shape
    dff = w_gate.shape[2]
    tm = min(2048, m)
    tf = FFN_TF
    assert dff // tf > 1 and cap % min(FFN_SUB, tm, cap) == 0
    return pl.pallas_call(
        functools.partial(_ffn_kernel, cap=cap),
        grid=(ne, m // tm, dff // tf),
        in_specs=[pl.BlockSpec((1, tm, d), lambda e, i, f: (e, i, 0)),
                  pl.BlockSpec((1, d, tf), lambda e, i, f: (e, 0, f)),
                  pl.BlockSpec((1, d, tf), lambda e, i, f: (e, 0, f)),
                  pl.BlockSpec((1, tf, d), lambda e, i, f: (e, f, 0)),
                  pl.BlockSpec((1, tm, 1), lambda e, i, f: (e, i, 0)),
                  pl.BlockSpec(gate2.shape, lambda e, i, f: (0, 0, 0))],
        out_specs=pl.BlockSpec((1, tm, d), lambda e, i, f: (e, i, 0)),
        out_shape=jax.ShapeDtypeStruct((ne, m, d), F32),
        compiler_params=_cparams(("parallel", "parallel", "arbitrary")),
        name="expert_ffn",
    )(xe, w_gate, w_up, w_down, g_col, gate2)


def _combine_kernel(idx_ref, x1_ref, y_ref, o_ref, *, cap):
    e = pl.program_id(1)

    @pl.when(e == 0)
    def _():
        o_ref[...] = x1_ref[...]

    sub = lax.broadcasted_iota(jnp.int32, (SUBLANES, o_ref.shape[2]), 0)

    def body(jj, carry):
        j0 = pl.multiple_of(jj * SUBLANES, SUBLANES)
        ys = y_ref[0, 0, pl.ds(j0, SUBLANES), :]
        for r in range(SUBLANES):
            t = idx_ref[0, 0, j0 + r]
            base = pl.multiple_of((t >> 3) << 3, SUBLANES)
            o_ref[0, pl.ds(base, SUBLANES), :] += jnp.where(sub == (t & (SUBLANES - 1)), ys[r:r + 1], 0.0)
        return carry

    lax.fori_loop(0, cap // SUBLANES, body, 0)


def _combine(idx3, x1, ye, cap):
    bsz, n, d = x1.shape
    resident = lambda b, e: (b, 0, 0)
    return pl.pallas_call(
        functools.partial(_combine_kernel, cap=cap),
        grid=(bsz, N_EXPERTS),
        in_specs=[pl.BlockSpec((1, 1, cap), lambda b, e: (b * N_EXPERTS + e, 0, 0), memory_space=pltpu.SMEM),
                  pl.BlockSpec((1, n, d), resident, pipeline_mode=pl.Buffered(1)),
                  pl.BlockSpec((1, 1, cap, d), lambda b, e: (e, b, 0, 0))],
        out_specs=pl.BlockSpec((1, n, d), resident, pipeline_mode=pl.Buffered(1)),
        out_shape=jax.ShapeDtypeStruct((bsz, n, d), F32),
        compiler_params=_cparams(("parallel", "arbitrary")),
        name="expert_combine",
    )(idx3, x1, ye)


def _rope_tables(n):
    rows = n // GRID_W
    row = jnp.repeat(jnp.arange(rows), GRID_W).astype(F32)
    col = jnp.tile(jnp.arange(GRID_W), rows).astype(F32)
    inv = ROPE_BASE ** (-jnp.arange(ROPE_PAIRS, dtype=F32) / ROPE_PAIRS)
    ang_r = row[:, None] * inv
    ang_c = col[:, None] * inv
    cos64 = jnp.concatenate([jnp.cos(ang_r), jnp.cos(ang_r), jnp.cos(ang_c), jnp.cos(ang_c)], axis=1)
    sin64 = jnp.concatenate([-jnp.sin(ang_r), jnp.sin(ang_r), -jnp.sin(ang_c), jnp.sin(ang_c)], axis=1)
    return jnp.tile(cos64, (1, 2)), jnp.tile(sin64, (1, 2))


def kernel(x, c, ctx, c_ctx, w_ada, b_ada, norm1_g, norm2_g, w_in, conv_w, conv_b, lru_wa, lru_ba, lru_wi,
           lru_bi, lru_lambda, q_norm_g, k_norm_g, lambda_q1, lambda_k1, lambda_q2, lambda_k2, subln_g, w_out,
           w_router, w_gate, w_up, w_down):
    assert w_ada.shape[0] == 1, "single-layer configuration"
    bsz, n, d = x.shape
    ctx_len = ctx.shape[1]
    cap = EC_FACTOR * n // N_EXPERTS

    rows = ((bsz + 1 + SUBLANES - 1) // SUBLANES) * SUBLANES
    cvec = jnp.zeros((rows, d), F32).at[:bsz].set(c).at[bsz].set(c_ctx)
    mod = _adaln(cvec, w_ada[0], b_ada[0][None]).reshape(rows, N_MOD, 1, d)
    mod_l = mod[:bsz]
    mod_c = mod[bsz:bsz + 1]

    w_in_bf = w_in[0].astype(BF16)
    w_out_bf = w_out[0].astype(BF16)
    w_r_bf = jnp.zeros((d, LANES), BF16).at[:, :N_EXPERTS].set(w_router[0].astype(BF16))
    qg = jnp.tile(q_norm_g[0], QK_WIDTH // HEAD_DIM)[None]
    kg = jnp.tile(k_norm_g[0], QK_WIDTH // HEAD_DIM)[None]
    seg = jnp.arange(MXU_DIM) // HEAD_DIM
    ones_bd = (seg[:, None] == seg[None, :]).astype(BF16)
    cos_t, sin_t = _rope_tables(n)
    cos_c = jnp.ones((ctx_len, LANES), F32)
    sin_c = jnp.zeros((ctx_len, LANES), F32)
    g1 = norm1_g[0][None]

    xl, gate_l, q_l, k_l, v_l = _inproj(x, mod_l[:, 0], mod_l[:, 1], g1, w_in_bf, qg, kg, cos_t, sin_t,
                                        ones_bd, True)
    xc, _, _, k_c, v_c = _inproj(ctx, mod_c[:, 0], mod_c[:, 1], g1, w_in_bf, qg, kg, cos_c, sin_c,
                                 ones_bd, False)

    w_f, b_f = _lru_gate_weights(lru_wa[0, 0], lru_ba[0, 0], lru_wi[0, 0], lru_bi[0, 0])
    w_b, b_b = _lru_gate_weights(lru_wa[0, 1], lru_ba[0, 1], lru_wi[0, 1], lru_bi[0, 1])
    halves = LRU_WIDTH // LRU_HALF
    lam_h = lru_lambda[0].reshape(2, halves, LRU_HALF).transpose(1, 0, 2)
    lru = _rglru(xl, xc, gate_l, conv_w[0], conv_b[0][None], w_f, w_b, b_f, b_b, lam_h)

    att = _attention(q_l, k_c, v_c, k_l, v_l, lambda_q1, lambda_k1, lambda_q2, lambda_k2, subln_g)

    x1, h2, logits = _outproj(lru, att, x, mod_l[:, 2], mod_l[:, 3], mod_l[:, 4], norm2_g[0][None],
                              w_out_bf, w_r_bf)

    idx, gates = _topk(jnp.swapaxes(logits, 1, 2), cap)
    idx3 = idx.reshape(bsz * N_EXPERTS, 1, cap)
    g_col = jnp.swapaxes(gates.reshape(bsz, N_EXPERTS, cap), 0, 1).reshape(N_EXPERTS, bsz * cap, 1)
    xe = _gather(idx3, h2, cap)
    ye = _ffn(xe.reshape(N_EXPERTS, bsz * cap, d), w_gate[0], w_up[0], w_down[0], g_col, mod_l[:, 5], cap)
    return _combine(idx3, x1, ye.reshape(N_EXPERTS, bsz, cap, d), cap)
```

```python
import functools

import jax
import jax.numpy as jnp
from jax import lax
from jax.experimental import pallas as pl
from jax.experimental.pallas import tpu as pltpu

F32 = jnp.float32
BF16 = jnp.bfloat16

EPS = 1e-6
GRID_W = 64
LRU_WIDTH = 512
LRU_BLOCKS = 8
LRU_C = 8.0
CONV_W = 4
ATT_HEADS = 4
HEAD_DIM = 64
V_DIM = 2 * HEAD_DIM
QK_WIDTH = ATT_HEADS * 2 * HEAD_DIM
ATT_WIDTH = ATT_HEADS * V_DIM
ROPE_PAIRS = HEAD_DIM // 4
ROPE_BASE = 10000.0
N_EXPERTS = 16
EC_FACTOR = 2
N_MOD = 6
LAM_INIT = 0.2
LOG2_E = 1.4426950408889634

LANES = 128
SUBLANES = 8
MXU_DIM = 256
VMEM_LIMIT = 56 * 1024 * 1024


def _cparams(sem):
    return pltpu.CompilerParams(dimension_semantics=sem, vmem_limit_bytes=VMEM_LIMIT)


def _dot(a, b):
    return jnp.dot(a, b, preferred_element_type=F32)


def _dot_nt(a, b):
    return lax.dot_general(a, b, (((1,), (1,)), ((), ())), preferred_element_type=F32)


def _split_bf16(x):
    hi = x.astype(BF16)
    lo = (x - hi.astype(F32)).astype(BF16)
    return hi, lo


def _adaln_kernel(c_ref, w_ref, b_ref, o_ref):
    c = c_ref[...]
    s = c * jax.nn.sigmoid(c)
    s_hi, s_lo = _split_bf16(s)
    w_hi, w_lo = _split_bf16(w_ref[...])
    o_ref[...] = _dot(s_hi, w_hi) + _dot(s_hi, w_lo) + _dot(s_lo, w_hi) + b_ref[...]


def _adaln(cvec, w, b):
    rows, d = cvec.shape
    cols = w.shape[1]
    tn = cols // 4
    return pl.pallas_call(
        _adaln_kernel,
        grid=(cols // tn,),
        in_specs=[pl.BlockSpec((rows, d), lambda j: (0, 0)),
                  pl.BlockSpec((d, tn), lambda j: (0, j)),
                  pl.BlockSpec((1, tn), lambda j: (0, j))],
        out_specs=pl.BlockSpec((rows, tn), lambda j: (0, j)),
        out_shape=jax.ShapeDtypeStruct((rows, cols), F32),
        compiler_params=_cparams(("arbitrary",)),
        name="adaln",
    )(cvec, w, b)


def _swap_halves16(x):
    lane = lax.broadcasted_iota(jnp.int32, x.shape, 1)
    first = (lane % 32) < 16
    return jnp.where(first, pltpu.roll(x, LANES - 16, 1), pltpu.roll(x, 16, 1))


def _qk_norm(t, g, ones_bd):
    outs = []
    for c in range(t.shape[1] // MXU_DIM):
        tc = t[:, c * MXU_DIM:(c + 1) * MXU_DIM]
        hi, lo = _split_bf16(tc * tc)
        ssum = _dot(hi, ones_bd) + _dot(lo, ones_bd)
        outs.append(tc * lax.rsqrt(ssum * (1.0 / HEAD_DIM) + EPS) * g[:, c * MXU_DIM:(c + 1) * MXU_DIM])
    return outs


def _inproj_kernel(x_ref, shift_ref, scale_ref, g1_ref, w_ref, qg_ref, kg_ref, cos_ref, sin_ref, ones_ref,
                   xl_ref, gate_ref, q_ref, k_ref, v_ref, *, use_rope):
    x = x_ref[0]
    ms = jnp.mean(x * x, axis=-1, keepdims=True)
    h = x * lax.rsqrt(ms + EPS) * g1_ref[...]
    h = h * (1.0 + scale_ref[0]) + shift_ref[0]
    p = _dot(h.astype(BF16), w_ref[...])
    o1, o2, o3, o4 = LRU_WIDTH, 2 * LRU_WIDTH, 2 * LRU_WIDTH + QK_WIDTH, 2 * LRU_WIDTH + 2 * QK_WIDTH
    xl_ref[0] = p[:, :o1]
    gate_ref[0] = p[:, o1:o2]
    v_ref[0] = p[:, o4:].astype(BF16)
    ones_bd = ones_ref[...]
    qn = _qk_norm(p[:, o2:o3], qg_ref[...], ones_bd)
    kn = _qk_norm(p[:, o3:o4], kg_ref[...], ones_bd)
    scale = HEAD_DIM ** -0.5 * LOG2_E
    for src, dst, mul in ((qn, q_ref, scale), (kn, k_ref, 1.0)):
        for c, tc in enumerate(src):
            for hh in range(MXU_DIM // LANES):
                th = tc[:, hh * LANES:(hh + 1) * LANES]
                if use_rope:
                    th = th * cos_ref[...] + _swap_halves16(th) * sin_ref[...]
                col = c * MXU_DIM + hh * LANES
                dst[0, :, col:col + LANES] = (th * mul).astype(BF16)


def _inproj(x, shift, scale, g1, w_bf, qg, kg, cos_t, sin_t, ones_bd, use_rope):
    bsz, n, d = x.shape
    tm = min(512, n)
    wid = w_bf.shape[1]
    per_b = shift.shape[0] > 1
    mod_map = (lambda b, i: (b, 0, 0)) if per_b else (lambda b, i: (0, 0, 0))
    full = lambda b, i: (0, 0)
    tok = lambda b, i: (b, i, 0)
    out_shapes = (jax.ShapeDtypeStruct((bsz, n, LRU_WIDTH), F32),
                  jax.ShapeDtypeStruct((bsz, n, LRU_WIDTH), F32),
                  jax.ShapeDtypeStruct((bsz, n, QK_WIDTH), BF16),
                  jax.ShapeDtypeStruct((bsz, n, QK_WIDTH), BF16),
                  jax.ShapeDtypeStruct((bsz, n, ATT_WIDTH), BF16))
    return pl.pallas_call(
        functools.partial(_inproj_kernel, use_rope=use_rope),
        grid=(bsz, n // tm),
        in_specs=[pl.BlockSpec((1, tm, d), tok),
                  pl.BlockSpec((1, 1, d), mod_map),
                  pl.BlockSpec((1, 1, d), mod_map),
                  pl.BlockSpec((1, d), full),
                  pl.BlockSpec((d, wid), full),
                  pl.BlockSpec((1, QK_WIDTH), full),
                  pl.BlockSpec((1, QK_WIDTH), full),
                  pl.BlockSpec((tm, LANES), lambda b, i: (i, 0)),
                  pl.BlockSpec((tm, LANES), lambda b, i: (i, 0)),
                  pl.BlockSpec((MXU_DIM, MXU_DIM), full)],
        out_specs=[pl.BlockSpec((1, tm, LRU_WIDTH), tok),
                   pl.BlockSpec((1, tm, LRU_WIDTH), tok),
                   pl.BlockSpec((1, tm, QK_WIDTH), tok),
                   pl.BlockSpec((1, tm, QK_WIDTH), tok),
                   pl.BlockSpec((1, tm, ATT_WIDTH), tok)],
        out_shape=out_shapes,
        compiler_params=_cparams(("parallel", "parallel")),
        name="inproj_rope" if use_rope else "inproj_ctx",
    )(x, shift, scale, g1, w_bf, qg, kg, cos_t, sin_t, ones_bd)


LRU_HALF = LRU_WIDTH // 2
LRU_CHUNK = 512


def _conv_chunk(x_ref, t0, rows, total, cw, cb):
    x = x_ref[0, pl.ds(t0, rows), :]
    prev_start = pl.multiple_of(jnp.maximum(t0 - SUBLANES, 0), SUBLANES)
    next_start = pl.multiple_of(jnp.minimum(t0 + rows, total - SUBLANES), SUBLANES)
    prev = jnp.where(t0 > 0, x_ref[0, pl.ds(prev_start, SUBLANES), :], 0.0)
    nxt = jnp.where(t0 + rows < total, x_ref[0, pl.ds(next_start, SUBLANES), :], 0.0)
    xe = jnp.concatenate([prev, x, nxt], axis=0)
    acc = cb
    for k in range(CONV_W):
        off = SUBLANES - 1 + k
        acc = acc + xe[off:off + rows] * cw[k:k + 1]
    return acc


def _sigmoid(x):
    return 0.5 * jnp.tanh(0.5 * x) + 0.5


def _lru_gates(xc, w, bias, sp):
    pre = _dot(xc.astype(BF16), w) + bias
    r = _sigmoid(pre[:, :LRU_HALF])
    i = _sigmoid(pre[:, LRU_HALF:])
    log_a = (-LRU_C * r) * sp
    a = jnp.exp(log_a)
    mult = jnp.sqrt(-jnp.tanh(log_a) * (a * a + 1.0))
    return a, mult, i * xc


def _scan_chunk(a, u, h, reverse):
    rows, width = a.shape
    groups = rows // SUBLANES
    a = a.reshape(groups, SUBLANES, width)
    u = u.reshape(groups, SUBLANES, width)
    sub = lax.broadcasted_iota(jnp.int32, a.shape, 1)
    s = 1
    while s < SUBLANES:
        if reverse:
            m = sub < SUBLANES - s
            a_sh = pltpu.roll(a, SUBLANES - s, 1)
            u_sh = pltpu.roll(u, SUBLANES - s, 1)
        else:
            m = sub >= s
            a_sh = pltpu.roll(a, s, 1)
            u_sh = pltpu.roll(u, s, 1)
        u = jnp.where(m, a * u_sh + u, u)
        a = jnp.where(m, a * a_sh, a)
        s *= 2
    outs = [None] * groups
    order = range(groups - 1, -1, -1) if reverse else range(groups)
    for g in order:
        hg = a[g] * h + u[g]
        h = hg[0:1] if reverse else hg[SUBLANES - 1:SUBLANES]
        outs[g] = hg
    return jnp.concatenate(outs, axis=0), h


def _rglru_kernel(xl_ref, xc_ref, gate_ref, cw_ref, cb_ref, wf_ref, wb_ref, bf_ref, bb_ref, lam_ref,
                  o_ref, hf_ref, cl_ref, cc_ref, *, n, ctx_len):
    cw = cw_ref[...]
    cb = cb_ref[...]
    tc = min(LRU_CHUNK, ctx_len)
    tl = min(LRU_CHUNK, n)

    def direction(d, w_ref, b_ref):
        reverse = d == 1
        w = w_ref[0]
        bias = b_ref[0]
        z = -lam_ref[0, pl.ds(d, 1), :]
        sp = jnp.maximum(z, 0.0) + jnp.log1p(jnp.exp(-jnp.abs(z)))
        first_row = ctx_len - 1 if reverse else 0

        def conv(src_ref, cache_ref, t0, rows, total):
            if reverse:
                return cache_ref[pl.ds(t0, rows), :]
            xc = _conv_chunk(src_ref, t0, rows, total, cw, cb)
            cache_ref[pl.ds(t0, rows), :] = xc
            return xc

        def ctx_step(c, h):
            cc = (ctx_len // tc - 1 - c) if reverse else c
            t0 = pl.multiple_of(cc * tc, SUBLANES)
            xc = conv(xc_ref, cc_ref, t0, tc, ctx_len)
            a, mult, ix = _lru_gates(xc, w, bias, sp)
            row = lax.broadcasted_iota(jnp.int32, a.shape, 0) + t0
            mult = jnp.where(row == first_row, 1.0, mult)
            _, h = _scan_chunk(a, mult * ix, h, reverse)
            return h

        h = lax.fori_loop(0, ctx_len // tc, ctx_step, jnp.zeros((1, LRU_HALF), F32))

        def lat_step(c, h):
            cc = (n // tl - 1 - c) if reverse else c
            t0 = pl.multiple_of(cc * tl, SUBLANES)
            xc = conv(xl_ref, cl_ref, t0, tl, n)
            a, mult, ix = _lru_gates(xc, w, bias, sp)
            hs, h = _scan_chunk(a, mult * ix, h, reverse)
            if reverse:
                y = (hf_ref[pl.ds(t0, tl), :] + hs) * jax.nn.gelu(gate_ref[0, pl.ds(t0, tl), :])
                o_ref[0, pl.ds(t0, tl), :] = y.astype(BF16)
            else:
                hf_ref[pl.ds(t0, tl), :] = hs
            return h

        lax.fori_loop(0, n // tl, lat_step, h)

    direction(0, wf_ref, bf_ref)
    direction(1, wb_ref, bb_ref)


def _rglru(xl, xc, gate, conv_w, conv_b, w_f, w_b, b_f, b_b, lam):
    bsz, n, _ = xl.shape
    ctx_len = xc.shape[1]
    halves = LRU_WIDTH // LRU_HALF
    tokh = lambda b, hf: (b, 0, hf)
    return pl.pallas_call(
        functools.partial(_rglru_kernel, n=n, ctx_len=ctx_len),
        grid=(bsz, halves),
        in_specs=[pl.BlockSpec((1, n, LRU_HALF), tokh),
                  pl.BlockSpec((1, ctx_len, LRU_HALF), tokh),
                  pl.BlockSpec((1, n, LRU_HALF), tokh),
                  pl.BlockSpec((CONV_W, LRU_HALF), lambda b, hf: (0, hf)),
                  pl.BlockSpec((1, LRU_HALF), lambda b, hf: (0, hf)),
                  pl.BlockSpec((1, LRU_HALF, 2 * LRU_HALF), lambda b, hf: (hf, 0, 0)),
                  pl.BlockSpec((1, LRU_HALF, 2 * LRU_HALF), lambda b, hf: (hf, 0, 0)),
                  pl.BlockSpec((1, 1, 2 * LRU_HALF), lambda b, hf: (hf, 0, 0)),
                  pl.BlockSpec((1, 1, 2 * LRU_HALF), lambda b, hf: (hf, 0, 0)),
                  pl.BlockSpec((1, 2, LRU_HALF), lambda b, hf: (hf, 0, 0))],
        out_specs=pl.BlockSpec((1, n, LRU_HALF), tokh),
        out_shape=jax.ShapeDtypeStruct((bsz, n, LRU_WIDTH), BF16),
        scratch_shapes=[pltpu.VMEM((n, LRU_HALF), F32),
                        pltpu.VMEM((n, LRU_HALF), F32),
                        pltpu.VMEM((ctx_len, LRU_HALF), F32)],
        compiler_params=_cparams(("parallel", "parallel")),
        name="rglru",
    )(xl, xc, gate, conv_w, conv_b, w_f, w_b, b_f, b_b, lam)


def _lru_gate_weights(wa, ba, wi, bi):
    halves = LRU_WIDTH // LRU_HALF
    per = LRU_BLOCKS // halves
    bw = LRU_WIDTH // LRU_BLOCKS

    def dense(w):
        w = w.reshape(halves, per, bw, bw)
        eye = jnp.eye(per, dtype=w.dtype)
        return jnp.einsum("hpij,pq->hpiqj", w, eye).reshape(halves, per * bw, per * bw)

    w_cat = jnp.concatenate([dense(wa), dense(wi)], axis=-1).astype(BF16)
    b_cat = jnp.concatenate([ba.reshape(halves, 1, LRU_HALF), bi.reshape(halves, 1, LRU_HALF)], axis=-1)
    return w_cat, b_cat


ATT_TQ = 256
ATT_TILES = 16
ATT_TK = 512
ATT_TK_GENERAL = 256
ATT_UNROLL = 4
ATT_BOUND_LIMIT = 48.0


def _attn_kernel(q_ref, kc_ref, vc_ref, kl_ref, vl_ref, lq1_ref, lk1_ref, lq2_ref, lk2_ref, sg_ref, o_ref,
                 kmax_ref, bnd_ref, p_a, p_b, *, n, tq, tiles):
    ctx_len = kc_ref.shape[1]
    tk = min(ATT_TK, n)
    nk = n // tk
    lam = (jnp.exp(jnp.sum(lq1_ref[...] * lk1_ref[...], keepdims=True))
           - jnp.exp(jnp.sum(lq2_ref[...] * lk2_ref[...], keepdims=True)) + LAM_INIT)
    lane_r = lax.broadcasted_iota(jnp.int32, (LANES, LANES), 0)
    lane_c = lax.broadcasted_iota(jnp.int32, (LANES, LANES), 1)
    same_half = ((lane_r < HEAD_DIM) == (lane_c < HEAD_DIM)).astype(BF16)

    def half_norms(x):
        return _dot(x * x, same_half)

    @pl.when(pl.program_id(2) == 0)
    def _():
        km = jnp.max(half_norms(kc_ref[0]), axis=0, keepdims=True)

        def body(j, km):
            t0 = pl.multiple_of(j * tk, tk)
            return jnp.maximum(km, jnp.max(half_norms(kl_ref[0, pl.ds(t0, tk), :]), axis=0, keepdims=True))

        km = lax.fori_loop(0, nk, body, km)
        kmax_ref[...] = jnp.broadcast_to(km, kmax_ref.shape)

    bnd = jnp.sqrt(half_norms(q_ref[0]) * kmax_ref[0:1, :])
    bnd_ref[...] = bnd
    fast = jnp.max(bnd) <= ATT_BOUND_LIMIT

    def split_q(t):
        q = q_ref[0, pl.ds(pl.multiple_of(t * tq, tq), tq), :]
        lane = lax.broadcasted_iota(jnp.int32, q.shape, 1)
        zero = jnp.zeros_like(q)
        return jnp.concatenate([jnp.where(lane < HEAD_DIM, q, zero), jnp.where(lane >= HEAD_DIM, q, zero)], axis=0)

    def finish(t, att):
        ms = jnp.mean(att * att, axis=-1, keepdims=True)
        y = att * lax.rsqrt(ms + EPS) * sg_ref[...]
        o_ref[0, pl.ds(pl.multiple_of(t * tq, tq), tq), :] = (y * (1.0 - LAM_INIT)).astype(BF16)

    def probs(q2, bound, buf, k, col, rows, lp):
        p = jnp.exp2(_dot_nt(q2, k) - bound)
        buf[:, pl.ds(col, rows)] = p
        for c in range(rows // LANES):
            lp = lp + p[:, c * LANES:(c + 1) * LANES]
        return lp

    def values(buf, r0, r1, v, col, rows, att):
        w = buf[:tq, pl.ds(col, rows)] * r0 - buf[tq:, pl.ds(col, rows)] * r1
        return att + _dot(w.astype(BF16), v)

    def stage(t, new, old, r0, r1, have_p, have_v):
        lp = jnp.zeros((2 * tq, LANES), F32)
        att = jnp.zeros((tq, V_DIM), F32)
        if have_p:
            q2 = split_q(t)
            bt = bnd_ref[pl.ds(pl.multiple_of(t * tq, tq), tq), :]
            bound = jnp.concatenate([bt[:, 0:1], bt[:, HEAD_DIM:HEAD_DIM + 1]], axis=0)
            lp = probs(q2, bound, new, kc_ref[0], 0, ctx_len, lp)
        if have_v:
            att = values(old, r0, r1, vc_ref[0], 0, ctx_len, att)

        def body(j, carry):
            lp, att = carry
            t0 = pl.multiple_of(j * tk, tk)
            col = pl.multiple_of(ctx_len + j * tk, LANES)
            if have_p:
                lp = probs(q2, bound, new, kl_ref[0, pl.ds(t0, tk), :], col, tk, lp)
            if have_v:
                att = values(old, r0, r1, vl_ref[0, pl.ds(t0, tk), :], col, tk, att)
            return lp, att

        lp, att = lax.fori_loop(0, nk, body, (lp, att), unroll=min(ATT_UNROLL, nk))
        if have_v:
            finish(t - 1, att)
        if have_p:
            l = jnp.sum(lp, axis=1, keepdims=True)
            return 1.0 / l[:tq], lam / l[tq:]
        return r0, r1

    def run_fast():
        r = stage(0, p_a, p_b, None, None, True, False)

        def pair(u, r):
            r = stage(2 * u + 1, p_b, p_a, r[0], r[1], True, True)
            return stage(2 * u + 2, p_a, p_b, r[0], r[1], True, True)

        r = lax.fori_loop(0, (tiles - 1) // 2, pair, r)
        if tiles % 2 == 0:
            r = stage(tiles - 1, p_b, p_a, r[0], r[1], True, True)
            stage(tiles, p_a, p_b, r[0], r[1], False, True)
        else:
            stage(tiles, p_b, p_a, r[0], r[1], False, True)

    def run_general():
        tkg = min(ATT_TK_GENERAL, n)

        def ext(v):
            ln = lax.broadcasted_iota(jnp.int32, v.shape, 1)
            return jnp.concatenate([v, jnp.where(ln == 0, 1.0, 0.0).astype(BF16)], axis=1)

        def step(q2, k, v, carry):
            m, acc = carry
            s = _dot_nt(q2, k)
            m_new = jnp.maximum(m, jnp.max(s, axis=1, keepdims=True))
            p = jnp.exp2(s - m_new).astype(BF16)
            return m_new, jnp.exp2(m - m_new) * acc + _dot(p, ext(v))

        def tile(t, carry):
            q2 = split_q(t)
            c = (jnp.full((2 * tq, 1), -1e30, F32), jnp.zeros((2 * tq, 2 * V_DIM), F32))
            c = step(q2, kc_ref[0], vc_ref[0], c)

            def body(j, c):
                t0 = pl.multiple_of(j * tkg, tkg)
                return step(q2, kl_ref[0, pl.ds(t0, tkg), :], vl_ref[0, pl.ds(t0, tkg), :], c)

            _, acc = lax.fori_loop(0, n // tkg, body, c, unroll=True)
            o = acc[:, :V_DIM] / acc[:, V_DIM:V_DIM + 1]
            finish(t, o[:tq] - lam * o[tq:])
            return carry

        lax.fori_loop(0, tiles, tile, 0)

    @pl.when(fast)
    def _():
        run_fast()

    @pl.when(jnp.logical_not(fast))
    def _():
        run_general()


def _attention(q, kc, vc, kl, vl, lq1, lk1, lq2, lk2, sg):
    bsz, n, _ = q.shape
    ctx_len = kc.shape[1]
    tq = min(ATT_TQ, n)
    tiles = min(ATT_TILES, n // tq)
    vec = lambda b, h, i: (0, 0)
    return pl.pallas_call(
        functools.partial(_attn_kernel, n=n, tq=tq, tiles=tiles),
        grid=(bsz, ATT_HEADS, n // (tq * tiles)),
        in_specs=[pl.BlockSpec((1, tq * tiles, V_DIM), lambda b, h, i: (b, i, h)),
                  pl.BlockSpec((1, ctx_len, V_DIM), lambda b, h, i: (b, 0, h)),
                  pl.BlockSpec((1, ctx_len, V_DIM), lambda b, h, i: (b, 0, h)),
                  pl.BlockSpec((1, n, V_DIM), lambda b, h, i: (b, 0, h)),
                  pl.BlockSpec((1, n, V_DIM), lambda b, h, i: (b, 0, h)),
                  pl.BlockSpec((1, HEAD_DIM), vec),
                  pl.BlockSpec((1, HEAD_DIM), vec),
                  pl.BlockSpec((1, HEAD_DIM), vec),
                  pl.BlockSpec((1, HEAD_DIM), vec),
                  pl.BlockSpec((1, V_DIM), vec)],
        out_specs=pl.BlockSpec((1, tq * tiles, V_DIM), lambda b, h, i: (b, i, h)),
        out_shape=jax.ShapeDtypeStruct((bsz, n, ATT_WIDTH), BF16),
        scratch_shapes=[pltpu.VMEM((SUBLANES, LANES), F32),
                        pltpu.VMEM((tq * tiles, LANES), F32),
                        pltpu.VMEM((2 * tq, ctx_len + n), F32),
                        pltpu.VMEM((2 * tq, ctx_len + n), F32)],
        compiler_params=_cparams(("arbitrary", "arbitrary", "arbitrary")),
        name="diff_attention",
    )(q, kc, vc, kl, vl, lq1, lk1, lq2, lk2, sg)


def _outproj_kernel(lru_ref, att_ref, x_ref, g1_ref, shift_ref, scale_ref, n2_ref, wo_ref, wr_ref,
                    x1_ref, h2_ref, lg_ref):
    mix = _dot(lru_ref[0], wo_ref[:LRU_WIDTH, :]) + _dot(att_ref[0], wo_ref[LRU_WIDTH:, :])
    x1 = x_ref[0] + g1_ref[0] * mix
    x1_ref[0] = x1
    ms = jnp.mean(x1 * x1, axis=-1, keepdims=True)
    h2 = x1 * lax.rsqrt(ms + EPS) * n2_ref[...]
    h2 = h2 * (1.0 + scale_ref[0]) + shift_ref[0]
    h2_ref[0] = h2
    lg_ref[0] = _dot(h2.astype(BF16), wr_ref[...])[:, :N_EXPERTS]


def _outproj(lru, att, x, g1, shift2, scale2, n2g, wo_bf, wr_bf):
    bsz, n, d = x.shape
    tm = min(512, n)
    tok = lambda b, i: (b, i, 0)
    mod = lambda b, i: (b, 0, 0)
    full = lambda b, i: (0, 0)
    return pl.pallas_call(
        _outproj_kernel,
        grid=(bsz, n // tm),
        in_specs=[pl.BlockSpec((1, tm, LRU_WIDTH), tok),
                  pl.BlockSpec((1, tm, ATT_WIDTH), tok),
                  pl.BlockSpec((1, tm, d), tok),
                  pl.BlockSpec((1, 1, d), mod),
                  pl.BlockSpec((1, 1, d), mod),
                  pl.BlockSpec((1, 1, d), mod),
                  pl.BlockSpec((1, d), full),
                  pl.BlockSpec((d, d), full),
                  pl.BlockSpec((d, LANES), full)],
        out_specs=[pl.BlockSpec((1, tm, d), tok),
                   pl.BlockSpec((1, tm, d), tok),
                   pl.BlockSpec((1, tm, N_EXPERTS), tok)],
        out_shape=(jax.ShapeDtypeStruct((bsz, n, d), F32),
                   jax.ShapeDtypeStruct((bsz, n, d), F32),
                   jax.ShapeDtypeStruct((bsz, n, N_EXPERTS), F32)),
        compiler_params=_cparams(("parallel", "parallel")),
        name="outproj_router",
    )(lru, att, x, g1, shift2, scale2, n2g, wo_bf, wr_bf)


TOPK_UNROLL = 4


def _topk_kernel(lg_ref, idx_ref, g_ref, aff_ref, rank_ref, start_ref, tot_ref, *, cap):
    lg = lg_ref[0]
    ne, nb, blk = lg.shape
    ex = jnp.exp(lg - jnp.max(lg, axis=0, keepdims=True))
    aff = ex / jnp.sum(ex, axis=0, keepdims=True)
    aff_ref[...] = aff
    capf = float(cap)

    def bisect(_, c):
        lo, hi = c
        mid = lo + ((hi - lo) >> 1)
        cnt = jnp.sum(jnp.where(aff >= pltpu.bitcast(mid, F32), 1.0, 0.0), axis=(1, 2), keepdims=True)
        ge = cnt >= capf
        return jnp.where(ge, mid, lo), jnp.where(ge, hi, mid)

    lo0 = jnp.zeros((ne, 1, 1), jnp.int32)
    hi0 = jnp.full((ne, 1, 1), 0x3F800001, jnp.int32)
    thr, nxt = lax.fori_loop(0, 31, bisect, (lo0, hi0))
    gt = aff >= pltpu.bitcast(nxt, F32)
    gtf = jnp.where(gt, 1.0, 0.0)
    eqf = jnp.where(jnp.logical_and(aff >= pltpu.bitcast(thr, F32), jnp.logical_not(gt)), 1.0, 0.0)
    need = capf - jnp.sum(gtf, axis=(1, 2), keepdims=True)

    rows = ne * nb
    tri = (lax.broadcasted_iota(jnp.int32, (blk, blk), 0)
           <= lax.broadcasted_iota(jnp.int32, (blk, blk), 1)).astype(BF16)
    r_i = lax.broadcasted_iota(jnp.int32, (rows, rows), 0)
    c_i = lax.broadcasted_iota(jnp.int32, (rows, rows), 1)
    earlier = jnp.logical_and(c_i < r_i, c_i >= (r_i // nb) * nb).astype(BF16)

    def prefix(x3):
        cin = _dot(x3.reshape(rows, blk).astype(BF16), tri)
        tot = jnp.broadcast_to(cin[:, blk - 1:blk], cin.shape)
        start = _dot(earlier, tot.astype(BF16))
        return cin.reshape(x3.shape), start.reshape(x3.shape), tot.reshape(x3.shape)

    cin, start, _ = prefix(eqf)
    sel = gtf + eqf * jnp.where(cin + start <= need, 1.0, 0.0)
    cin, start, tot = prefix(sel)
    rank_ref[...] = cin * sel
    start_ref[...] = start
    tot_ref[...] = tot

    slot = (lax.broadcasted_iota(jnp.int32, (1, cap), 1) + 1).astype(F32)
    row_id = lax.broadcasted_iota(jnp.int32, (nb, 1), 0).astype(F32)
    lane_id = lax.broadcasted_iota(jnp.int32, (blk, 1), 0).astype(F32)

    def per_expert(e):
        a = aff_ref[e]
        st = start_ref[e][:, 0:1]
        in_row = jnp.logical_and(st < slot, slot <= st + tot_ref[e][:, 0:1])
        in_row_bf = jnp.where(in_row, 1.0, 0.0).astype(BF16)
        row_of = jnp.sum(jnp.where(in_row, row_id, 0.0), axis=0, keepdims=True)
        rank_need = slot - jnp.sum(jnp.where(in_row, st, 0.0), axis=0, keepdims=True)

        def pick(x_bf):
            return lax.dot_general(x_bf, in_row_bf, (((0,), (0,)), ((), ())), preferred_element_type=F32)

        hit = pick(rank_ref[e].astype(BF16)) == rank_need
        lane_of = jnp.sum(jnp.where(hit, lane_id, 0.0), axis=0, keepdims=True)
        a_hi = a.astype(BF16)
        a_mid = (a - a_hi.astype(F32)).astype(BF16)
        a_lo = (a - a_hi.astype(F32) - a_mid.astype(F32)).astype(BF16)
        a_sel = (pick(a_hi) + pick(a_mid)) + pick(a_lo)
        idx_ref[0, e] = (row_of * float(blk) + lane_of).astype(jnp.int32)
        g_ref[0, e] = jnp.sum(jnp.where(hit, a_sel, 0.0), axis=0, keepdims=True)

    def expert_group(gi, carry):
        for r in range(TOPK_UNROLL):
            per_expert(gi * TOPK_UNROLL + r)
        return carry

    lax.fori_loop(0, ne // TOPK_UNROLL, expert_group, 0)


def _topk(logits_t, cap):
    bsz, _, n = logits_t.shape
    blk = min(LANES, n)
    nb = n // blk
    slab = pltpu.VMEM((N_EXPERTS, nb, blk), F32)
    return pl.pallas_call(
        functools.partial(_topk_kernel, cap=cap),
        grid=(bsz,),
        in_specs=[pl.BlockSpec((1, N_EXPERTS, nb, blk), lambda b: (b, 0, 0, 0))],
        out_specs=[pl.BlockSpec((1, N_EXPERTS, 1, cap), lambda b: (b, 0, 0, 0)),
                   pl.BlockSpec((1, N_EXPERTS, 1, cap), lambda b: (b, 0, 0, 0))],
        out_shape=(jax.ShapeDtypeStruct((bsz, N_EXPERTS, 1, cap), jnp.int32),
                   jax.ShapeDtypeStruct((bsz, N_EXPERTS, 1, cap), F32)),
        scratch_shapes=[slab, slab, slab, slab],
        compiler_params=_cparams(("parallel",)),
        name="expert_topk",
    )(logits_t.reshape(bsz, N_EXPERTS, nb, blk))


def _gather_kernel(idx_ref, h_ref, o_ref, buf_ref, *, cap):
    def body(jj, carry):
        j0 = pl.multiple_of(jj * SUBLANES, SUBLANES)
        rows = [h_ref[0, pl.ds(idx_ref[0, 0, j0 + r], 1), :] for r in range(SUBLANES)]
        buf_ref[pl.ds(j0, SUBLANES), :] = jnp.concatenate(rows, axis=0)
        return carry

    lax.fori_loop(0, cap // SUBLANES, body, 0)
    o_ref[0, 0] = buf_ref[...].astype(BF16)


def _gather(idx3, h2, cap):
    bsz, n, d = h2.shape
    return pl.pallas_call(
        functools.partial(_gather_kernel, cap=cap),
        grid=(bsz, N_EXPERTS),
        in_specs=[pl.BlockSpec((1, 1, cap), lambda b, e: (b * N_EXPERTS + e, 0, 0), memory_space=pltpu.SMEM),
                  pl.BlockSpec((1, n, d), lambda b, e: (b, 0, 0))],
        out_specs=pl.BlockSpec((1, 1, cap, d), lambda b, e: (e, b, 0, 0)),
        out_shape=jax.ShapeDtypeStruct((N_EXPERTS, bsz, cap, d), BF16),
        scratch_shapes=[pltpu.VMEM((cap, d), F32)],
        compiler_params=_cparams(("parallel", "arbitrary")),
        name="expert_gather",
    )(idx3, h2)


FFN_TF = 256


FFN_SUB = 512


def _ffn_kernel(x_ref, wg_ref, wu_ref, wd_ref, g_ref, gate_ref, o_ref, *, cap):
    i = pl.program_id(1)
    f = pl.program_id(2)
    nf = pl.num_programs(2)
    tm = x_ref.shape[1]
    sub = min(FFN_SUB, tm, cap)

    def body(first, last):
        wgu = jnp.concatenate([wg_ref[0].astype(BF16), wu_ref[0].astype(BF16)], axis=1)
        wd = wd_ref[0].astype(BF16)
        tf = wd.shape[0]
        for r in range(tm // sub):
            rows = slice(r * sub, (r + 1) * sub)
            gu = _dot(x_ref[0, rows, :], wgu)
            hg = gu[:, :tf]
            h = (hg * jax.nn.sigmoid(hg)) * gu[:, tf:]
            y = _dot(h.astype(BF16), wd)
            if not first:
                y = o_ref[0, rows, :] + y
            if last:
                sample = (i * tm + r * sub) // cap
                y = y * (g_ref[0, rows, :] * gate_ref[sample])
            o_ref[0, rows, :] = y

    @pl.when(f == 0)
    def _():
        body(True, False)

    @pl.when(jnp.logical_and(f > 0, f < nf - 1))
    def _():
        body(False, False)

    @pl.when(f == nf - 1)
    def _():
        body(False, True)


def _ffn(xe, w_gate, w_up, w_down, g_col, gate2, cap):
    ne, m, d = xe.shape
    dff = w_gate.shape[2]
    tm = min(2048, m)
    tf = FFN_TF
    assert dff // tf > 1 and cap % min(FFN_SUB, tm, cap) == 0
    return pl.pallas_call(
        functools.partial(_ffn_kernel, cap=cap),
        grid=(ne, m // tm, dff // tf),
        in_specs=[pl.BlockSpec((1, tm, d), lambda e, i, f: (e, i, 0)),
                  pl.BlockSpec((1, d, tf), lambda e, i, f: (e, 0, f)),
                  pl.BlockSpec((1, d, tf), lambda e, i, f: (e, 0, f)),
                  pl.BlockSpec((1, tf, d), lambda e, i, f: (e, f, 0)),
                  pl.BlockSpec((1, tm, 1), lambda e, i, f: (e, i, 0)),
                  pl.BlockSpec(gate2.shape, lambda e, i, f: (0, 0, 0))],
        out_specs=pl.BlockSpec((1, tm, d), lambda e, i, f: (e, i, 0)),
        out_shape=jax.ShapeDtypeStruct((ne, m, d), F32),
        compiler_params=_cparams(("parallel", "parallel", "arbitrary")),
        name="expert_ffn",
    )(xe, w_gate, w_up, w_down, g_col, gate2)


def _combine_kernel(idx_ref, x1_ref, y_ref, o_ref, *, cap):
    e = pl.program_id(1)

    @pl.when(e == 0)
    def _():
        o_ref[...] = x1_ref[...]

    sub = lax.broadcasted_iota(jnp.int32, (SUBLANES, o_ref.shape[2]), 0)

    def body(jj, carry):
        j0 = pl.multiple_of(jj * SUBLANES, SUBLANES)
        ys = y_ref[0, 0, pl.ds(j0, SUBLANES), :]
        for r in range(0, SUBLANES, 2):
            t0 = idx_ref[0, 0, j0 + r]
            t1 = idx_ref[0, 0, j0 + r + 1]
            b0 = pl.multiple_of((t0 >> 3) << 3, SUBLANES)
            b1 = pl.multiple_of((t1 >> 3) << 3, SUBLANES)
            tile0 = o_ref[0, pl.ds(b0, SUBLANES), :]
            tile1 = o_ref[0, pl.ds(b1, SUBLANES), :]
            u0 = jnp.where(sub == (t0 & (SUBLANES - 1)), ys[r:r + 1], 0.0)
            u1 = jnp.where(sub == (t1 & (SUBLANES - 1)), ys[r + 1:r + 2], 0.0)
            o_ref[0, pl.ds(b0, SUBLANES), :] = tile0 + u0
            o_ref[0, pl.ds(b1, SUBLANES), :] = tile1 + u1 + jnp.where(b0 == b1, u0, 0.0)
        return carry

    lax.fori_loop(0, cap // SUBLANES, body, 0)


def _combine(idx3, x1, ye, cap):
    bsz, n, d = x1.shape
    resident = lambda b, e: (b, 0, 0)
    return pl.pallas_call(
        functools.partial(_combine_kernel, cap=cap),
        grid=(bsz, N_EXPERTS),
        in_specs=[pl.BlockSpec((1, 1, cap), lambda b, e: (b * N_EXPERTS + e, 0, 0), memory_space=pltpu.SMEM),
                  pl.BlockSpec((1, n, d), resident, pipeline_mode=pl.Buffered(1)),
                  pl.BlockSpec((1, 1, cap, d), lambda b, e: (e, b, 0, 0))],
        out_specs=pl.BlockSpec((1, n, d), resident, pipeline_mode=pl.Buffered(1)),
        out_shape=jax.ShapeDtypeStruct((bsz, n, d), F32),
        compiler_params=_cparams(("parallel", "arbitrary")),
        name="expert_combine",
    )(idx3, x1, ye)


def _rope_tables(n):
    rows = n // GRID_W
    row = jnp.repeat(jnp.arange(rows), GRID_W).astype(F32)
    col = jnp.tile(jnp.arange(GRID_W), rows).astype(F32)
    inv = ROPE_BASE ** (-jnp.arange(ROPE_PAIRS, dtype=F32) / ROPE_PAIRS)
    ang_r = row[:, None] * inv
    ang_c = col[:, None] * inv
    cos64 = jnp.concatenate([jnp.cos(ang_r), jnp.cos(ang_r), jnp.cos(ang_c), jnp.cos(ang_c)], axis=1)
    sin64 = jnp.concatenate([-jnp.sin(ang_r), jnp.sin(ang_r), -jnp.sin(ang_c), jnp.sin(ang_c)], axis=1)
    return jnp.tile(cos64, (1, 2)), jnp.tile(sin64, (1, 2))


def kernel(x, c, ctx, c_ctx, w_ada, b_ada, norm1_g, norm2_g, w_in, conv_w, conv_b, lru_wa, lru_ba, lru_wi,
           lru_bi, lru_lambda, q_norm_g, k_norm_g, lambda_q1, lambda_k1, lambda_q2, lambda_k2, subln_g, w_out,
           w_router, w_gate, w_up, w_down):
    assert w_ada.shape[0] == 1, "single-layer configuration"
    bsz, n, d = x.shape
    ctx_len = ctx.shape[1]
    cap = EC_FACTOR * n // N_EXPERTS

    rows = ((bsz + 1 + SUBLANES - 1) // SUBLANES) * SUBLANES
    cvec = jnp.zeros((rows, d), F32).at[:bsz].set(c).at[bsz].set(c_ctx)
    mod = _adaln(cvec, w_ada[0], b_ada[0][None]).reshape(rows, N_MOD, 1, d)
    mod_l = mod[:bsz]
    mod_c = mod[bsz:bsz + 1]

    w_in_bf = w_in[0].astype(BF16)
    w_out_bf = w_out[0].astype(BF16)
    w_r_bf = jnp.zeros((d, LANES), BF16).at[:, :N_EXPERTS].set(w_router[0].astype(BF16))
    qg = jnp.tile(q_norm_g[0], QK_WIDTH // HEAD_DIM)[None]
    kg = jnp.tile(k_norm_g[0], QK_WIDTH // HEAD_DIM)[None]
    seg = jnp.arange(MXU_DIM) // HEAD_DIM
    ones_bd = (seg[:, None] == seg[None, :]).astype(BF16)
    cos_t, sin_t = _rope_tables(n)
    cos_c = jnp.ones((ctx_len, LANES), F32)
    sin_c = jnp.zeros((ctx_len, LANES), F32)
    g1 = norm1_g[0][None]

    xl, gate_l, q_l, k_l, v_l = _inproj(x, mod_l[:, 0], mod_l[:, 1], g1, w_in_bf, qg, kg, cos_t, sin_t,
                                        ones_bd, True)
    xc, _, _, k_c, v_c = _inproj(ctx, mod_c[:, 0], mod_c[:, 1], g1, w_in_bf, qg, kg, cos_c, sin_c,
                                 ones_bd, False)

    w_f, b_f = _lru_gate_weights(lru_wa[0, 0], lru_ba[0, 0], lru_wi[0, 0], lru_bi[0, 0])
    w_b, b_b = _lru_gate_weights(lru_wa[0, 1], lru_ba[0, 1], lru_wi[0, 1], lru_bi[0, 1])
    halves = LRU_WIDTH // LRU_HALF
    lam_h = lru_lambda[0].reshape(2, halves, LRU_HALF).transpose(1, 0, 2)
    lru = _rglru(xl, xc, gate_l, conv_w[0], conv_b[0][None], w_f, w_b, b_f, b_b, lam_h)

    att = _attention(q_l, k_c, v_c, k_l, v_l, lambda_q1, lambda_k1, lambda_q2, lambda_k2, subln_g)

    x1, h2, logits = _outproj(lru, att, x, mod_l[:, 2], mod_l[:, 3], mod_l[:, 4], norm2_g[0][None],
                              w_out_bf, w_r_bf)

    idx, gates = _topk(jnp.swapaxes(logits, 1, 2), cap)
    idx3 = idx.reshape(bsz * N_EXPERTS, 1, cap)
    g_col = jnp.swapaxes(gates.reshape(bsz, N_EXPERTS, cap), 0, 1).reshape(N_EXPERTS, bsz * cap, 1)
    xe = _gather(idx3, h2, cap)
    ye = _ffn(xe.reshape(N_EXPERTS, bsz * cap, d), w_gate[0], w_up[0], w_down[0], g_col, mod_l[:, 5], cap)
    return _combine(idx3, x1, ye.reshape(N_EXPERTS, bsz, cap, d), cap)
```

```python
import functools

import jax
import jax.numpy as jnp
from jax import lax
from jax.experimental import pallas as pl
from jax.experimental.pallas import tpu as pltpu

F32 = jnp.float32
BF16 = jnp.bfloat16

EPS = 1e-6
GRID_W = 64
LRU_WIDTH = 512
LRU_BLOCKS = 8
LRU_C = 8.0
CONV_W = 4
ATT_HEADS = 4
HEAD_DIM = 64
V_DIM = 2 * HEAD_DIM
QK_WIDTH = ATT_HEADS * 2 * HEAD_DIM
ATT_WIDTH = ATT_HEADS * V_DIM
ROPE_PAIRS = HEAD_DIM // 4
ROPE_BASE = 10000.0
N_EXPERTS = 16
EC_FACTOR = 2
N_MOD = 6
LAM_INIT = 0.2
LOG2_E = 1.4426950408889634

LANES = 128
SUBLANES = 8
MXU_DIM = 256
VMEM_LIMIT = 56 * 1024 * 1024


def _cparams(sem):
    return pltpu.CompilerParams(dimension_semantics=sem, vmem_limit_bytes=VMEM_LIMIT)


def _dot(a, b):
    return jnp.dot(a, b, preferred_element_type=F32)


def _dot_nt(a, b):
    return lax.dot_general(a, b, (((1,), (1,)), ((), ())), preferred_element_type=F32)


def _split_bf16(x):
    hi = x.astype(BF16)
    lo = (x - hi.astype(F32)).astype(BF16)
    return hi, lo


def _adaln_kernel(c_ref, w_ref, b_ref, o_ref):
    c = c_ref[...]
    s = c * jax.nn.sigmoid(c)
    s_hi, s_lo = _split_bf16(s)
    w_hi, w_lo = _split_bf16(w_ref[...])
    o_ref[...] = _dot(s_hi, w_hi) + _dot(s_hi, w_lo) + _dot(s_lo, w_hi) + b_ref[...]


def _adaln(cvec, w, b):
    rows, d = cvec.shape
    cols = w.shape[1]
    tn = cols // 4
    return pl.pallas_call(
        _adaln_kernel,
        grid=(cols // tn,),
        in_specs=[pl.BlockSpec((rows, d), lambda j: (0, 0)),
                  pl.BlockSpec((d, tn), lambda j: (0, j)),
                  pl.BlockSpec((1, tn), lambda j: (0, j))],
        out_specs=pl.BlockSpec((rows, tn), lambda j: (0, j)),
        out_shape=jax.ShapeDtypeStruct((rows, cols), F32),
        compiler_params=_cparams(("arbitrary",)),
        name="adaln",
    )(cvec, w, b)


def _swap_halves16(x):
    lane = lax.broadcasted_iota(jnp.int32, x.shape, 1)
    first = (lane % 32) < 16
    return jnp.where(first, pltpu.roll(x, LANES - 16, 1), pltpu.roll(x, 16, 1))


def _qk_norm(t, g, ones_bd):
    outs = []
    for c in range(t.shape[1] // MXU_DIM):
        tc = t[:, c * MXU_DIM:(c + 1) * MXU_DIM]
        hi, lo = _split_bf16(tc * tc)
        ssum = _dot(hi, ones_bd) + _dot(lo, ones_bd)
        outs.append(tc * lax.rsqrt(ssum * (1.0 / HEAD_DIM) + EPS) * g[:, c * MXU_DIM:(c + 1) * MXU_DIM])
    return outs


def _inproj_kernel(x_ref, shift_ref, scale_ref, g1_ref, w_ref, qg_ref, kg_ref, cos_ref, sin_ref, ones_ref,
                   xl_ref, gate_ref, q_ref, k_ref, v_ref, *, use_rope):
    x = x_ref[0]
    ms = jnp.mean(x * x, axis=-1, keepdims=True)
    h = x * lax.rsqrt(ms + EPS) * g1_ref[...]
    h = h * (1.0 + scale_ref[0]) + shift_ref[0]
    p = _dot(h.astype(BF16), w_ref[...])
    o1, o2, o3, o4 = LRU_WIDTH, 2 * LRU_WIDTH, 2 * LRU_WIDTH + QK_WIDTH, 2 * LRU_WIDTH + 2 * QK_WIDTH
    xl_ref[0] = p[:, :o1]
    gate_ref[0] = p[:, o1:o2]
    v_ref[0] = p[:, o4:].astype(BF16)
    ones_bd = ones_ref[...]
    qn = _qk_norm(p[:, o2:o3], qg_ref[...], ones_bd)
    kn = _qk_norm(p[:, o3:o4], kg_ref[...], ones_bd)
    scale = HEAD_DIM ** -0.5 * LOG2_E
    for src, dst, mul in ((qn, q_ref, scale), (kn, k_ref, 1.0)):
        for c, tc in enumerate(src):
            for hh in range(MXU_DIM // LANES):
                th = tc[:, hh * LANES:(hh + 1) * LANES]
                if use_rope:
                    th = th * cos_ref[...] + _swap_halves16(th) * sin_ref[...]
                col = c * MXU_DIM + hh * LANES
                dst[0, :, col:col + LANES] = (th * mul).astype(BF16)


def _inproj(x, shift, scale, g1, w_bf, qg, kg, cos_t, sin_t, ones_bd, use_rope):
    bsz, n, d = x.shape
    tm = min(512, n)
    wid = w_bf.shape[1]
    per_b = shift.shape[0] > 1
    mod_map = (lambda b, i: (b, 0, 0)) if per_b else (lambda b, i: (0, 0, 0))
    full = lambda b, i: (0, 0)
    tok = lambda b, i: (b, i, 0)
    out_shapes = (jax.ShapeDtypeStruct((bsz, n, LRU_WIDTH), F32),
                  jax.ShapeDtypeStruct((bsz, n, LRU_WIDTH), F32),
                  jax.ShapeDtypeStruct((bsz, n, QK_WIDTH), BF16),
                  jax.ShapeDtypeStruct((bsz, n, QK_WIDTH), BF16),
                  jax.ShapeDtypeStruct((bsz, n, ATT_WIDTH), BF16))
    return pl.pallas_call(
        functools.partial(_inproj_kernel, use_rope=use_rope),
        grid=(bsz, n // tm),
        in_specs=[pl.BlockSpec((1, tm, d), tok),
                  pl.BlockSpec((1, 1, d), mod_map),
                  pl.BlockSpec((1, 1, d), mod_map),
                  pl.BlockSpec((1, d), full),
                  pl.BlockSpec((d, wid), full),
                  pl.BlockSpec((1, QK_WIDTH), full),
                  pl.BlockSpec((1, QK_WIDTH), full),
                  pl.BlockSpec((tm, LANES), lambda b, i: (i, 0)),
                  pl.BlockSpec((tm, LANES), lambda b, i: (i, 0)),
                  pl.BlockSpec((MXU_DIM, MXU_DIM), full)],
        out_specs=[pl.BlockSpec((1, tm, LRU_WIDTH), tok),
                   pl.BlockSpec((1, tm, LRU_WIDTH), tok),
                   pl.BlockSpec((1, tm, QK_WIDTH), tok),
                   pl.BlockSpec((1, tm, QK_WIDTH), tok),
                   pl.BlockSpec((1, tm, ATT_WIDTH), tok)],
        out_shape=out_shapes,
        compiler_params=_cparams(("parallel", "parallel")),
        name="inproj_rope" if use_rope else "inproj_ctx",
    )(x, shift, scale, g1, w_bf, qg, kg, cos_t, sin_t, ones_bd)


LRU_HALF = LRU_WIDTH // 2
LRU_CHUNK = 512


def _conv_chunk(x_ref, t0, rows, total, cw, cb):
    x = x_ref[0, pl.ds(t0, rows), :]
    prev_start = pl.multiple_of(jnp.maximum(t0 - SUBLANES, 0), SUBLANES)
    next_start = pl.multiple_of(jnp.minimum(t0 + rows, total - SUBLANES), SUBLANES)
    prev = jnp.where(t0 > 0, x_ref[0, pl.ds(prev_start, SUBLANES), :], 0.0)
    nxt = jnp.where(t0 + rows < total, x_ref[0, pl.ds(next_start, SUBLANES), :], 0.0)
    xe = jnp.concatenate([prev, x, nxt], axis=0)
    acc = cb
    for k in range(CONV_W):
        off = SUBLANES - 1 + k
        acc = acc + xe[off:off + rows] * cw[k:k + 1]
    return acc


def _sigmoid(x):
    return 0.5 * jnp.tanh(0.5 * x) + 0.5


def _lru_gates(xc, w, bias, sp):
    pre = _dot(xc.astype(BF16), w) + bias
    r = _sigmoid(pre[:, :LRU_HALF])
    i = _sigmoid(pre[:, LRU_HALF:])
    log_a = (-LRU_C * r) * sp
    a = jnp.exp(log_a)
    mult = jnp.sqrt(-jnp.tanh(log_a) * (a * a + 1.0))
    return a, mult, i * xc


def _scan_chunk(a, u, h, reverse):
    rows, width = a.shape
    groups = rows // SUBLANES
    a = a.reshape(groups, SUBLANES, width)
    u = u.reshape(groups, SUBLANES, width)
    sub = lax.broadcasted_iota(jnp.int32, a.shape, 1)
    s = 1
    while s < SUBLANES:
        if reverse:
            m = sub < SUBLANES - s
            a_sh = pltpu.roll(a, SUBLANES - s, 1)
            u_sh = pltpu.roll(u, SUBLANES - s, 1)
        else:
            m = sub >= s
            a_sh = pltpu.roll(a, s, 1)
            u_sh = pltpu.roll(u, s, 1)
        u = jnp.where(m, a * u_sh + u, u)
        a = jnp.where(m, a * a_sh, a)
        s *= 2
    outs = [None] * groups
    order = range(groups - 1, -1, -1) if reverse else range(groups)
    for g in order:
        hg = a[g] * h + u[g]
        h = hg[0:1] if reverse else hg[SUBLANES - 1:SUBLANES]
        outs[g] = hg
    return jnp.concatenate(outs, axis=0), h


def _rglru_kernel(xl_ref, xc_ref, gate_ref, cw_ref, cb_ref, wf_ref, wb_ref, bf_ref, bb_ref, lam_ref,
                  o_ref, hf_ref, cl_ref, cc_ref, *, n, ctx_len):
    cw = cw_ref[...]
    cb = cb_ref[...]
    tc = min(LRU_CHUNK, ctx_len)
    tl = min(LRU_CHUNK, n)

    def direction(d, w_ref, b_ref):
        reverse = d == 1
        w = w_ref[0]
        bias = b_ref[0]
        z = -lam_ref[0, pl.ds(d, 1), :]
        sp = jnp.maximum(z, 0.0) + jnp.log1p(jnp.exp(-jnp.abs(z)))
        first_row = ctx_len - 1 if reverse else 0

        def conv(src_ref, cache_ref, t0, rows, total):
            if reverse:
                return cache_ref[pl.ds(t0, rows), :]
            xc = _conv_chunk(src_ref, t0, rows, total, cw, cb)
            cache_ref[pl.ds(t0, rows), :] = xc
            return xc

        def ctx_step(c, h):
            cc = (ctx_len // tc - 1 - c) if reverse else c
            t0 = pl.multiple_of(cc * tc, SUBLANES)
            xc = conv(xc_ref, cc_ref, t0, tc, ctx_len)
            a, mult, ix = _lru_gates(xc, w, bias, sp)
            row = lax.broadcasted_iota(jnp.int32, a.shape, 0) + t0
            mult = jnp.where(row == first_row, 1.0, mult)
            _, h = _scan_chunk(a, mult * ix, h, reverse)
            return h

        h = lax.fori_loop(0, ctx_len // tc, ctx_step, jnp.zeros((1, LRU_HALF), F32))

        def lat_step(c, h):
            cc = (n // tl - 1 - c) if reverse else c
            t0 = pl.multiple_of(cc * tl, SUBLANES)
            xc = conv(xl_ref, cl_ref, t0, tl, n)
            a, mult, ix = _lru_gates(xc, w, bias, sp)
            hs, h = _scan_chunk(a, mult * ix, h, reverse)
            if reverse:
                y = (hf_ref[pl.ds(t0, tl), :] + hs) * jax.nn.gelu(gate_ref[0, pl.ds(t0, tl), :])
                o_ref[0, pl.ds(t0, tl), :] = y.astype(BF16)
            else:
                hf_ref[pl.ds(t0, tl), :] = hs
            return h

        lax.fori_loop(0, n // tl, lat_step, h)

    direction(0, wf_ref, bf_ref)
    direction(1, wb_ref, bb_ref)


def _rglru(xl, xc, gate, conv_w, conv_b, w_f, w_b, b_f, b_b, lam):
    bsz, n, _ = xl.shape
    ctx_len = xc.shape[1]
    halves = LRU_WIDTH // LRU_HALF
    tokh = lambda b, hf: (b, 0, hf)
    return pl.pallas_call(
        functools.partial(_rglru_kernel, n=n, ctx_len=ctx_len),
        grid=(bsz, halves),
        in_specs=[pl.BlockSpec((1, n, LRU_HALF), tokh),
                  pl.BlockSpec((1, ctx_len, LRU_HALF), tokh),
                  pl.BlockSpec((1, n, LRU_HALF), tokh),
                  pl.BlockSpec((CONV_W, LRU_HALF), lambda b, hf: (0, hf)),
                  pl.BlockSpec((1, LRU_HALF), lambda b, hf: (0, hf)),
                  pl.BlockSpec((1, LRU_HALF, 2 * LRU_HALF), lambda b, hf: (hf, 0, 0)),
                  pl.BlockSpec((1, LRU_HALF, 2 * LRU_HALF), lambda b, hf: (hf, 0, 0)),
                  pl.BlockSpec((1, 1, 2 * LRU_HALF), lambda b, hf: (hf, 0, 0)),
                  pl.BlockSpec((1, 1, 2 * LRU_HALF), lambda b, hf: (hf, 0, 0)),
                  pl.BlockSpec((1, 2, LRU_HALF), lambda b, hf: (hf, 0, 0))],
        out_specs=pl.BlockSpec((1, n, LRU_HALF), tokh),
        out_shape=jax.ShapeDtypeStruct((bsz, n, LRU_WIDTH), BF16),
        scratch_shapes=[pltpu.VMEM((n, LRU_HALF), F32),
                        pltpu.VMEM((n, LRU_HALF), F32),
                        pltpu.VMEM((ctx_len, LRU_HALF), F32)],
        compiler_params=_cparams(("parallel", "parallel")),
        name="rglru",
    )(xl, xc, gate, conv_w, conv_b, w_f, w_b, b_f, b_b, lam)


def _lru_gate_weights(wa, ba, wi, bi):
    halves = LRU_WIDTH // LRU_HALF
    per = LRU_BLOCKS // halves
    bw = LRU_WIDTH // LRU_BLOCKS

    def dense(w):
        w = w.reshape(halves, per, bw, bw)
        eye = jnp.eye(per, dtype=w.dtype)
        return jnp.einsum("hpij,pq->hpiqj", w, eye).reshape(halves, per * bw, per * bw)

    w_cat = jnp.concatenate([dense(wa), dense(wi)], axis=-1).astype(BF16)
    b_cat = jnp.concatenate([ba.reshape(halves, 1, LRU_HALF), bi.reshape(halves, 1, LRU_HALF)], axis=-1)
    return w_cat, b_cat


ATT_TQ = 256
ATT_TK = 256


def _attn_kernel(q_ref, kc_ref, vc_ref, kl_ref, vl_ref, lq1_ref, lk1_ref, lq2_ref, lk2_ref, sg_ref, o_ref,
                 *, n, tk):
    q = q_ref[0]
    tq = q.shape[0]
    lane = lax.broadcasted_iota(jnp.int32, q.shape, 1)
    zero = jnp.zeros_like(q)
    q2 = jnp.concatenate([jnp.where(lane < HEAD_DIM, q, zero), jnp.where(lane >= HEAD_DIM, q, zero)], axis=0)

    def ext(v):
        ln = lax.broadcasted_iota(jnp.int32, v.shape, 1)
        return jnp.concatenate([v, jnp.where(ln == 0, 1.0, 0.0).astype(BF16)], axis=1)

    def step(k, v, carry):
        m, acc = carry
        s = _dot_nt(q2, k)
        m_new = jnp.maximum(m, jnp.max(s, axis=1, keepdims=True))
        p = jnp.exp2(s - m_new).astype(BF16)
        acc = jnp.exp2(m - m_new) * acc + _dot(p, ext(v))
        return m_new, acc

    carry = (jnp.full((2 * tq, 1), -1e30, F32), jnp.zeros((2 * tq, 2 * V_DIM), F32))
    carry = step(kc_ref[0], vc_ref[0], carry)
    for j in range(n // tk):
        carry = step(kl_ref[0, j * tk:(j + 1) * tk, :], vl_ref[0, j * tk:(j + 1) * tk, :], carry)
    _, acc = carry
    o = acc[:, :V_DIM] / acc[:, V_DIM:V_DIM + 1]
    lam = (jnp.exp(jnp.sum(lq1_ref[...] * lk1_ref[...], keepdims=True))
           - jnp.exp(jnp.sum(lq2_ref[...] * lk2_ref[...], keepdims=True)) + LAM_INIT)
    att = o[:tq] - lam * o[tq:]
    ms = jnp.mean(att * att, axis=-1, keepdims=True)
    y = att * lax.rsqrt(ms + EPS) * sg_ref[...]
    o_ref[0] = (y * (1.0 - LAM_INIT)).astype(BF16)


def _attention(q, kc, vc, kl, vl, lq1, lk1, lq2, lk2, sg):
    bsz, n, _ = q.shape
    ctx_len = kc.shape[1]
    tq = min(ATT_TQ, n)
    tk = min(ATT_TK, n)
    vec = lambda b, h, i: (0, 0)
    return pl.pallas_call(
        functools.partial(_attn_kernel, n=n, tk=tk),
        grid=(bsz, ATT_HEADS, n // tq),
        in_specs=[pl.BlockSpec((1, tq, V_DIM), lambda b, h, i: (b, i, h)),
                  pl.BlockSpec((1, ctx_len, V_DIM), lambda b, h, i: (b, 0, h)),
                  pl.BlockSpec((1, ctx_len, V_DIM), lambda b, h, i: (b, 0, h)),
                  pl.BlockSpec((1, n, V_DIM), lambda b, h, i: (b, 0, h)),
                  pl.BlockSpec((1, n, V_DIM), lambda b, h, i: (b, 0, h)),
                  pl.BlockSpec((1, HEAD_DIM), vec),
                  pl.BlockSpec((1, HEAD_DIM), vec),
                  pl.BlockSpec((1, HEAD_DIM), vec),
                  pl.BlockSpec((1, HEAD_DIM), vec),
                  pl.BlockSpec((1, V_DIM), vec)],
        out_specs=pl.BlockSpec((1, tq, V_DIM), lambda b, h, i: (b, i, h)),
        out_shape=jax.ShapeDtypeStruct((bsz, n, ATT_WIDTH), BF16),
        compiler_params=_cparams(("parallel", "parallel", "arbitrary")),
        name="diff_attention",
    )(q, kc, vc, kl, vl, lq1, lk1, lq2, lk2, sg)


def _outproj_kernel(lru_ref, att_ref, x_ref, g1_ref, shift_ref, scale_ref, n2_ref, wo_ref, wr_ref,
                    x1_ref, h2_ref, lg_ref):
    mix = _dot(lru_ref[0], wo_ref[:LRU_WIDTH, :]) + _dot(att_ref[0], wo_ref[LRU_WIDTH:, :])
    x1 = x_ref[0] + g1_ref[0] * mix
    x1_ref[0] = x1
    ms = jnp.mean(x1 * x1, axis=-1, keepdims=True)
    h2 = x1 * lax.rsqrt(ms + EPS) * n2_ref[...]
    h2 = h2 * (1.0 + scale_ref[0]) + shift_ref[0]
    h2_ref[0] = h2
    lg_ref[0] = _dot_nt(wr_ref[...], h2.astype(BF16))


def _outproj(lru, att, x, g1, shift2, scale2, n2g, wo_bf, wr_bf):
    bsz, n, d = x.shape
    tm = min(512, n)
    tok = lambda b, i: (b, i, 0)
    mod = lambda b, i: (b, 0, 0)
    full = lambda b, i: (0, 0)
    return pl.pallas_call(
        _outproj_kernel,
        grid=(bsz, n // tm),
        in_specs=[pl.BlockSpec((1, tm, LRU_WIDTH), tok),
                  pl.BlockSpec((1, tm, ATT_WIDTH), tok),
                  pl.BlockSpec((1, tm, d), tok),
                  pl.BlockSpec((1, 1, d), mod),
                  pl.BlockSpec((1, 1, d), mod),
                  pl.BlockSpec((1, 1, d), mod),
                  pl.BlockSpec((1, d), full),
                  pl.BlockSpec((d, d), full),
                  pl.BlockSpec((N_EXPERTS, d), full)],
        out_specs=[pl.BlockSpec((1, tm, d), tok),
                   pl.BlockSpec((1, tm, d), tok),
                   pl.BlockSpec((1, N_EXPERTS, tm), lambda b, i: (b, 0, i))],
        out_shape=(jax.ShapeDtypeStruct((bsz, n, d), F32),
                   jax.ShapeDtypeStruct((bsz, n, d), F32),
                   jax.ShapeDtypeStruct((bsz, N_EXPERTS, n), F32)),
        compiler_params=_cparams(("parallel", "parallel")),
        name="outproj_router",
    )(lru, att, x, g1, shift2, scale2, n2g, wo_bf, wr_bf)


TOPK_UNROLL = 4


def _topk_kernel(lg_ref, idx_ref, g_ref, aff_ref, rank_ref, start_ref, tot_ref, *, cap):
    lg = lg_ref[0]
    ne, nb, blk = lg.shape
    ex = jnp.exp(lg - jnp.max(lg, axis=0, keepdims=True))
    aff = ex / jnp.sum(ex, axis=0, keepdims=True)
    aff_ref[...] = aff
    capf = float(cap)

    def bisect(_, c):
        lo, hi = c
        mid = lo + ((hi - lo) >> 1)
        cnt = jnp.sum(jnp.where(aff >= pltpu.bitcast(mid, F32), 1.0, 0.0), axis=(1, 2), keepdims=True)
        ge = cnt >= capf
        return jnp.where(ge, mid, lo), jnp.where(ge, hi, mid)

    lo0 = jnp.zeros((ne, 1, 1), jnp.int32)
    hi0 = jnp.full((ne, 1, 1), 0x3F800001, jnp.int32)
    thr, nxt = lax.fori_loop(0, 31, bisect, (lo0, hi0))
    gt = aff >= pltpu.bitcast(nxt, F32)
    gtf = jnp.where(gt, 1.0, 0.0)
    eqf = jnp.where(jnp.logical_and(aff >= pltpu.bitcast(thr, F32), jnp.logical_not(gt)), 1.0, 0.0)
    need = capf - jnp.sum(gtf, axis=(1, 2), keepdims=True)

    rows = ne * nb
    tri = (lax.broadcasted_iota(jnp.int32, (blk, blk), 0)
           <= lax.broadcasted_iota(jnp.int32, (blk, blk), 1)).astype(BF16)
    r_i = lax.broadcasted_iota(jnp.int32, (rows, rows), 0)
    c_i = lax.broadcasted_iota(jnp.int32, (rows, rows), 1)
    earlier = jnp.logical_and(c_i < r_i, c_i >= (r_i // nb) * nb).astype(BF16)

    def prefix(x3):
        cin = _dot(x3.reshape(rows, blk).astype(BF16), tri)
        tot = jnp.broadcast_to(cin[:, blk - 1:blk], cin.shape)
        start = _dot(earlier, tot.astype(BF16))
        return cin.reshape(x3.shape), start.reshape(x3.shape), tot.reshape(x3.shape)

    cin, start, _ = prefix(eqf)
    sel = gtf + eqf * jnp.where(cin + start <= need, 1.0, 0.0)
    cin, start, tot = prefix(sel)
    rank_ref[...] = cin * sel
    start_ref[...] = start
    tot_ref[...] = tot

    slot = (lax.broadcasted_iota(jnp.int32, (1, cap), 1) + 1).astype(F32)
    row_id = lax.broadcasted_iota(jnp.int32, (nb, 1), 0).astype(F32)
    lane_id = lax.broadcasted_iota(jnp.int32, (blk, 1), 0).astype(F32)

    def per_expert(e):
        a = aff_ref[e]
        st = start_ref[e][:, 0:1]
        in_row = jnp.logical_and(st < slot, slot <= st + tot_ref[e][:, 0:1])
        in_row_bf = jnp.where(in_row, 1.0, 0.0).astype(BF16)
        row_of = jnp.sum(jnp.where(in_row, row_id, 0.0), axis=0, keepdims=True)
        rank_need = slot - jnp.sum(jnp.where(in_row, st, 0.0), axis=0, keepdims=True)

        def pick(x_bf):
            return lax.dot_general(x_bf, in_row_bf, (((0,), (0,)), ((), ())), preferred_element_type=F32)

        hit = pick(rank_ref[e].astype(BF16)) == rank_need
        lane_of = jnp.sum(jnp.where(hit, lane_id, 0.0), axis=0, keepdims=True)
        a_hi = a.astype(BF16)
        a_mid = (a - a_hi.astype(F32)).astype(BF16)
        a_lo = (a - a_hi.astype(F32) - a_mid.astype(F32)).astype(BF16)
        a_sel = (pick(a_hi) + pick(a_mid)) + pick(a_lo)
        idx_ref[0, e] = (row_of * float(blk) + lane_of).astype(jnp.int32)
        g_ref[0, e] = jnp.sum(jnp.where(hit, a_sel, 0.0), axis=0, keepdims=True)

    def expert_group(gi, carry):
        for r in range(TOPK_UNROLL):
            per_expert(gi * TOPK_UNROLL + r)
        return carry

    lax.fori_loop(0, ne // TOPK_UNROLL, expert_group, 0)


def _topk(logits_t, cap):
    bsz, _, n = logits_t.shape
    blk = min(LANES, n)
    nb = n // blk
    slab = pltpu.VMEM((N_EXPERTS, nb, blk), F32)
    return pl.pallas_call(
        functools.partial(_topk_kernel, cap=cap),
        grid=(bsz,),
        in_specs=[pl.BlockSpec((1, N_EXPERTS, nb, blk), lambda b: (b, 0, 0, 0))],
        out_specs=[pl.BlockSpec((1, N_EXPERTS, 1, cap), lambda b: (b, 0, 0, 0)),
                   pl.BlockSpec((1, N_EXPERTS, 1, cap), lambda b: (b, 0, 0, 0))],
        out_shape=(jax.ShapeDtypeStruct((bsz, N_EXPERTS, 1, cap), jnp.int32),
                   jax.ShapeDtypeStruct((bsz, N_EXPERTS, 1, cap), F32)),
        scratch_shapes=[slab, slab, slab, slab],
        compiler_params=_cparams(("parallel",)),
        name="expert_topk",
    )(logits_t.reshape(bsz, N_EXPERTS, nb, blk))


GATHER_UNROLL = 4


def _gather_kernel(idx_ref, h_ref, o_ref, *, cap):
    unroll = GATHER_UNROLL if cap % (SUBLANES * GATHER_UNROLL) == 0 else 1

    def body(jj, carry):
        for u in range(unroll):
            j0 = pl.multiple_of((jj * unroll + u) * SUBLANES, SUBLANES)
            rows = [h_ref[0, pl.ds(idx_ref[0, 0, j0 + r], 1), :] for r in range(SUBLANES)]
            o_ref[0, 0, pl.ds(j0, SUBLANES), :] = jnp.concatenate(rows, axis=0).astype(BF16)
        return carry

    lax.fori_loop(0, cap // (SUBLANES * unroll), body, 0)


def _gather(idx3, h2, cap):
    bsz, n, d = h2.shape
    return pl.pallas_call(
        functools.partial(_gather_kernel, cap=cap),
        grid=(bsz, N_EXPERTS),
        in_specs=[pl.BlockSpec((1, 1, cap), lambda b, e: (b * N_EXPERTS + e, 0, 0), memory_space=pltpu.SMEM),
                  pl.BlockSpec((1, n, d), lambda b, e: (b, 0, 0))],
        out_specs=pl.BlockSpec((1, 1, cap, d), lambda b, e: (e, b, 0, 0)),
        out_shape=jax.ShapeDtypeStruct((N_EXPERTS, bsz, cap, d), BF16),
        compiler_params=_cparams(("parallel", "arbitrary")),
        name="expert_gather",
    )(idx3, h2)


FFN_TF = 256


FFN_SUB = 512


def _ffn_kernel(x_ref, wg_ref, wu_ref, wd_ref, g_ref, gate_ref, o_ref, *, cap):
    i = pl.program_id(1)
    f = pl.program_id(2)
    nf = pl.num_programs(2)
    tm = x_ref.shape[1]
    sub = min(FFN_SUB, tm, cap)

    def body(first, last):
        wgu = jnp.concatenate([wg_ref[0].astype(BF16), wu_ref[0].astype(BF16)], axis=1)
        wd = wd_ref[0].astype(BF16)
        tf = wd.shape[0]
        for r in range(tm // sub):
            rows = slice(r * sub, (r + 1) * sub)
            gu = _dot(x_ref[0, rows, :], wgu)
            hg = gu[:, :tf]
            h = (hg * jax.nn.sigmoid(hg)) * gu[:, tf:]
            y = _dot(h.astype(BF16), wd)
            if not first:
                y = o_ref[0, rows, :] + y
            if last:
                sample = (i * tm + r * sub) // cap
                y = y * (g_ref[0, rows, :] * gate_ref[sample])
            o_ref[0, rows, :] = y

    @pl.when(f == 0)
    def _():
        body(True, False)

    @pl.when(jnp.logical_and(f > 0, f < nf - 1))
    def _():
        body(False, False)

    @pl.when(f == nf - 1)
    def _():
        body(False, True)


def _ffn(xe, w_gate, w_up, w_down, g_col, gate2, cap):
    ne, m, d = xe.shape
    dff = w_gate.shape[2]
    tm = min(2048, m)
    tf = FFN_TF
    assert dff // tf > 1 and cap % min(FFN_SUB, tm, cap) == 0
    return pl.pallas_call(
        functools.partial(_ffn_kernel, cap=cap),
        grid=(ne, m // tm, dff // tf),
        in_specs=[pl.BlockSpec((1, tm, d), lambda e, i, f: (e, i, 0)),
                  pl.BlockSpec((1, d, tf), lambda e, i, f: (e, 0, f)),
                  pl.BlockSpec((1, d, tf), lambda e, i, f: (e, 0, f)),
                  pl.BlockSpec((1, tf, d), lambda e, i, f: (e, f, 0)),
                  pl.BlockSpec((1, tm, 1), lambda e, i, f: (e, i, 0)),
                  pl.BlockSpec(gate2.shape, lambda e, i, f: (0, 0, 0))],
        out_specs=pl.BlockSpec((1, tm, d), lambda e, i, f: (e, i, 0)),
        out_shape=jax.ShapeDtypeStruct((ne, m, d), F32),
        compiler_params=_cparams(("parallel", "parallel", "arbitrary")),
        name="expert_ffn",
    )(xe, w_gate, w_up, w_down, g_col, gate2)


def _combine_kernel(idx_ref, x1_hbm, y_ref, o_ref, sem, *, cap):
    b = pl.program_id(0)
    e = pl.program_id(1)

    @pl.when(e == 0)
    def _():
        init = pltpu.make_async_copy(x1_hbm.at[b], o_ref.at[0], sem)
        init.start()
        init.wait()

    sub = lax.broadcasted_iota(jnp.int32, (SUBLANES, o_ref.shape[2]), 0)

    def body(jj, carry):
        j0 = pl.multiple_of(jj * SUBLANES, SUBLANES)
        ys = y_ref[0, 0, pl.ds(j0, SUBLANES), :]
        for r in range(0, SUBLANES, 2):
            t0 = idx_ref[0, 0, j0 + r]
            t1 = idx_ref[0, 0, j0 + r + 1]
            b0 = pl.multiple_of((t0 >> 3) << 3, SUBLANES)
            b1 = pl.multiple_of((t1 >> 3) << 3, SUBLANES)
            tile0 = o_ref[0, pl.ds(b0, SUBLANES), :]
            tile1 = o_ref[0, pl.ds(b1, SUBLANES), :]
            u0 = jnp.where(sub == (t0 & (SUBLANES - 1)), ys[r:r + 1], 0.0)
            u1 = jnp.where(sub == (t1 & (SUBLANES - 1)), ys[r + 1:r + 2], 0.0)
            o_ref[0, pl.ds(b0, SUBLANES), :] = tile0 + u0
            o_ref[0, pl.ds(b1, SUBLANES), :] = tile1 + u1 + jnp.where(b0 == b1, u0, 0.0)
        return carry

    lax.fori_loop(0, cap // SUBLANES, body, 0)


def _combine(idx3, x1, ye, cap):
    bsz, n, d = x1.shape
    resident = lambda b, e: (b, 0, 0)
    return pl.pallas_call(
        functools.partial(_combine_kernel, cap=cap),
        grid=(bsz, N_EXPERTS),
        in_specs=[pl.BlockSpec((1, 1, cap), lambda b, e: (b * N_EXPERTS + e, 0, 0), memory_space=pltpu.SMEM),
                  pl.BlockSpec(memory_space=pl.ANY),
                  pl.BlockSpec((1, 1, cap, d), lambda b, e: (e, b, 0, 0))],
        out_specs=pl.BlockSpec((1, n, d), resident, pipeline_mode=pl.Buffered(1)),
        out_shape=jax.ShapeDtypeStruct((bsz, n, d), F32),
        scratch_shapes=[pltpu.SemaphoreType.DMA],
        compiler_params=_cparams(("parallel", "arbitrary")),
        name="expert_combine",
    )(idx3, x1, ye)


def _rope_tables(n):
    rows = n // GRID_W
    row = jnp.repeat(jnp.arange(rows), GRID_W).astype(F32)
    col = jnp.tile(jnp.arange(GRID_W), rows).astype(F32)
    inv = ROPE_BASE ** (-jnp.arange(ROPE_PAIRS, dtype=F32) / ROPE_PAIRS)
    ang_r = row[:, None] * inv
    ang_c = col[:, None] * inv
    cos64 = jnp.concatenate([jnp.cos(ang_r), jnp.cos(ang_r), jnp.cos(ang_c), jnp.cos(ang_c)], axis=1)
    sin64 = jnp.concatenate([-jnp.sin(ang_r), jnp.sin(ang_r), -jnp.sin(ang_c), jnp.sin(ang_c)], axis=1)
    return jnp.tile(cos64, (1, 2)), jnp.tile(sin64, (1, 2))


def kernel(x, c, ctx, c_ctx, w_ada, b_ada, norm1_g, norm2_g, w_in, conv_w, conv_b, lru_wa, lru_ba, lru_wi,
           lru_bi, lru_lambda, q_norm_g, k_norm_g, lambda_q1, lambda_k1, lambda_q2, lambda_k2, subln_g, w_out,
           w_router, w_gate, w_up, w_down):
    assert w_ada.shape[0] == 1, "single-layer configuration"
    bsz, n, d = x.shape
    ctx_len = ctx.shape[1]
    cap = EC_FACTOR * n // N_EXPERTS

    rows = ((bsz + 1 + SUBLANES - 1) // SUBLANES) * SUBLANES
    cvec = jnp.zeros((rows, d), F32).at[:bsz].set(c).at[bsz].set(c_ctx)
    mod = _adaln(cvec, w_ada[0], b_ada[0][None]).reshape(rows, N_MOD, 1, d)
    mod_l = mod[:bsz]
    mod_c = mod[bsz:bsz + 1]

    w_in_bf = w_in[0].astype(BF16)
    w_out_bf = w_out[0].astype(BF16)
    w_r_bf = w_router[0].T.astype(BF16)
    qg = jnp.tile(q_norm_g[0], QK_WIDTH // HEAD_DIM)[None]
    kg = jnp.tile(k_norm_g[0], QK_WIDTH // HEAD_DIM)[None]
    seg = jnp.arange(MXU_DIM) // HEAD_DIM
    ones_bd = (seg[:, None] == seg[None, :]).astype(BF16)
    cos_t, sin_t = _rope_tables(n)
    cos_c = jnp.ones((ctx_len, LANES), F32)
    sin_c = jnp.zeros((ctx_len, LANES), F32)
    g1 = norm1_g[0][None]

    xl, gate_l, q_l, k_l, v_l = _inproj(x, mod_l[:, 0], mod_l[:, 1], g1, w_in_bf, qg, kg, cos_t, sin_t,
                                        ones_bd, True)
    xc, _, _, k_c, v_c = _inproj(ctx, mod_c[:, 0], mod_c[:, 1], g1, w_in_bf, qg, kg, cos_c, sin_c,
                                 ones_bd, False)

    w_f, b_f = _lru_gate_weights(lru_wa[0, 0], lru_ba[0, 0], lru_wi[0, 0], lru_bi[0, 0])
    w_b, b_b = _lru_gate_weights(lru_wa[0, 1], lru_ba[0, 1], lru_wi[0, 1], lru_bi[0, 1])
    halves = LRU_WIDTH // LRU_HALF
    lam_h = lru_lambda[0].reshape(2, halves, LRU_HALF).transpose(1, 0, 2)
    lru = _rglru(xl, xc, gate_l, conv_w[0], conv_b[0][None], w_f, w_b, b_f, b_b, lam_h)

    att = _attention(q_l, k_c, v_c, k_l, v_l, lambda_q1, lambda_k1, lambda_q2, lambda_k2, subln_g)

    x1, h2, logits = _outproj(lru, att, x, mod_l[:, 2], mod_l[:, 3], mod_l[:, 4], norm2_g[0][None],
                              w_out_bf, w_r_bf)

    idx, gates = _topk(logits, cap)
    idx3 = idx.reshape(bsz * N_EXPERTS, 1, cap)
    g_col = jnp.swapaxes(gates.reshape(bsz, N_EXPERTS, cap), 0, 1).reshape(N_EXPERTS, bsz * cap, 1)
    xe = _gather(idx3, h2, cap)
    ye = _ffn(xe.reshape(N_EXPERTS, bsz * cap, d), w_gate[0], w_up[0], w_down[0], g_col, mod_l[:, 5], cap)
    return _combine(idx3, x1, ye.reshape(N_EXPERTS, bsz, cap, d), cap)
```

```python
import functools

import jax
import jax.numpy as jnp
from jax import lax
from jax.experimental import pallas as pl
from jax.experimental.pallas import tpu as pltpu

F32 = jnp.float32
BF16 = jnp.bfloat16

EPS = 1e-6
GRID_W = 64
LRU_WIDTH = 512
LRU_BLOCKS = 8
LRU_C = 8.0
CONV_W = 4
ATT_HEADS = 4
HEAD_DIM = 64
V_DIM = 2 * HEAD_DIM
QK_WIDTH = ATT_HEADS * 2 * HEAD_DIM
ATT_WIDTH = ATT_HEADS * V_DIM
ROPE_PAIRS = HEAD_DIM // 4
ROPE_BASE = 10000.0
N_EXPERTS = 16
EC_FACTOR = 2
N_MOD = 6
LAM_INIT = 0.2
LOG2_E = 1.4426950408889634

LANES = 128
SUBLANES = 8
MXU_DIM = 256
VMEM_LIMIT = 56 * 1024 * 1024


def _cparams(sem):
    return pltpu.CompilerParams(dimension_semantics=sem, vmem_limit_bytes=VMEM_LIMIT)


def _dot(a, b):
    return jnp.dot(a, b, preferred_element_type=F32)


def _dot_nt(a, b):
    return lax.dot_general(a, b, (((1,), (1,)), ((), ())), preferred_element_type=F32)


def _split_bf16(x):
    hi = x.astype(BF16)
    lo = (x - hi.astype(F32)).astype(BF16)
    return hi, lo


def _adaln_kernel(c_ref, w_ref, b_ref, o_ref):
    c = c_ref[...]
    s = c * jax.nn.sigmoid(c)
    s_hi, s_lo = _split_bf16(s)
    w_hi, w_lo = _split_bf16(w_ref[...])
    o_ref[...] = _dot(s_hi, w_hi) + _dot(s_hi, w_lo) + _dot(s_lo, w_hi) + b_ref[...]


def _adaln(cvec, w, b):
    rows, d = cvec.shape
    cols = w.shape[1]
    tn = cols // 4
    return pl.pallas_call(
        _adaln_kernel,
        grid=(cols // tn,),
        in_specs=[pl.BlockSpec((rows, d), lambda j: (0, 0)),
                  pl.BlockSpec((d, tn), lambda j: (0, j)),
                  pl.BlockSpec((1, tn), lambda j: (0, j))],
        out_specs=pl.BlockSpec((rows, tn), lambda j: (0, j)),
        out_shape=jax.ShapeDtypeStruct((rows, cols), F32),
        compiler_params=_cparams(("arbitrary",)),
        name="adaln",
    )(cvec, w, b)


def _swap_halves16(x):
    lane = lax.broadcasted_iota(jnp.int32, x.shape, 1)
    first = (lane % 32) < 16
    return jnp.where(first, pltpu.roll(x, LANES - 16, 1), pltpu.roll(x, 16, 1))


def _qk_norm(t, g, ones_bd):
    outs = []
    for c in range(t.shape[1] // MXU_DIM):
        tc = t[:, c * MXU_DIM:(c + 1) * MXU_DIM]
        hi, lo = _split_bf16(tc * tc)
        ssum = _dot(hi, ones_bd) + _dot(lo, ones_bd)
        outs.append(tc * lax.rsqrt(ssum * (1.0 / HEAD_DIM) + EPS) * g[:, c * MXU_DIM:(c + 1) * MXU_DIM])
    return outs


def _inproj_kernel(x_ref, shift_ref, scale_ref, g1_ref, w_ref, qg_ref, kg_ref, cos_ref, sin_ref, ones_ref,
                   xl_ref, gate_ref, q_ref, k_ref, v_ref, *, use_rope):
    x = x_ref[0]
    ms = jnp.mean(x * x, axis=-1, keepdims=True)
    h = x * lax.rsqrt(ms + EPS) * g1_ref[...]
    h = h * (1.0 + scale_ref[0]) + shift_ref[0]
    p = _dot(h.astype(BF16), w_ref[...])
    o1, o2, o3, o4 = LRU_WIDTH, 2 * LRU_WIDTH, 2 * LRU_WIDTH + QK_WIDTH, 2 * LRU_WIDTH + 2 * QK_WIDTH
    xl_ref[0] = p[:, :o1]
    gate_ref[0] = p[:, o1:o2]
    v_ref[0] = p[:, o4:].astype(BF16)
    ones_bd = ones_ref[...]
    qn = _qk_norm(p[:, o2:o3], qg_ref[...], ones_bd)
    kn = _qk_norm(p[:, o3:o4], kg_ref[...], ones_bd)
    scale = HEAD_DIM ** -0.5 * LOG2_E
    for src, dst, mul in ((qn, q_ref, scale), (kn, k_ref, 1.0)):
        for c, tc in enumerate(src):
            for hh in range(MXU_DIM // LANES):
                th = tc[:, hh * LANES:(hh + 1) * LANES]
                if use_rope:
                    th = th * cos_ref[...] + _swap_halves16(th) * sin_ref[...]
                col = c * MXU_DIM + hh * LANES
                dst[0, :, col:col + LANES] = (th * mul).astype(BF16)


def _inproj(x, shift, scale, g1, w_bf, qg, kg, cos_t, sin_t, ones_bd, use_rope):
    bsz, n, d = x.shape
    tm = min(512, n)
    wid = w_bf.shape[1]
    per_b = shift.shape[0] > 1
    mod_map = (lambda b, i: (b, 0, 0)) if per_b else (lambda b, i: (0, 0, 0))
    full = lambda b, i: (0, 0)
    tok = lambda b, i: (b, i, 0)
    out_shapes = (jax.ShapeDtypeStruct((bsz, n, LRU_WIDTH), F32),
                  jax.ShapeDtypeStruct((bsz, n, LRU_WIDTH), F32),
                  jax.ShapeDtypeStruct((bsz, n, QK_WIDTH), BF16),
                  jax.ShapeDtypeStruct((bsz, n, QK_WIDTH), BF16),
                  jax.ShapeDtypeStruct((bsz, n, ATT_WIDTH), BF16))
    return pl.pallas_call(
        functools.partial(_inproj_kernel, use_rope=use_rope),
        grid=(bsz, n // tm),
        in_specs=[pl.BlockSpec((1, tm, d), tok),
                  pl.BlockSpec((1, 1, d), mod_map),
                  pl.BlockSpec((1, 1, d), mod_map),
                  pl.BlockSpec((1, d), full),
                  pl.BlockSpec((d, wid), full),
                  pl.BlockSpec((1, QK_WIDTH), full),
                  pl.BlockSpec((1, QK_WIDTH), full),
                  pl.BlockSpec((tm, LANES), lambda b, i: (i, 0)),
                  pl.BlockSpec((tm, LANES), lambda b, i: (i, 0)),
                  pl.BlockSpec((MXU_DIM, MXU_DIM), full)],
        out_specs=[pl.BlockSpec((1, tm, LRU_WIDTH), tok),
                   pl.BlockSpec((1, tm, LRU_WIDTH), tok),
                   pl.BlockSpec((1, tm, QK_WIDTH), tok),
                   pl.BlockSpec((1, tm, QK_WIDTH), tok),
                   pl.BlockSpec((1, tm, ATT_WIDTH), tok)],
        out_shape=out_shapes,
        compiler_params=_cparams(("parallel", "parallel")),
        name="inproj_rope" if use_rope else "inproj_ctx",
    )(x, shift, scale, g1, w_bf, qg, kg, cos_t, sin_t, ones_bd)


LRU_HALF = LRU_WIDTH // 2
LRU_CHUNK = 512


def _conv_chunk(x_ref, t0, rows, total, cw, cb):
    x = x_ref[0, pl.ds(t0, rows), :]
    prev_start = pl.multiple_of(jnp.maximum(t0 - SUBLANES, 0), SUBLANES)
    next_start = pl.multiple_of(jnp.minimum(t0 + rows, total - SUBLANES), SUBLANES)
    prev = jnp.where(t0 > 0, x_ref[0, pl.ds(prev_start, SUBLANES), :], 0.0)
    nxt = jnp.where(t0 + rows < total, x_ref[0, pl.ds(next_start, SUBLANES), :], 0.0)
    xe = jnp.concatenate([prev, x, nxt], axis=0)
    acc = cb
    for k in range(CONV_W):
        off = SUBLANES - 1 + k
        acc = acc + xe[off:off + rows] * cw[k:k + 1]
    return acc


def _sigmoid(x):
    return 0.5 * jnp.tanh(0.5 * x) + 0.5


def _lru_gates(xc, w, bias, sp):
    pre = _dot(xc.astype(BF16), w) + bias
    r = _sigmoid(pre[:, :LRU_HALF])
    i = _sigmoid(pre[:, LRU_HALF:])
    log_a = (-LRU_C * r) * sp
    a = jnp.exp(log_a)
    mult = jnp.sqrt(-jnp.tanh(log_a) * (a * a + 1.0))
    return a, mult, i * xc


def _scan_chunk(a, u, h, reverse):
    rows, width = a.shape
    groups = rows // SUBLANES
    a = a.reshape(groups, SUBLANES, width)
    u = u.reshape(groups, SUBLANES, width)
    sub = lax.broadcasted_iota(jnp.int32, a.shape, 1)
    s = 1
    while s < SUBLANES:
        if reverse:
            m = sub < SUBLANES - s
            a_sh = pltpu.roll(a, SUBLANES - s, 1)
            u_sh = pltpu.roll(u, SUBLANES - s, 1)
        else:
            m = sub >= s
            a_sh = pltpu.roll(a, s, 1)
            u_sh = pltpu.roll(u, s, 1)
        u = jnp.where(m, a * u_sh + u, u)
        a = jnp.where(m, a * a_sh, a)
        s *= 2
    outs = [None] * groups
    order = range(groups - 1, -1, -1) if reverse else range(groups)
    for g in order:
        hg = a[g] * h + u[g]
        h = hg[0:1] if reverse else hg[SUBLANES - 1:SUBLANES]
        outs[g] = hg
    return jnp.concatenate(outs, axis=0), h


def _rglru_kernel(xl_ref, xc_ref, gate_ref, cw_ref, cb_ref, wf_ref, wb_ref, bf_ref, bb_ref, lam_ref,
                  o_ref, hf_ref, cl_ref, cc_ref, *, n, ctx_len):
    cw = cw_ref[...]
    cb = cb_ref[...]
    tc = min(LRU_CHUNK, ctx_len)
    tl = min(LRU_CHUNK, n)

    def direction(d, w_ref, b_ref):
        reverse = d == 1
        w = w_ref[0]
        bias = b_ref[0]
        z = -lam_ref[0, pl.ds(d, 1), :]
        sp = jnp.maximum(z, 0.0) + jnp.log1p(jnp.exp(-jnp.abs(z)))
        first_row = ctx_len - 1 if reverse else 0

        def conv(src_ref, cache_ref, t0, rows, total):
            if reverse:
                return cache_ref[pl.ds(t0, rows), :]
            xc = _conv_chunk(src_ref, t0, rows, total, cw, cb)
            cache_ref[pl.ds(t0, rows), :] = xc
            return xc

        def ctx_step(c, h):
            cc = (ctx_len // tc - 1 - c) if reverse else c
            t0 = pl.multiple_of(cc * tc, SUBLANES)
            xc = conv(xc_ref, cc_ref, t0, tc, ctx_len)
            a, mult, ix = _lru_gates(xc, w, bias, sp)
            row = lax.broadcasted_iota(jnp.int32, a.shape, 0) + t0
            mult = jnp.where(row == first_row, 1.0, mult)
            _, h = _scan_chunk(a, mult * ix, h, reverse)
            return h

        h = lax.fori_loop(0, ctx_len // tc, ctx_step, jnp.zeros((1, LRU_HALF), F32))

        def lat_step(c, h):
            cc = (n // tl - 1 - c) if reverse else c
            t0 = pl.multiple_of(cc * tl, SUBLANES)
            xc = conv(xl_ref, cl_ref, t0, tl, n)
            a, mult, ix = _lru_gates(xc, w, bias, sp)
            hs, h = _scan_chunk(a, mult * ix, h, reverse)
            if reverse:
                y = (hf_ref[pl.ds(t0, tl), :] + hs) * jax.nn.gelu(gate_ref[0, pl.ds(t0, tl), :])
                o_ref[0, pl.ds(t0, tl), :] = y.astype(BF16)
            else:
                hf_ref[pl.ds(t0, tl), :] = hs
            return h

        lax.fori_loop(0, n // tl, lat_step, h)

    direction(0, wf_ref, bf_ref)
    direction(1, wb_ref, bb_ref)


def _rglru(xl, xc, gate, conv_w, conv_b, w_f, w_b, b_f, b_b, lam):
    bsz, n, _ = xl.shape
    ctx_len = xc.shape[1]
    halves = LRU_WIDTH // LRU_HALF
    tokh = lambda b, hf: (b, 0, hf)
    return pl.pallas_call(
        functools.partial(_rglru_kernel, n=n, ctx_len=ctx_len),
        grid=(bsz, halves),
        in_specs=[pl.BlockSpec((1, n, LRU_HALF), tokh),
                  pl.BlockSpec((1, ctx_len, LRU_HALF), tokh),
                  pl.BlockSpec((1, n, LRU_HALF), tokh),
                  pl.BlockSpec((CONV_W, LRU_HALF), lambda b, hf: (0, hf)),
                  pl.BlockSpec((1, LRU_HALF), lambda b, hf: (0, hf)),
                  pl.BlockSpec((1, LRU_HALF, 2 * LRU_HALF), lambda b, hf: (hf, 0, 0)),
                  pl.BlockSpec((1, LRU_HALF, 2 * LRU_HALF), lambda b, hf: (hf, 0, 0)),
                  pl.BlockSpec((1, 1, 2 * LRU_HALF), lambda b, hf: (hf, 0, 0)),
                  pl.BlockSpec((1, 1, 2 * LRU_HALF), lambda b, hf: (hf, 0, 0)),
                  pl.BlockSpec((1, 2, LRU_HALF), lambda b, hf: (hf, 0, 0))],
        out_specs=pl.BlockSpec((1, n, LRU_HALF), tokh),
        out_shape=jax.ShapeDtypeStruct((bsz, n, LRU_WIDTH), BF16),
        scratch_shapes=[pltpu.VMEM((n, LRU_HALF), F32),
                        pltpu.VMEM((n, LRU_HALF), F32),
                        pltpu.VMEM((ctx_len, LRU_HALF), F32)],
        compiler_params=_cparams(("parallel", "parallel")),
        name="rglru",
    )(xl, xc, gate, conv_w, conv_b, w_f, w_b, b_f, b_b, lam)


def _lru_gate_weights(wa, ba, wi, bi):
    halves = LRU_WIDTH // LRU_HALF
    per = LRU_BLOCKS // halves
    bw = LRU_WIDTH // LRU_BLOCKS

    def dense(w):
        w = w.reshape(halves, per, bw, bw)
        eye = jnp.eye(per, dtype=w.dtype)
        return jnp.einsum("hpij,pq->hpiqj", w, eye).reshape(halves, per * bw, per * bw)

    w_cat = jnp.concatenate([dense(wa), dense(wi)], axis=-1).astype(BF16)
    b_cat = jnp.concatenate([ba.reshape(halves, 1, LRU_HALF), bi.reshape(halves, 1, LRU_HALF)], axis=-1)
    return w_cat, b_cat


ATT_TQ = 256
ATT_TK = 256


def _attn_kernel(q_ref, kc_ref, vc_ref, kl_ref, vl_ref, lq1_ref, lk1_ref, lq2_ref, lk2_ref, sg_ref, o_ref,
                 *, n, tk):
    q = q_ref[0]
    tq = q.shape[0]
    lane = lax.broadcasted_iota(jnp.int32, q.shape, 1)
    zero = jnp.zeros_like(q)
    q2 = jnp.concatenate([jnp.where(lane < HEAD_DIM, q, zero), jnp.where(lane >= HEAD_DIM, q, zero)], axis=0)

    def ext(v):
        ln = lax.broadcasted_iota(jnp.int32, v.shape, 1)
        return jnp.concatenate([v, jnp.where(ln == 0, 1.0, 0.0).astype(BF16)], axis=1)

    def step(k, v, carry):
        m, acc = carry
        s = _dot_nt(q2, k)
        m_new = jnp.maximum(m, jnp.max(s, axis=1, keepdims=True))
        p = jnp.exp2(s - m_new).astype(BF16)
        acc = jnp.exp2(m - m_new) * acc + _dot(p, ext(v))
        return m_new, acc

    carry = (jnp.full((2 * tq, 1), -1e30, F32), jnp.zeros((2 * tq, 2 * V_DIM), F32))
    carry = step(kc_ref[0], vc_ref[0], carry)
    for j in range(n // tk):
        carry = step(kl_ref[0, j * tk:(j + 1) * tk, :], vl_ref[0, j * tk:(j + 1) * tk, :], carry)
    _, acc = carry
    o = acc[:, :V_DIM] / acc[:, V_DIM:V_DIM + 1]
    lam = (jnp.exp(jnp.sum(lq1_ref[...] * lk1_ref[...], keepdims=True))
           - jnp.exp(jnp.sum(lq2_ref[...] * lk2_ref[...], keepdims=True)) + LAM_INIT)
    att = o[:tq] - lam * o[tq:]
    ms = jnp.mean(att * att, axis=-1, keepdims=True)
    y = att * lax.rsqrt(ms + EPS) * sg_ref[...]
    o_ref[0] = (y * (1.0 - LAM_INIT)).astype(BF16)


def _attention(q, kc, vc, kl, vl, lq1, lk1, lq2, lk2, sg):
    bsz, n, _ = q.shape
    ctx_len = kc.shape[1]
    tq = min(ATT_TQ, n)
    tk = min(ATT_TK, n)
    vec = lambda b, h, i: (0, 0)
    return pl.pallas_call(
        functools.partial(_attn_kernel, n=n, tk=tk),
        grid=(bsz, ATT_HEADS, n // tq),
        in_specs=[pl.BlockSpec((1, tq, V_DIM), lambda b, h, i: (b, i, h)),
                  pl.BlockSpec((1, ctx_len, V_DIM), lambda b, h, i: (b, 0, h)),
                  pl.BlockSpec((1, ctx_len, V_DIM), lambda b, h, i: (b, 0, h)),
                  pl.BlockSpec((1, n, V_DIM), lambda b, h, i: (b, 0, h)),
                  pl.BlockSpec((1, n, V_DIM), lambda b, h, i: (b, 0, h)),
                  pl.BlockSpec((1, HEAD_DIM), vec),
                  pl.BlockSpec((1, HEAD_DIM), vec),
                  pl.BlockSpec((1, HEAD_DIM), vec),
                  pl.BlockSpec((1, HEAD_DIM), vec),
                  pl.BlockSpec((1, V_DIM), vec)],
        out_specs=pl.BlockSpec((1, tq, V_DIM), lambda b, h, i: (b, i, h)),
        out_shape=jax.ShapeDtypeStruct((bsz, n, ATT_WIDTH), BF16),
        compiler_params=_cparams(("parallel", "parallel", "arbitrary")),
        name="diff_attention",
    )(q, kc, vc, kl, vl, lq1, lk1, lq2, lk2, sg)


def _outproj_kernel(lru_ref, att_ref, x_ref, g1_ref, shift_ref, scale_ref, n2_ref, wo_ref, wr_ref,
                    x1_ref, h2_ref, lg_ref):
    mix = _dot(lru_ref[0], wo_ref[:LRU_WIDTH, :]) + _dot(att_ref[0], wo_ref[LRU_WIDTH:, :])
    x1 = x_ref[0] + g1_ref[0] * mix
    x1_ref[0] = x1
    ms = jnp.mean(x1 * x1, axis=-1, keepdims=True)
    h2 = x1 * lax.rsqrt(ms + EPS) * n2_ref[...]
    h2 = h2 * (1.0 + scale_ref[0]) + shift_ref[0]
    h2_ref[0] = h2
    lg_ref[0] = _dot_nt(wr_ref[...], h2.astype(BF16))


def _outproj(lru, att, x, g1, shift2, scale2, n2g, wo_bf, wr_bf):
    bsz, n, d = x.shape
    tm = min(512, n)
    tok = lambda b, i: (b, i, 0)
    mod = lambda b, i: (b, 0, 0)
    full = lambda b, i: (0, 0)
    return pl.pallas_call(
        _outproj_kernel,
        grid=(bsz, n // tm),
        in_specs=[pl.BlockSpec((1, tm, LRU_WIDTH), tok),
                  pl.BlockSpec((1, tm, ATT_WIDTH), tok),
                  pl.BlockSpec((1, tm, d), tok),
                  pl.BlockSpec((1, 1, d), mod),
                  pl.BlockSpec((1, 1, d), mod),
                  pl.BlockSpec((1, 1, d), mod),
                  pl.BlockSpec((1, d), full),
                  pl.BlockSpec((d, d), full),
                  pl.BlockSpec((N_EXPERTS, d), full)],
        out_specs=[pl.BlockSpec((1, tm, d), tok),
                   pl.BlockSpec((1, tm, d), tok),
                   pl.BlockSpec((1, N_EXPERTS, tm), lambda b, i: (b, 0, i))],
        out_shape=(jax.ShapeDtypeStruct((bsz, n, d), F32),
                   jax.ShapeDtypeStruct((bsz, n, d), F32),
                   jax.ShapeDtypeStruct((bsz, N_EXPERTS, n), F32)),
        compiler_params=_cparams(("parallel", "parallel")),
        name="outproj_router",
    )(lru, att, x, g1, shift2, scale2, n2g, wo_bf, wr_bf)


TOPK_UNROLL = 4


def _topk_kernel(lg_ref, idx_ref, g_ref, aff_ref, rank_ref, start_ref, tot_ref, *, cap):
    lg = lg_ref[0]
    ne, nb, blk = lg.shape
    ex = jnp.exp(lg - jnp.max(lg, axis=0, keepdims=True))
    aff = ex / jnp.sum(ex, axis=0, keepdims=True)
    aff_ref[...] = aff
    capf = float(cap)

    def bisect(_, c):
        lo, hi = c
        mid = lo + ((hi - lo) >> 1)
        cnt = jnp.sum(jnp.where(aff >= pltpu.bitcast(mid, F32), 1.0, 0.0), axis=(1, 2), keepdims=True)
        ge = cnt >= capf
        return jnp.where(ge, mid, lo), jnp.where(ge, hi, mid)

    lo0 = jnp.zeros((ne, 1, 1), jnp.int32)
    hi0 = jnp.full((ne, 1, 1), 0x3F800001, jnp.int32)
    thr, nxt = lax.fori_loop(0, 31, bisect, (lo0, hi0))
    gt = aff >= pltpu.bitcast(nxt, F32)
    gtf = jnp.where(gt, 1.0, 0.0)
    eqf = jnp.where(jnp.logical_and(aff >= pltpu.bitcast(thr, F32), jnp.logical_not(gt)), 1.0, 0.0)
    need = capf - jnp.sum(gtf, axis=(1, 2), keepdims=True)

    rows = ne * nb
    tri = (lax.broadcasted_iota(jnp.int32, (blk, blk), 0)
           <= lax.broadcasted_iota(jnp.int32, (blk, blk), 1)).astype(BF16)
    r_i = lax.broadcasted_iota(jnp.int32, (rows, rows), 0)
    c_i = lax.broadcasted_iota(jnp.int32, (rows, rows), 1)
    earlier = jnp.logical_and(c_i < r_i, c_i >= (r_i // nb) * nb).astype(BF16)

    def prefix(x3):
        cin = _dot(x3.reshape(rows, blk).astype(BF16), tri)
        tot = jnp.broadcast_to(cin[:, blk - 1:blk], cin.shape)
        start = _dot(earlier, tot.astype(BF16))
        return cin.reshape(x3.shape), start.reshape(x3.shape), tot.reshape(x3.shape)

    cin, start, _ = prefix(eqf)
    sel = gtf + eqf * jnp.where(cin + start <= need, 1.0, 0.0)
    cin, start, tot = prefix(sel)
    rank_ref[...] = cin * sel
    start_ref[...] = start
    tot_ref[...] = tot

    slot = (lax.broadcasted_iota(jnp.int32, (1, cap), 1) + 1).astype(F32)
    row_id = lax.broadcasted_iota(jnp.int32, (nb, 1), 0).astype(F32)
    lane_id = lax.broadcasted_iota(jnp.int32, (blk, 1), 0).astype(F32)

    def per_expert(e):
        a = aff_ref[e]
        st = start_ref[e][:, 0:1]
        in_row = jnp.logical_and(st < slot, slot <= st + tot_ref[e][:, 0:1])
        in_row_bf = jnp.where(in_row, 1.0, 0.0).astype(BF16)
        row_of = jnp.sum(jnp.where(in_row, row_id, 0.0), axis=0, keepdims=True)
        rank_need = slot - jnp.sum(jnp.where(in_row, st, 0.0), axis=0, keepdims=True)

        def pick(x_bf):
            return lax.dot_general(x_bf, in_row_bf, (((0,), (0,)), ((), ())), preferred_element_type=F32)

        hit = pick(rank_ref[e].astype(BF16)) == rank_need
        lane_of = jnp.sum(jnp.where(hit, lane_id, 0.0), axis=0, keepdims=True)
        a_hi = a.astype(BF16)
        a_mid = (a - a_hi.astype(F32)).astype(BF16)
        a_lo = (a - a_hi.astype(F32) - a_mid.astype(F32)).astype(BF16)
        a_sel = (pick(a_hi) + pick(a_mid)) + pick(a_lo)
        idx_ref[0, e] = (row_of * float(blk) + lane_of).astype(jnp.int32)
        g_ref[0, e] = jnp.sum(jnp.where(hit, a_sel, 0.0), axis=0, keepdims=True)

    def expert_group(gi, carry):
        for r in range(TOPK_UNROLL):
            per_expert(gi * TOPK_UNROLL + r)
        return carry

    lax.fori_loop(0, ne // TOPK_UNROLL, expert_group, 0)


def _topk(logits_t, cap):
    bsz, _, n = logits_t.shape
    blk = min(LANES, n)
    nb = n // blk
    slab = pltpu.VMEM((N_EXPERTS, nb, blk), F32)
    return pl.pallas_call(
        functools.partial(_topk_kernel, cap=cap),
        grid=(bsz,),
        in_specs=[pl.BlockSpec((1, N_EXPERTS, nb, blk), lambda b: (b, 0, 0, 0))],
        out_specs=[pl.BlockSpec((1, N_EXPERTS, 1, cap), lambda b: (b, 0, 0, 0)),
                   pl.BlockSpec((1, N_EXPERTS, 1, cap), lambda b: (b, 0, 0, 0))],
        out_shape=(jax.ShapeDtypeStruct((bsz, N_EXPERTS, 1, cap), jnp.int32),
                   jax.ShapeDtypeStruct((bsz, N_EXPERTS, 1, cap), F32)),
        scratch_shapes=[slab, slab, slab, slab],
        compiler_params=_cparams(("parallel",)),
        name="expert_topk",
    )(logits_t.reshape(bsz, N_EXPERTS, nb, blk))


GATHER_UNROLL = 4


EXPERTS_PER_STEP = 4


def _gather_kernel(idx_ref, h_ref, o_ref, *, cap):
    unroll = GATHER_UNROLL if cap % (SUBLANES * GATHER_UNROLL) == 0 else 1
    for g in range(EXPERTS_PER_STEP):
        def body(jj, carry):
            for u in range(unroll):
                j0 = pl.multiple_of((jj * unroll + u) * SUBLANES, SUBLANES)
                rows = [h_ref[0, pl.ds(idx_ref[g, 0, j0 + r], 1), :] for r in range(SUBLANES)]
                o_ref[g, 0, pl.ds(j0, SUBLANES), :] = jnp.concatenate(rows, axis=0).astype(BF16)
            return carry

        lax.fori_loop(0, cap // (SUBLANES * unroll), body, 0)


def _gather(idx3, h2, cap):
    bsz, n, d = h2.shape
    eps = EXPERTS_PER_STEP
    groups = N_EXPERTS // eps
    return pl.pallas_call(
        functools.partial(_gather_kernel, cap=cap),
        grid=(bsz, groups),
        in_specs=[pl.BlockSpec((eps, 1, cap), lambda b, e: (b * groups + e, 0, 0), memory_space=pltpu.SMEM),
                  pl.BlockSpec((1, n, d), lambda b, e: (b, 0, 0))],
        out_specs=pl.BlockSpec((eps, 1, cap, d), lambda b, e: (e, b, 0, 0)),
        out_shape=jax.ShapeDtypeStruct((N_EXPERTS, bsz, cap, d), BF16),
        compiler_params=_cparams(("parallel", "arbitrary")),
        name="expert_gather",
    )(idx3, h2)


FFN_TF = 256


FFN_SUB = 512


def _ffn_kernel(x_ref, wg_ref, wu_ref, wd_ref, g_ref, gate_ref, o_ref, *, cap):
    i = pl.program_id(1)
    f = pl.program_id(2)
    nf = pl.num_programs(2)
    tm = x_ref.shape[1]
    sub = min(FFN_SUB, tm, cap)

    def body(first, last):
        wgu = jnp.concatenate([wg_ref[0].astype(BF16), wu_ref[0].astype(BF16)], axis=1)
        wd = wd_ref[0].astype(BF16)
        tf = wd.shape[0]
        for r in range(tm // sub):
            rows = slice(r * sub, (r + 1) * sub)
            gu = _dot(x_ref[0, rows, :], wgu)
            hg = gu[:, :tf]
            h = (hg * jax.nn.sigmoid(hg)) * gu[:, tf:]
            y = _dot(h.astype(BF16), wd)
            if not first:
                y = o_ref[0, rows, :] + y
            if last:
                sample = (i * tm + r * sub) // cap
                g_col = jnp.transpose(jnp.broadcast_to(g_ref[0, :, rows], (LANES, sub)))[:, 0:1]
                y = y * (g_col * gate_ref[sample])
            o_ref[0, rows, :] = y

    @pl.when(f == 0)
    def _():
        body(True, False)

    @pl.when(jnp.logical_and(f > 0, f < nf - 1))
    def _():
        body(False, False)

    @pl.when(f == nf - 1)
    def _():
        body(False, True)


def _ffn(xe, w_gate, w_up, w_down, g_row, gate2, cap):
    ne, m, d = xe.shape
    dff = w_gate.shape[2]
    tm = min(2048, m)
    tf = FFN_TF
    assert dff // tf > 1 and cap % min(FFN_SUB, tm, cap) == 0
    return pl.pallas_call(
        functools.partial(_ffn_kernel, cap=cap),
        grid=(ne, m // tm, dff // tf),
        in_specs=[pl.BlockSpec((1, tm, d), lambda e, i, f: (e, i, 0)),
                  pl.BlockSpec((1, d, tf), lambda e, i, f: (e, 0, f)),
                  pl.BlockSpec((1, d, tf), lambda e, i, f: (e, 0, f)),
                  pl.BlockSpec((1, tf, d), lambda e, i, f: (e, f, 0)),
                  pl.BlockSpec((1, 1, tm), lambda e, i, f: (e, 0, i)),
                  pl.BlockSpec(gate2.shape, lambda e, i, f: (0, 0, 0))],
        out_specs=pl.BlockSpec((1, tm, d), lambda e, i, f: (e, i, 0)),
        out_shape=jax.ShapeDtypeStruct((ne, m, d), F32),
        compiler_params=_cparams(("parallel", "parallel", "arbitrary")),
        name="expert_ffn",
    )(xe, w_gate, w_up, w_down, g_row, gate2)


def _combine_kernel(idx_ref, x1_hbm, y_ref, o_ref, sem, *, cap):
    b = pl.program_id(0)
    e = pl.program_id(1)

    @pl.when(e == 0)
    def _():
        init = pltpu.make_async_copy(x1_hbm.at[b], o_ref.at[0], sem)
        init.start()
        init.wait()

    sub = lax.broadcasted_iota(jnp.int32, (SUBLANES, o_ref.shape[2]), 0)
    unroll = COMBINE_UNROLL if cap % (SUBLANES * COMBINE_UNROLL) == 0 else 1

    for g in range(EXPERTS_PER_STEP):
        def body(jj, carry):
            for u in range(unroll):
                j0 = pl.multiple_of((jj * unroll + u) * SUBLANES, SUBLANES)
                ys = y_ref[g, 0, pl.ds(j0, SUBLANES), :]
                for r in range(0, SUBLANES, 2):
                    t0 = idx_ref[g, 0, j0 + r]
                    t1 = idx_ref[g, 0, j0 + r + 1]
                    b0 = pl.multiple_of((t0 >> 3) << 3, SUBLANES)
                    b1 = pl.multiple_of((t1 >> 3) << 3, SUBLANES)
                    tile0 = o_ref[0, pl.ds(b0, SUBLANES), :]
                    tile1 = o_ref[0, pl.ds(b1, SUBLANES), :]
                    u0 = jnp.where(sub == (t0 & (SUBLANES - 1)), ys[r:r + 1], 0.0)
                    u1 = jnp.where(sub == (t1 & (SUBLANES - 1)), ys[r + 1:r + 2], 0.0)
                    o_ref[0, pl.ds(b0, SUBLANES), :] = tile0 + u0
                    o_ref[0, pl.ds(b1, SUBLANES), :] = tile1 + u1 + jnp.where(b0 == b1, u0, 0.0)
            return carry

        lax.fori_loop(0, cap // (SUBLANES * unroll), body, 0)


COMBINE_UNROLL = 2


def _combine(idx3, x1, ye, cap):
    bsz, n, d = x1.shape
    eps = EXPERTS_PER_STEP
    groups = N_EXPERTS // eps
    resident = lambda b, e: (b, 0, 0)
    return pl.pallas_call(
        functools.partial(_combine_kernel, cap=cap),
        grid=(bsz, groups),
        in_specs=[pl.BlockSpec((eps, 1, cap), lambda b, e: (b * groups + e, 0, 0), memory_space=pltpu.SMEM),
                  pl.BlockSpec(memory_space=pl.ANY),
                  pl.BlockSpec((eps, 1, cap, d), lambda b, e: (e, b, 0, 0))],
        out_specs=pl.BlockSpec((1, n, d), resident, pipeline_mode=pl.Buffered(1)),
        out_shape=jax.ShapeDtypeStruct((bsz, n, d), F32),
        scratch_shapes=[pltpu.SemaphoreType.DMA],
        compiler_params=_cparams(("parallel", "arbitrary")),
        name="expert_combine",
    )(idx3, x1, ye)


def _rope_tables(n):
    rows = n // GRID_W
    row = jnp.repeat(jnp.arange(rows), GRID_W).astype(F32)
    col = jnp.tile(jnp.arange(GRID_W), rows).astype(F32)
    inv = ROPE_BASE ** (-jnp.arange(ROPE_PAIRS, dtype=F32) / ROPE_PAIRS)
    ang_r = row[:, None] * inv
    ang_c = col[:, None] * inv
    cos64 = jnp.concatenate([jnp.cos(ang_r), jnp.cos(ang_r), jnp.cos(ang_c), jnp.cos(ang_c)], axis=1)
    sin64 = jnp.concatenate([-jnp.sin(ang_r), jnp.sin(ang_r), -jnp.sin(ang_c), jnp.sin(ang_c)], axis=1)
    return jnp.tile(cos64, (1, 2)), jnp.tile(sin64, (1, 2))


def kernel(x, c, ctx, c_ctx, w_ada, b_ada, norm1_g, norm2_g, w_in, conv_w, conv_b, lru_wa, lru_ba, lru_wi,
           lru_bi, lru_lambda, q_norm_g, k_norm_g, lambda_q1, lambda_k1, lambda_q2, lambda_k2, subln_g, w_out,
           w_router, w_gate, w_up, w_down):
    assert w_ada.shape[0] == 1, "single-layer configuration"
    bsz, n, d = x.shape
    ctx_len = ctx.shape[1]
    cap = EC_FACTOR * n // N_EXPERTS

    rows = ((bsz + 1 + SUBLANES - 1) // SUBLANES) * SUBLANES
    cvec = jnp.zeros((rows, d), F32).at[:bsz].set(c).at[bsz].set(c_ctx)
    mod = _adaln(cvec, w_ada[0], b_ada[0][None]).reshape(rows, N_MOD, 1, d)
    mod_l = mod[:bsz]
    mod_c = mod[bsz:bsz + 1]

    w_in_bf = w_in[0].astype(BF16)
    w_out_bf = w_out[0].astype(BF16)
    w_r_bf = w_router[0].T.astype(BF16)
    qg = jnp.tile(q_norm_g[0], QK_WIDTH // HEAD_DIM)[None]
    kg = jnp.tile(k_norm_g[0], QK_WIDTH // HEAD_DIM)[None]
    seg = jnp.arange(MXU_DIM) // HEAD_DIM
    ones_bd = (seg[:, None] == seg[None, :]).astype(BF16)
    cos_t, sin_t = _rope_tables(n)
    cos_c = jnp.ones((ctx_len, LANES), F32)
    sin_c = jnp.zeros((ctx_len, LANES), F32)
    g1 = norm1_g[0][None]

    xl, gate_l, q_l, k_l, v_l = _inproj(x, mod_l[:, 0], mod_l[:, 1], g1, w_in_bf, qg, kg, cos_t, sin_t,
                                        ones_bd, True)
    xc, _, _, k_c, v_c = _inproj(ctx, mod_c[:, 0], mod_c[:, 1], g1, w_in_bf, qg, kg, cos_c, sin_c,
                                 ones_bd, False)

    w_f, b_f = _lru_gate_weights(lru_wa[0, 0], lru_ba[0, 0], lru_wi[0, 0], lru_bi[0, 0])
    w_b, b_b = _lru_gate_weights(lru_wa[0, 1], lru_ba[0, 1], lru_wi[0, 1], lru_bi[0, 1])
    halves = LRU_WIDTH // LRU_HALF
    lam_h = lru_lambda[0].reshape(2, halves, LRU_HALF).transpose(1, 0, 2)
    lru = _rglru(xl, xc, gate_l, conv_w[0], conv_b[0][None], w_f, w_b, b_f, b_b, lam_h)

    att = _attention(q_l, k_c, v_c, k_l, v_l, lambda_q1, lambda_k1, lambda_q2, lambda_k2, subln_g)

    x1, h2, logits = _outproj(lru, att, x, mod_l[:, 2], mod_l[:, 3], mod_l[:, 4], norm2_g[0][None],
                              w_out_bf, w_r_bf)

    idx, gates = _topk(logits, cap)
    idx3 = idx.reshape(bsz * N_EXPERTS, 1, cap)
    g_row = jnp.swapaxes(gates.reshape(bsz, N_EXPERTS, cap), 0, 1).reshape(N_EXPERTS, 1, bsz * cap)
    xe = _gather(idx3, h2, cap)
    ye = _ffn(xe.reshape(N_EXPERTS, bsz * cap, d), w_gate[0], w_up[0], w_down[0], g_row, mod_l[:, 5], cap)
    return _combine(idx3, x1, ye.reshape(N_EXPERTS, bsz, cap, d), cap)
```

```python
import functools

import jax
import jax.numpy as jnp
from jax import lax
from jax.experimental import pallas as pl
from jax.experimental.pallas import tpu as pltpu

F32 = jnp.float32
BF16 = jnp.bfloat16

EPS = 1e-6
GRID_W = 64
LRU_WIDTH = 512
LRU_BLOCKS = 8
LRU_C = 8.0
CONV_W = 4
ATT_HEADS = 4
HEAD_DIM = 64
V_DIM = 2 * HEAD_DIM
QK_WIDTH = ATT_HEADS * 2 * HEAD_DIM
ATT_WIDTH = ATT_HEADS * V_DIM
ROPE_PAIRS = HEAD_DIM // 4
ROPE_BASE = 10000.0
N_EXPERTS = 16
EC_FACTOR = 2
N_MOD = 6
LAM_INIT = 0.2
LOG2_E = 1.4426950408889634

LANES = 128
SUBLANES = 8
MXU_DIM = 256
VMEM_LIMIT = 56 * 1024 * 1024


def _cparams(sem):
    return pltpu.CompilerParams(dimension_semantics=sem, vmem_limit_bytes=VMEM_LIMIT)


def _dot(a, b):
    return jnp.dot(a, b, preferred_element_type=F32)


def _dot_nt(a, b):
    return lax.dot_general(a, b, (((1,), (1,)), ((), ())), preferred_element_type=F32)


def _split_bf16(x):
    hi = x.astype(BF16)
    lo = (x - hi.astype(F32)).astype(BF16)
    return hi, lo


def _adaln_kernel(c_ref, w_ref, b_ref, o_ref):
    c = c_ref[...]
    s = c * jax.nn.sigmoid(c)
    s_hi, s_lo = _split_bf16(s)
    w_hi, w_lo = _split_bf16(w_ref[...])
    o_ref[...] = _dot(s_hi, w_hi) + _dot(s_hi, w_lo) + _dot(s_lo, w_hi) + b_ref[...]


def _adaln(cvec, w, b):
    rows, d = cvec.shape
    cols = w.shape[1]
    tn = cols // 4
    return pl.pallas_call(
        _adaln_kernel,
        grid=(cols // tn,),
        in_specs=[pl.BlockSpec((rows, d), lambda j: (0, 0)),
                  pl.BlockSpec((d, tn), lambda j: (0, j)),
                  pl.BlockSpec((1, tn), lambda j: (0, j))],
        out_specs=pl.BlockSpec((rows, tn), lambda j: (0, j)),
        out_shape=jax.ShapeDtypeStruct((rows, cols), F32),
        compiler_params=_cparams(("arbitrary",)),
        name="adaln",
    )(cvec, w, b)


def _swap_halves16(x):
    lane = lax.broadcasted_iota(jnp.int32, x.shape, 1)
    first = (lane % 32) < 16
    return jnp.where(first, pltpu.roll(x, LANES - 16, 1), pltpu.roll(x, 16, 1))


def _qk_norm(t, g, ones_bd):
    outs = []
    for c in range(t.shape[1] // MXU_DIM):
        tc = t[:, c * MXU_DIM:(c + 1) * MXU_DIM]
        hi, lo = _split_bf16(tc * tc)
        ssum = _dot(hi, ones_bd) + _dot(lo, ones_bd)
        outs.append(tc * lax.rsqrt(ssum * (1.0 / HEAD_DIM) + EPS) * g[:, c * MXU_DIM:(c + 1) * MXU_DIM])
    return outs


def _inproj_kernel(x_ref, shift_ref, scale_ref, g1_ref, w_ref, qg_ref, kg_ref, cos_ref, sin_ref, ones_ref,
                   xl_ref, gate_ref, q_ref, k_ref, v_ref, *, use_rope):
    x = x_ref[0]
    ms = jnp.mean(x * x, axis=-1, keepdims=True)
    h = x * lax.rsqrt(ms + EPS) * g1_ref[...]
    h = h * (1.0 + scale_ref[0]) + shift_ref[0]
    p = _dot(h.astype(BF16), w_ref[...])
    o1, o2, o3, o4 = LRU_WIDTH, 2 * LRU_WIDTH, 2 * LRU_WIDTH + QK_WIDTH, 2 * LRU_WIDTH + 2 * QK_WIDTH
    xl_ref[0] = p[:, :o1]
    gate_ref[0] = p[:, o1:o2]
    v_ref[0] = p[:, o4:].astype(BF16)
    ones_bd = ones_ref[...]
    qn = _qk_norm(p[:, o2:o3], qg_ref[...], ones_bd)
    kn = _qk_norm(p[:, o3:o4], kg_ref[...], ones_bd)
    scale = HEAD_DIM ** -0.5 * LOG2_E
    for src, dst, mul in ((qn, q_ref, scale), (kn, k_ref, 1.0)):
        for c, tc in enumerate(src):
            for hh in range(MXU_DIM // LANES):
                th = tc[:, hh * LANES:(hh + 1) * LANES]
                if use_rope:
                    th = th * cos_ref[...] + _swap_halves16(th) * sin_ref[...]
                col = c * MXU_DIM + hh * LANES
                dst[0, :, col:col + LANES] = (th * mul).astype(BF16)


def _inproj(x, shift, scale, g1, w_bf, qg, kg, cos_t, sin_t, ones_bd, use_rope):
    bsz, n, d = x.shape
    tm = min(512, n)
    wid = w_bf.shape[1]
    per_b = shift.shape[0] > 1
    mod_map = (lambda b, i: (b, 0, 0)) if per_b else (lambda b, i: (0, 0, 0))
    full = lambda b, i: (0, 0)
    tok = lambda b, i: (b, i, 0)
    out_shapes = (jax.ShapeDtypeStruct((bsz, n, LRU_WIDTH), F32),
                  jax.ShapeDtypeStruct((bsz, n, LRU_WIDTH), F32),
                  jax.ShapeDtypeStruct((bsz, n, QK_WIDTH), BF16),
                  jax.ShapeDtypeStruct((bsz, n, QK_WIDTH), BF16),
                  jax.ShapeDtypeStruct((bsz, n, ATT_WIDTH), BF16))
    return pl.pallas_call(
        functools.partial(_inproj_kernel, use_rope=use_rope),
        grid=(bsz, n // tm),
        in_specs=[pl.BlockSpec((1, tm, d), tok),
                  pl.BlockSpec((1, 1, d), mod_map),
                  pl.BlockSpec((1, 1, d), mod_map),
                  pl.BlockSpec((1, d), full),
                  pl.BlockSpec((d, wid), full),
                  pl.BlockSpec((1, QK_WIDTH), full),
                  pl.BlockSpec((1, QK_WIDTH), full),
                  pl.BlockSpec((tm, LANES), lambda b, i: (i, 0)),
                  pl.BlockSpec((tm, LANES), lambda b, i: (i, 0)),
                  pl.BlockSpec((MXU_DIM, MXU_DIM), full)],
        out_specs=[pl.BlockSpec((1, tm, LRU_WIDTH), tok),
                   pl.BlockSpec((1, tm, LRU_WIDTH), tok),
                   pl.BlockSpec((1, tm, QK_WIDTH), tok),
                   pl.BlockSpec((1, tm, QK_WIDTH), tok),
                   pl.BlockSpec((1, tm, ATT_WIDTH), tok)],
        out_shape=out_shapes,
        compiler_params=_cparams(("parallel", "parallel")),
        name="inproj_rope" if use_rope else "inproj_ctx",
    )(x, shift, scale, g1, w_bf, qg, kg, cos_t, sin_t, ones_bd)


LRU_HALF = LRU_WIDTH // 2
LRU_CHUNK = 512


def _conv_chunk(x_ref, t0, rows, total, cw, cb):
    x = x_ref[0, pl.ds(t0, rows), :]
    prev_start = pl.multiple_of(jnp.maximum(t0 - SUBLANES, 0), SUBLANES)
    next_start = pl.multiple_of(jnp.minimum(t0 + rows, total - SUBLANES), SUBLANES)
    prev = jnp.where(t0 > 0, x_ref[0, pl.ds(prev_start, SUBLANES), :], 0.0)
    nxt = jnp.where(t0 + rows < total, x_ref[0, pl.ds(next_start, SUBLANES), :], 0.0)
    xe = jnp.concatenate([prev, x, nxt], axis=0)
    acc = cb
    for k in range(CONV_W):
        off = SUBLANES - 1 + k
        acc = acc + xe[off:off + rows] * cw[k:k + 1]
    return acc


def _sigmoid(x):
    return 0.5 * jnp.tanh(0.5 * x) + 0.5


def _lru_gates(xc, w, bias, sp):
    pre = _dot(xc.astype(BF16), w) + bias
    r = _sigmoid(pre[:, :LRU_HALF])
    i = _sigmoid(pre[:, LRU_HALF:])
    log_a = (-LRU_C * r) * sp
    a = jnp.exp(log_a)
    mult = jnp.sqrt(-jnp.tanh(log_a) * (a * a + 1.0))
    return a, mult, i * xc


def _scan_chunk(a, u, h, reverse):
    rows, width = a.shape
    groups = rows // SUBLANES
    a = a.reshape(groups, SUBLANES, width)
    u = u.reshape(groups, SUBLANES, width)
    sub = lax.broadcasted_iota(jnp.int32, a.shape, 1)
    s = 1
    while s < SUBLANES:
        if reverse:
            m = sub < SUBLANES - s
            a_sh = pltpu.roll(a, SUBLANES - s, 1)
            u_sh = pltpu.roll(u, SUBLANES - s, 1)
        else:
            m = sub >= s
            a_sh = pltpu.roll(a, s, 1)
            u_sh = pltpu.roll(u, s, 1)
        u = jnp.where(m, a * u_sh + u, u)
        a = jnp.where(m, a * a_sh, a)
        s *= 2
    outs = [None] * groups
    order = range(groups - 1, -1, -1) if reverse else range(groups)
    for g in order:
        hg = a[g] * h + u[g]
        h = hg[0:1] if reverse else hg[SUBLANES - 1:SUBLANES]
        outs[g] = hg
    return jnp.concatenate(outs, axis=0), h


def _rglru_kernel(xl_ref, xc_ref, gate_ref, cw_ref, cb_ref, wf_ref, wb_ref, bf_ref, bb_ref, lam_ref,
                  o_ref, hf_ref, cl_ref, cc_ref, *, n, ctx_len):
    cw = cw_ref[...]
    cb = cb_ref[...]
    tc = min(LRU_CHUNK, ctx_len)
    tl = min(LRU_CHUNK, n)

    def direction(d, w_ref, b_ref):
        reverse = d == 1
        w = w_ref[0]
        bias = b_ref[0]
        z = -lam_ref[0, pl.ds(d, 1), :]
        sp = jnp.maximum(z, 0.0) + jnp.log1p(jnp.exp(-jnp.abs(z)))
        first_row = ctx_len - 1 if reverse else 0

        def conv(src_ref, cache_ref, t0, rows, total):
            if reverse:
                return cache_ref[pl.ds(t0, rows), :]
            xc = _conv_chunk(src_ref, t0, rows, total, cw, cb)
            cache_ref[pl.ds(t0, rows), :] = xc
            return xc

        def ctx_step(c, h):
            cc = (ctx_len // tc - 1 - c) if reverse else c
            t0 = pl.multiple_of(cc * tc, SUBLANES)
            xc = conv(xc_ref, cc_ref, t0, tc, ctx_len)
            a, mult, ix = _lru_gates(xc, w, bias, sp)
            row = lax.broadcasted_iota(jnp.int32, a.shape, 0) + t0
            mult = jnp.where(row == first_row, 1.0, mult)
            _, h = _scan_chunk(a, mult * ix, h, reverse)
            return h

        h = lax.fori_loop(0, ctx_len // tc, ctx_step, jnp.zeros((1, LRU_HALF), F32))

        def lat_step(c, h):
            cc = (n // tl - 1 - c) if reverse else c
            t0 = pl.multiple_of(cc * tl, SUBLANES)
            xc = conv(xl_ref, cl_ref, t0, tl, n)
            a, mult, ix = _lru_gates(xc, w, bias, sp)
            hs, h = _scan_chunk(a, mult * ix, h, reverse)
            if reverse:
                y = (hf_ref[pl.ds(t0, tl), :] + hs) * jax.nn.gelu(gate_ref[0, pl.ds(t0, tl), :])
                o_ref[0, pl.ds(t0, tl), :] = y.astype(BF16)
            else:
                hf_ref[pl.ds(t0, tl), :] = hs
            return h

        lax.fori_loop(0, n // tl, lat_step, h)

    direction(0, wf_ref, bf_ref)
    direction(1, wb_ref, bb_ref)


def _rglru(xl, xc, gate, conv_w, conv_b, w_f, w_b, b_f, b_b, lam):
    bsz, n, _ = xl.shape
    ctx_len = xc.shape[1]
    halves = LRU_WIDTH // LRU_HALF
    tokh = lambda b, hf: (b, 0, hf)
    return pl.pallas_call(
        functools.partial(_rglru_kernel, n=n, ctx_len=ctx_len),
        grid=(bsz, halves),
        in_specs=[pl.BlockSpec((1, n, LRU_HALF), tokh),
                  pl.BlockSpec((1, ctx_len, LRU_HALF), tokh),
                  pl.BlockSpec((1, n, LRU_HALF), tokh),
                  pl.BlockSpec((CONV_W, LRU_HALF), lambda b, hf: (0, hf)),
                  pl.BlockSpec((1, LRU_HALF), lambda b, hf: (0, hf)),
                  pl.BlockSpec((1, LRU_HALF, 2 * LRU_HALF), lambda b, hf: (hf, 0, 0)),
                  pl.BlockSpec((1, LRU_HALF, 2 * LRU_HALF), lambda b, hf: (hf, 0, 0)),
                  pl.BlockSpec((1, 1, 2 * LRU_HALF), lambda b, hf: (hf, 0, 0)),
                  pl.BlockSpec((1, 1, 2 * LRU_HALF), lambda b, hf: (hf, 0, 0)),
                  pl.BlockSpec((1, 2, LRU_HALF), lambda b, hf: (hf, 0, 0))],
        out_specs=pl.BlockSpec((1, n, LRU_HALF), tokh),
        out_shape=jax.ShapeDtypeStruct((bsz, n, LRU_WIDTH), BF16),
        scratch_shapes=[pltpu.VMEM((n, LRU_HALF), F32),
                        pltpu.VMEM((n, LRU_HALF), F32),
                        pltpu.VMEM((ctx_len, LRU_HALF), F32)],
        compiler_params=_cparams(("parallel", "parallel")),
        name="rglru",
    )(xl, xc, gate, conv_w, conv_b, w_f, w_b, b_f, b_b, lam)


def _lru_gate_weights(wa, ba, wi, bi):
    halves = LRU_WIDTH // LRU_HALF
    per = LRU_BLOCKS // halves
    bw = LRU_WIDTH // LRU_BLOCKS

    def dense(w):
        w = w.reshape(halves, per, bw, bw)
        eye = jnp.eye(per, dtype=w.dtype)
        return jnp.einsum("hpij,pq->hpiqj", w, eye).reshape(halves, per * bw, per * bw)

    w_cat = jnp.concatenate([dense(wa), dense(wi)], axis=-1).astype(BF16)
    b_cat = jnp.concatenate([ba.reshape(halves, 1, LRU_HALF), bi.reshape(halves, 1, LRU_HALF)], axis=-1)
    return w_cat, b_cat


ATT_TQ = 256
ATT_TK = 256


def _attn_kernel(q_ref, kc_ref, vc_ref, kl_ref, vl_ref, lq1_ref, lk1_ref, lq2_ref, lk2_ref, sg_ref, o_ref,
                 *, n, tk):
    q = q_ref[0]
    tq = q.shape[0]
    lane = lax.broadcasted_iota(jnp.int32, q.shape, 1)
    zero = jnp.zeros_like(q)
    q2 = jnp.concatenate([jnp.where(lane < HEAD_DIM, q, zero), jnp.where(lane >= HEAD_DIM, q, zero)], axis=0)

    def ext(v):
        ln = lax.broadcasted_iota(jnp.int32, v.shape, 1)
        return jnp.concatenate([v, jnp.where(ln == 0, 1.0, 0.0).astype(BF16)], axis=1)

    def step(k, v, carry):
        m, acc = carry
        s = _dot_nt(q2, k)
        m_new = jnp.maximum(m, jnp.max(s, axis=1, keepdims=True))
        p = jnp.exp2(s - m_new).astype(BF16)
        acc = jnp.exp2(m - m_new) * acc + _dot(p, ext(v))
        return m_new, acc

    carry = (jnp.full((2 * tq, 1), -1e30, F32), jnp.zeros((2 * tq, 2 * V_DIM), F32))
    carry = step(kc_ref[0], vc_ref[0], carry)
    for j in range(n // tk):
        carry = step(kl_ref[0, j * tk:(j + 1) * tk, :], vl_ref[0, j * tk:(j + 1) * tk, :], carry)
    _, acc = carry
    o = acc[:, :V_DIM] / acc[:, V_DIM:V_DIM + 1]
    lam = (jnp.exp(jnp.sum(lq1_ref[...] * lk1_ref[...], keepdims=True))
           - jnp.exp(jnp.sum(lq2_ref[...] * lk2_ref[...], keepdims=True)) + LAM_INIT)
    att = o[:tq] - lam * o[tq:]
    ms = jnp.mean(att * att, axis=-1, keepdims=True)
    y = att * lax.rsqrt(ms + EPS) * sg_ref[...]
    o_ref[0] = (y * (1.0 - LAM_INIT)).astype(BF16)


def _attention(q, kc, vc, kl, vl, lq1, lk1, lq2, lk2, sg):
    bsz, n, _ = q.shape
    ctx_len = kc.shape[1]
    tq = min(ATT_TQ, n)
    tk = min(ATT_TK, n)
    vec = lambda b, h, i: (0, 0)
    return pl.pallas_call(
        functools.partial(_attn_kernel, n=n, tk=tk),
        grid=(bsz, ATT_HEADS, n // tq),
        in_specs=[pl.BlockSpec((1, tq, V_DIM), lambda b, h, i: (b, i, h)),
                  pl.BlockSpec((1, ctx_len, V_DIM), lambda b, h, i: (b, 0, h)),
                  pl.BlockSpec((1, ctx_len, V_DIM), lambda b, h, i: (b, 0, h)),
                  pl.BlockSpec((1, n, V_DIM), lambda b, h, i: (b, 0, h)),
                  pl.BlockSpec((1, n, V_DIM), lambda b, h, i: (b, 0, h)),
                  pl.BlockSpec((1, HEAD_DIM), vec),
                  pl.BlockSpec((1, HEAD_DIM), vec),
                  pl.BlockSpec((1, HEAD_DIM), vec),
                  pl.BlockSpec((1, HEAD_DIM), vec),
                  pl.BlockSpec((1, V_DIM), vec)],
        out_specs=pl.BlockSpec((1, tq, V_DIM), lambda b, h, i: (b, i, h)),
        out_shape=jax.ShapeDtypeStruct((bsz, n, ATT_WIDTH), BF16),
        compiler_params=_cparams(("parallel", "parallel", "arbitrary")),
        name="diff_attention",
    )(q, kc, vc, kl, vl, lq1, lk1, lq2, lk2, sg)


def _outproj_kernel(lru_ref, att_ref, x_ref, g1_ref, shift_ref, scale_ref, n2_ref, wo_ref, wr_ref,
                    x1_ref, h2_ref, lg_ref):
    mix = _dot(lru_ref[0], wo_ref[:LRU_WIDTH, :]) + _dot(att_ref[0], wo_ref[LRU_WIDTH:, :])
    x1 = x_ref[0] + g1_ref[0] * mix
    x1_ref[0] = x1
    ms = jnp.mean(x1 * x1, axis=-1, keepdims=True)
    h2 = x1 * lax.rsqrt(ms + EPS) * n2_ref[...]
    h2 = h2 * (1.0 + scale_ref[0]) + shift_ref[0]
    h2_ref[0] = h2
    lg_ref[0] = _dot_nt(wr_ref[...], h2.astype(BF16))


def _outproj(lru, att, x, g1, shift2, scale2, n2g, wo_bf, wr_bf):
    bsz, n, d = x.shape
    tm = min(512, n)
    tok = lambda b, i: (b, i, 0)
    mod = lambda b, i: (b, 0, 0)
    full = lambda b, i: (0, 0)
    return pl.pallas_call(
        _outproj_kernel,
        grid=(bsz, n // tm),
        in_specs=[pl.BlockSpec((1, tm, LRU_WIDTH), tok),
                  pl.BlockSpec((1, tm, ATT_WIDTH), tok),
                  pl.BlockSpec((1, tm, d), tok),
                  pl.BlockSpec((1, 1, d), mod),
                  pl.BlockSpec((1, 1, d), mod),
                  pl.BlockSpec((1, 1, d), mod),
                  pl.BlockSpec((1, d), full),
                  pl.BlockSpec((d, d), full),
                  pl.BlockSpec((N_EXPERTS, d), full)],
        out_specs=[pl.BlockSpec((1, tm, d), tok),
                   pl.BlockSpec((1, tm, d), tok),
                   pl.BlockSpec((1, N_EXPERTS, tm), lambda b, i: (b, 0, i))],
        out_shape=(jax.ShapeDtypeStruct((bsz, n, d), F32),
                   jax.ShapeDtypeStruct((bsz, n, d), F32),
                   jax.ShapeDtypeStruct((bsz, N_EXPERTS, n), F32)),
        compiler_params=_cparams(("parallel", "parallel")),
        name="outproj_router",
    )(lru, att, x, g1, shift2, scale2, n2g, wo_bf, wr_bf)


TOPK_UNROLL = 4


def _topk_kernel(lg_ref, idx_ref, g_ref, aff_ref, rank_ref, start_ref, tot_ref, *, cap):
    lg = lg_ref[0]
    ne, nb, blk = lg.shape
    ex = jnp.exp(lg - jnp.max(lg, axis=0, keepdims=True))
    aff = ex / jnp.sum(ex, axis=0, keepdims=True)
    aff_ref[...] = aff
    capf = float(cap)

    def bisect(_, c):
        lo, hi = c
        mid = lo + ((hi - lo) >> 1)
        cnt = jnp.sum(jnp.where(aff >= pltpu.bitcast(mid, F32), 1.0, 0.0), axis=(1, 2), keepdims=True)
        ge = cnt >= capf
        return jnp.where(ge, mid, lo), jnp.where(ge, hi, mid)

    lo0 = jnp.zeros((ne, 1, 1), jnp.int32)
    hi0 = jnp.full((ne, 1, 1), 0x3F800001, jnp.int32)
    thr, nxt = lax.fori_loop(0, 31, bisect, (lo0, hi0))
    gt = aff >= pltpu.bitcast(nxt, F32)
    gtf = jnp.where(gt, 1.0, 0.0)
    eqf = jnp.where(jnp.logical_and(aff >= pltpu.bitcast(thr, F32), jnp.logical_not(gt)), 1.0, 0.0)
    need = capf - jnp.sum(gtf, axis=(1, 2), keepdims=True)

    rows = ne * nb
    tri = (lax.broadcasted_iota(jnp.int32, (blk, blk), 0)
           <= lax.broadcasted_iota(jnp.int32, (blk, blk), 1)).astype(BF16)
    r_i = lax.broadcasted_iota(jnp.int32, (rows, rows), 0)
    c_i = lax.broadcasted_iota(jnp.int32, (rows, rows), 1)
    earlier = jnp.logical_and(c_i < r_i, c_i >= (r_i // nb) * nb).astype(BF16)

    def prefix(x3):
        cin = _dot(x3.reshape(rows, blk).astype(BF16), tri)
        tot = jnp.broadcast_to(cin[:, blk - 1:blk], cin.shape)
        start = _dot(earlier, tot.astype(BF16))
        return cin.reshape(x3.shape), start.reshape(x3.shape), tot.reshape(x3.shape)

    cin, start, _ = prefix(eqf)
    sel = gtf + eqf * jnp.where(cin + start <= need, 1.0, 0.0)
    cin, start, tot = prefix(sel)
    rank_ref[...] = cin * sel
    start_ref[...] = start
    tot_ref[...] = tot

    slot = (lax.broadcasted_iota(jnp.int32, (1, cap), 1) + 1).astype(F32)
    row_id = lax.broadcasted_iota(jnp.int32, (nb, 1), 0).astype(F32)
    lane_id = lax.broadcasted_iota(jnp.int32, (blk, 1), 0).astype(F32)

    def per_expert(e):
        a = aff_ref[e]
        st = start_ref[e][:, 0:1]
        in_row = jnp.logical_and(st < slot, slot <= st + tot_ref[e][:, 0:1])
        in_row_bf = jnp.where(in_row, 1.0, 0.0).astype(BF16)
        row_of = jnp.sum(jnp.where(in_row, row_id, 0.0), axis=0, keepdims=True)
        rank_need = slot - jnp.sum(jnp.where(in_row, st, 0.0), axis=0, keepdims=True)

        def pick(x_bf):
            return lax.dot_general(x_bf, in_row_bf, (((0,), (0,)), ((), ())), preferred_element_type=F32)

        hit = pick(rank_ref[e].astype(BF16)) == rank_need
        lane_of = jnp.sum(jnp.where(hit, lane_id, 0.0), axis=0, keepdims=True)
        a_hi = a.astype(BF16)
        a_mid = (a - a_hi.astype(F32)).astype(BF16)
        a_lo = (a - a_hi.astype(F32) - a_mid.astype(F32)).astype(BF16)
        a_sel = (pick(a_hi) + pick(a_mid)) + pick(a_lo)
        idx_ref[0, e] = (row_of * float(blk) + lane_of).astype(jnp.int32)
        g_ref[0, e] = jnp.sum(jnp.where(hit, a_sel, 0.0), axis=0, keepdims=True)

    def expert_group(gi, carry):
        for r in range(TOPK_UNROLL):
            per_expert(gi * TOPK_UNROLL + r)
        return carry

    lax.fori_loop(0, ne // TOPK_UNROLL, expert_group, 0)


def _topk(logits_t, cap):
    bsz, _, n = logits_t.shape
    blk = min(LANES, n)
    nb = n // blk
    slab = pltpu.VMEM((N_EXPERTS, nb, blk), F32)
    return pl.pallas_call(
        functools.partial(_topk_kernel, cap=cap),
        grid=(bsz,),
        in_specs=[pl.BlockSpec((1, N_EXPERTS, nb, blk), lambda b: (b, 0, 0, 0))],
        out_specs=[pl.BlockSpec((1, N_EXPERTS, 1, cap), lambda b: (b, 0, 0, 0)),
                   pl.BlockSpec((1, N_EXPERTS, 1, cap), lambda b: (b, 0, 0, 0))],
        out_shape=(jax.ShapeDtypeStruct((bsz, N_EXPERTS, 1, cap), jnp.int32),
                   jax.ShapeDtypeStruct((bsz, N_EXPERTS, 1, cap), F32)),
        scratch_shapes=[slab, slab, slab, slab],
        compiler_params=_cparams(("parallel",)),
        name="expert_topk",
    )(logits_t.reshape(bsz, N_EXPERTS, nb, blk))


GATHER_UNROLL = 4


EXPERTS_PER_STEP = 1


def _gather_kernel(idx_ref, h_ref, o_ref, *, cap):
    unroll = GATHER_UNROLL if cap % (SUBLANES * GATHER_UNROLL) == 0 else 1
    for g in range(EXPERTS_PER_STEP):
        def body(jj, carry):
            for u in range(unroll):
                j0 = pl.multiple_of((jj * unroll + u) * SUBLANES, SUBLANES)
                rows = [h_ref[0, pl.ds(idx_ref[g, 0, j0 + r], 1), :] for r in range(SUBLANES)]
                o_ref[g, 0, pl.ds(j0, SUBLANES), :] = jnp.concatenate(rows, axis=0).astype(BF16)
            return carry

        lax.fori_loop(0, cap // (SUBLANES * unroll), body, 0)


def _gather(idx3, h2, cap):
    bsz, n, d = h2.shape
    eps = EXPERTS_PER_STEP
    groups = N_EXPERTS // eps
    return pl.pallas_call(
        functools.partial(_gather_kernel, cap=cap),
        grid=(bsz, groups),
        in_specs=[pl.BlockSpec((eps, 1, cap), lambda b, e: (b * groups + e, 0, 0), memory_space=pltpu.SMEM),
                  pl.BlockSpec((1, n, d), lambda b, e: (b, 0, 0))],
        out_specs=pl.BlockSpec((eps, 1, cap, d), lambda b, e: (e, b, 0, 0)),
        out_shape=jax.ShapeDtypeStruct((N_EXPERTS, bsz, cap, d), BF16),
        compiler_params=_cparams(("parallel", "arbitrary")),
        name="expert_gather",
    )(idx3, h2)


FFN_TF = 256


FFN_SUB = 512


def _ffn_kernel(x_ref, wg_ref, wu_ref, wd_ref, g_ref, gate_ref, o_ref, *, cap):
    i = pl.program_id(1)
    f = pl.program_id(2)
    nf = pl.num_programs(2)
    tm = x_ref.shape[1]
    sub = min(FFN_SUB, tm, cap)

    def body(first, last):
        wgu = jnp.concatenate([wg_ref[0].astype(BF16), wu_ref[0].astype(BF16)], axis=1)
        wd = wd_ref[0].astype(BF16)
        tf = wd.shape[0]
        for r in range(tm // sub):
            rows = slice(r * sub, (r + 1) * sub)
            gu = _dot(x_ref[0, rows, :], wgu)
            hg = gu[:, :tf]
            h = (hg * jax.nn.sigmoid(hg)) * gu[:, tf:]
            y = _dot(h.astype(BF16), wd)
            if not first:
                y = o_ref[0, rows, :] + y
            if last:
                sample = (i * tm + r * sub) // cap
                g_col = jnp.transpose(jnp.broadcast_to(g_ref[0, :, rows], (LANES, sub)))[:, 0:1]
                y = y * (g_col * gate_ref[sample])
            o_ref[0, rows, :] = y

    @pl.when(f == 0)
    def _():
        body(True, False)

    @pl.when(jnp.logical_and(f > 0, f < nf - 1))
    def _():
        body(False, False)

    @pl.when(f == nf - 1)
    def _():
        body(False, True)


def _ffn(xe, w_gate, w_up, w_down, g_row, gate2, cap):
    ne, m, d = xe.shape
    dff = w_gate.shape[2]
    tm = min(2048, m)
    tf = FFN_TF
    assert dff // tf > 1 and cap % min(FFN_SUB, tm, cap) == 0
    return pl.pallas_call(
        functools.partial(_ffn_kernel, cap=cap),
        grid=(ne, m // tm, dff // tf),
        in_specs=[pl.BlockSpec((1, tm, d), lambda e, i, f: (e, i, 0)),
                  pl.BlockSpec((1, d, tf), lambda e, i, f: (e, 0, f)),
                  pl.BlockSpec((1, d, tf), lambda e, i, f: (e, 0, f)),
                  pl.BlockSpec((1, tf, d), lambda e, i, f: (e, f, 0)),
                  pl.BlockSpec((1, 1, tm), lambda e, i, f: (e, 0, i)),
                  pl.BlockSpec(gate2.shape, lambda e, i, f: (0, 0, 0))],
        out_specs=pl.BlockSpec((1, tm, d), lambda e, i, f: (e, i, 0)),
        out_shape=jax.ShapeDtypeStruct((ne, m, d), F32),
        compiler_params=_cparams(("parallel", "parallel", "arbitrary")),
        name="expert_ffn",
    )(xe, w_gate, w_up, w_down, g_row, gate2)


def _combine_kernel(idx_ref, x1_hbm, y_ref, o_ref, sem, *, cap):
    b = pl.program_id(0)
    e = pl.program_id(1)

    @pl.when(e == 0)
    def _():
        init = pltpu.make_async_copy(x1_hbm.at[b], o_ref.at[0], sem)
        init.start()
        init.wait()

    sub = lax.broadcasted_iota(jnp.int32, (SUBLANES, o_ref.shape[2]), 0)
    unroll = COMBINE_UNROLL if cap % (SUBLANES * COMBINE_UNROLL) == 0 else 1

    for g in range(EXPERTS_PER_STEP):
        def body(jj, carry):
            for u in range(unroll):
                j0 = pl.multiple_of((jj * unroll + u) * SUBLANES, SUBLANES)
                ys = y_ref[g, 0, pl.ds(j0, SUBLANES), :]
                for r in range(0, SUBLANES, 2):
                    t0 = idx_ref[g, 0, j0 + r]
                    t1 = idx_ref[g, 0, j0 + r + 1]
                    b0 = pl.multiple_of((t0 >> 3) << 3, SUBLANES)
                    b1 = pl.multiple_of((t1 >> 3) << 3, SUBLANES)
                    tile0 = o_ref[0, pl.ds(b0, SUBLANES), :]
                    tile1 = o_ref[0, pl.ds(b1, SUBLANES), :]
                    u0 = jnp.where(sub == (t0 & (SUBLANES - 1)), ys[r:r + 1], 0.0)
                    u1 = jnp.where(sub == (t1 & (SUBLANES - 1)), ys[r + 1:r + 2], 0.0)
                    o_ref[0, pl.ds(b0, SUBLANES), :] = tile0 + u0
                    o_ref[0, pl.ds(b1, SUBLANES), :] = tile1 + u1 + jnp.where(b0 == b1, u0, 0.0)
            return carry

        lax.fori_loop(0, cap // (SUBLANES * unroll), body, 0)


COMBINE_UNROLL = 2


def _combine(idx3, x1, ye, cap):
    bsz, n, d = x1.shape
    eps = EXPERTS_PER_STEP
    groups = N_EXPERTS // eps
    resident = lambda b, e: (b, 0, 0)
    return pl.pallas_call(
        functools.partial(_combine_kernel, cap=cap),
        grid=(bsz, groups),
        in_specs=[pl.BlockSpec((eps, 1, cap), lambda b, e: (b * groups + e, 0, 0), memory_space=pltpu.SMEM),
                  pl.BlockSpec(memory_space=pl.ANY),
                  pl.BlockSpec((eps, 1, cap, d), lambda b, e: (e, b, 0, 0))],
        out_specs=pl.BlockSpec((1, n, d), resident, pipeline_mode=pl.Buffered(1)),
        out_shape=jax.ShapeDtypeStruct((bsz, n, d), F32),
        scratch_shapes=[pltpu.SemaphoreType.DMA],
        compiler_params=_cparams(("parallel", "arbitrary")),
        name="expert_combine",
    )(idx3, x1, ye)


def _rope_tables(n):
    rows = n // GRID_W
    row = jnp.repeat(jnp.arange(rows), GRID_W).astype(F32)
    col = jnp.tile(jnp.arange(GRID_W), rows).astype(F32)
    inv = ROPE_BASE ** (-jnp.arange(ROPE_PAIRS, dtype=F32) / ROPE_PAIRS)
    ang_r = row[:, None] * inv
    ang_c = col[:, None] * inv
    cos64 = jnp.concatenate([jnp.cos(ang_r), jnp.cos(ang_r), jnp.cos(ang_c), jnp.cos(ang_c)], axis=1)
    sin64 = jnp.concatenate([-jnp.sin(ang_r), jnp.sin(ang_r), -jnp.sin(ang_c), jnp.sin(ang_c)], axis=1)
    return jnp.tile(cos64, (1, 2)), jnp.tile(sin64, (1, 2))


def kernel(x, c, ctx, c_ctx, w_ada, b_ada, norm1_g, norm2_g, w_in, conv_w, conv_b, lru_wa, lru_ba, lru_wi,
           lru_bi, lru_lambda, q_norm_g, k_norm_g, lambda_q1, lambda_k1, lambda_q2, lambda_k2, subln_g, w_out,
           w_router, w_gate, w_up, w_down):
    assert w_ada.shape[0] == 1, "single-layer configuration"
    bsz, n, d = x.shape
    ctx_len = ctx.shape[1]
    cap = EC_FACTOR * n // N_EXPERTS

    rows = ((bsz + 1 + SUBLANES - 1) // SUBLANES) * SUBLANES
    cvec = jnp.zeros((rows, d), F32).at[:bsz].set(c).at[bsz].set(c_ctx)
    mod = _adaln(cvec, w_ada[0], b_ada[0][None]).reshape(rows, N_MOD, 1, d)
    mod_l = mod[:bsz]
    mod_c = mod[bsz:bsz + 1]

    w_in_bf = w_in[0].astype(BF16)
    w_out_bf = w_out[0].astype(BF16)
    w_r_bf = w_router[0].T.astype(BF16)
    qg = jnp.tile(q_norm_g[0], QK_WIDTH // HEAD_DIM)[None]
    kg = jnp.tile(k_norm_g[0], QK_WIDTH // HEAD_DIM)[None]
    seg = jnp.arange(MXU_DIM) // HEAD_DIM
    ones_bd = (seg[:, None] == seg[None, :]).astype(BF16)
    cos_t, sin_t = _rope_tables(n)
    cos_c = jnp.ones((ctx_len, LANES), F32)
    sin_c = jnp.zeros((ctx_len, LANES), F32)
    g1 = norm1_g[0][None]

    xl, gate_l, q_l, k_l, v_l = _inproj(x, mod_l[:, 0], mod_l[:, 1], g1, w_in_bf, qg, kg, cos_t, sin_t,
                                        ones_bd, True)
    xc, _, _, k_c, v_c = _inproj(ctx, mod_c[:, 0], mod_c[:, 1], g1, w_in_bf, qg, kg, cos_c, sin_c,
                                 ones_bd, False)

    w_f, b_f = _lru_gate_weights(lru_wa[0, 0], lru_ba[0, 0], lru_wi[0, 0], lru_bi[0, 0])
    w_b, b_b = _lru_gate_weights(lru_wa[0, 1], lru_ba[0, 1], lru_wi[0, 1], lru_bi[0, 1])
    halves = LRU_WIDTH // LRU_HALF
    lam_h = lru_lambda[0].reshape(2, halves, LRU_HALF).transpose(1, 0, 2)
    lru = _rglru(xl, xc, gate_l, conv_w[0], conv_b[0][None], w_f, w_b, b_f, b_b, lam_h)

    att = _attention(q_l, k_c, v_c, k_l, v_l, lambda_q1, lambda_k1, lambda_q2, lambda_k2, subln_g)

    x1, h2, logits = _outproj(lru, att, x, mod_l[:, 2], mod_l[:, 3], mod_l[:, 4], norm2_g[0][None],
                              w_out_bf, w_r_bf)

    idx, gates = _topk(logits, cap)
    idx3 = idx.reshape(bsz * N_EXPERTS, 1, cap)
    g_row = jnp.swapaxes(gates.reshape(bsz, N_EXPERTS, cap), 0, 1).reshape(N_EXPERTS, 1, bsz * cap)
    xe = _gather(idx3, h2, cap)
    ye = _ffn(xe.reshape(N_EXPERTS, bsz * cap, d), w_gate[0], w_up[0], w_down[0], g_row, mod_l[:, 5], cap)
    return _combine(idx3, x1, ye.reshape(N_EXPERTS, bsz, cap, d), cap)
```

```python
import functools

import jax
import jax.numpy as jnp
from jax import lax
from jax.experimental import pallas as pl
from jax.experimental.pallas import tpu as pltpu

F32 = jnp.float32
BF16 = jnp.bfloat16

EPS = 1e-6
GRID_W = 64
LRU_WIDTH = 512
LRU_BLOCKS = 8
LRU_C = 8.0
CONV_W = 4
ATT_HEADS = 4
HEAD_DIM = 64
V_DIM = 2 * HEAD_DIM
QK_WIDTH = ATT_HEADS * 2 * HEAD_DIM
ATT_WIDTH = ATT_HEADS * V_DIM
ROPE_PAIRS = HEAD_DIM // 4
ROPE_BASE = 10000.0
N_EXPERTS = 16
EC_FACTOR = 2
N_MOD = 6
LAM_INIT = 0.2
LOG2_E = 1.4426950408889634

LANES = 128
SUBLANES = 8
MXU_DIM = 256
VMEM_LIMIT = 56 * 1024 * 1024


def _cparams(sem):
    return pltpu.CompilerParams(dimension_semantics=sem, vmem_limit_bytes=VMEM_LIMIT)


def _dot(a, b):
    return jnp.dot(a, b, preferred_element_type=F32)


def _dot_nt(a, b):
    return lax.dot_general(a, b, (((1,), (1,)), ((), ())), preferred_element_type=F32)


def _split_bf16(x):
    hi = x.astype(BF16)
    lo = (x - hi.astype(F32)).astype(BF16)
    return hi, lo


def _adaln_kernel(c_ref, w_ref, b_ref, o_ref):
    c = c_ref[...]
    s = c * jax.nn.sigmoid(c)
    s_hi, s_lo = _split_bf16(s)
    w_hi, w_lo = _split_bf16(w_ref[...])
    o_ref[...] = _dot(s_hi, w_hi) + _dot(s_hi, w_lo) + _dot(s_lo, w_hi) + b_ref[...]


def _adaln(cvec, w, b):
    rows, d = cvec.shape
    cols = w.shape[1]
    tn = cols // 4
    return pl.pallas_call(
        _adaln_kernel,
        grid=(cols // tn,),
        in_specs=[pl.BlockSpec((rows, d), lambda j: (0, 0)),
                  pl.BlockSpec((d, tn), lambda j: (0, j)),
                  pl.BlockSpec((1, tn), lambda j: (0, j))],
        out_specs=pl.BlockSpec((rows, tn), lambda j: (0, j)),
        out_shape=jax.ShapeDtypeStruct((rows, cols), F32),
        compiler_params=_cparams(("arbitrary",)),
        name="adaln",
    )(cvec, w, b)


def _swap_halves16(x):
    lane = lax.broadcasted_iota(jnp.int32, x.shape, 1)
    first = (lane % 32) < 16
    return jnp.where(first, pltpu.roll(x, LANES - 16, 1), pltpu.roll(x, 16, 1))


def _qk_norm(t, g, ones_bd):
    outs = []
    for c in range(t.shape[1] // MXU_DIM):
        tc = t[:, c * MXU_DIM:(c + 1) * MXU_DIM]
        hi, lo = _split_bf16(tc * tc)
        ssum = _dot(hi, ones_bd) + _dot(lo, ones_bd)
        outs.append(tc * lax.rsqrt(ssum * (1.0 / HEAD_DIM) + EPS) * g[:, c * MXU_DIM:(c + 1) * MXU_DIM])
    return outs


def _inproj_kernel(x_ref, shift_ref, scale_ref, g1_ref, w_ref, qg_ref, kg_ref, cos_ref, sin_ref, ones_ref,
                   xl_ref, gate_ref, q_ref, k_ref, v_ref, *, use_rope):
    x = x_ref[0]
    ms = jnp.mean(x * x, axis=-1, keepdims=True)
    h = x * lax.rsqrt(ms + EPS) * g1_ref[...]
    h = h * (1.0 + scale_ref[0]) + shift_ref[0]
    p = _dot(h.astype(BF16), w_ref[...])
    o1, o2, o3, o4 = LRU_WIDTH, 2 * LRU_WIDTH, 2 * LRU_WIDTH + QK_WIDTH, 2 * LRU_WIDTH + 2 * QK_WIDTH
    xl_ref[0] = p[:, :o1]
    gate_ref[0] = p[:, o1:o2]
    v_ref[0] = p[:, o4:].astype(BF16)
    ones_bd = ones_ref[...]
    qn = _qk_norm(p[:, o2:o3], qg_ref[...], ones_bd)
    kn = _qk_norm(p[:, o3:o4], kg_ref[...], ones_bd)
    scale = HEAD_DIM ** -0.5 * LOG2_E
    for src, dst, mul in ((qn, q_ref, scale), (kn, k_ref, 1.0)):
        for c, tc in enumerate(src):
            for hh in range(MXU_DIM // LANES):
                th = tc[:, hh * LANES:(hh + 1) * LANES]
                if use_rope:
                    th = th * cos_ref[...] + _swap_halves16(th) * sin_ref[...]
                col = c * MXU_DIM + hh * LANES
                dst[0, :, col:col + LANES] = (th * mul).astype(BF16)


def _inproj(x, shift, scale, g1, w_bf, qg, kg, cos_t, sin_t, ones_bd, use_rope):
    bsz, n, d = x.shape
    tm = min(512, n)
    wid = w_bf.shape[1]
    per_b = shift.shape[0] > 1
    mod_map = (lambda b, i: (b, 0, 0)) if per_b else (lambda b, i: (0, 0, 0))
    full = lambda b, i: (0, 0)
    tok = lambda b, i: (b, i, 0)
    out_shapes = (jax.ShapeDtypeStruct((bsz, n, LRU_WIDTH), F32),
                  jax.ShapeDtypeStruct((bsz, n, LRU_WIDTH), F32),
                  jax.ShapeDtypeStruct((bsz, n, QK_WIDTH), BF16),
                  jax.ShapeDtypeStruct((bsz, n, QK_WIDTH), BF16),
                  jax.ShapeDtypeStruct((bsz, n, ATT_WIDTH), BF16))
    return pl.pallas_call(
        functools.partial(_inproj_kernel, use_rope=use_rope),
        grid=(bsz, n // tm),
        in_specs=[pl.BlockSpec((1, tm, d), tok),
                  pl.BlockSpec((1, 1, d), mod_map),
                  pl.BlockSpec((1, 1, d), mod_map),
                  pl.BlockSpec((1, d), full),
                  pl.BlockSpec((d, wid), full),
                  pl.BlockSpec((1, QK_WIDTH), full),
                  pl.BlockSpec((1, QK_WIDTH), full),
                  pl.BlockSpec((tm, LANES), lambda b, i: (i, 0)),
                  pl.BlockSpec((tm, LANES), lambda b, i: (i, 0)),
                  pl.BlockSpec((MXU_DIM, MXU_DIM), full)],
        out_specs=[pl.BlockSpec((1, tm, LRU_WIDTH), tok),
                   pl.BlockSpec((1, tm, LRU_WIDTH), tok),
                   pl.BlockSpec((1, tm, QK_WIDTH), tok),
                   pl.BlockSpec((1, tm, QK_WIDTH), tok),
                   pl.BlockSpec((1, tm, ATT_WIDTH), tok)],
        out_shape=out_shapes,
        compiler_params=_cparams(("parallel", "parallel")),
        name="inproj_rope" if use_rope else "inproj_ctx",
    )(x, shift, scale, g1, w_bf, qg, kg, cos_t, sin_t, ones_bd)


LRU_HALF = LRU_WIDTH // 2
LRU_CHUNK = 512


def _to_strands(x):
    rows, width = x.shape
    return jnp.swapaxes(x.reshape(rows // SUBLANES, SUBLANES, width), 0, 1).reshape(rows, width)


def _from_strands(x):
    rows, width = x.shape
    return jnp.swapaxes(x.reshape(SUBLANES, rows // SUBLANES, width), 0, 1).reshape(rows, width)


def _planes(x):
    g = x.shape[0] // SUBLANES
    return [x[s * g:(s + 1) * g] for s in range(SUBLANES)]


def _shift_down(p, first):
    row = lax.broadcasted_iota(jnp.int32, p.shape, 0)
    return jnp.where(row == 0, first, pltpu.roll(p, 1, 0))


def _shift_up(p, last):
    row = lax.broadcasted_iota(jnp.int32, p.shape, 0)
    return jnp.where(row == p.shape[0] - 1, last, pltpu.roll(p, p.shape[0] - 1, 0))


def _conv_strands(x_ref, t0, rows, total, cw, cb):
    p = _planes(_to_strands(x_ref[0, pl.ds(t0, rows), :]))
    prev_start = pl.multiple_of(jnp.maximum(t0 - SUBLANES, 0), SUBLANES)
    next_start = pl.multiple_of(jnp.minimum(t0 + rows, total - SUBLANES), SUBLANES)
    prev = jnp.where(t0 > 0, x_ref[0, pl.ds(prev_start, SUBLANES), :], 0.0)
    nxt = jnp.where(t0 + rows < total, x_ref[0, pl.ds(next_start, SUBLANES), :], 0.0)
    before = _shift_down(p[SUBLANES - 1], prev[SUBLANES - 1:SUBLANES])
    after0 = _shift_up(p[0], nxt[0:1])
    after1 = _shift_up(p[1], nxt[1:2])
    ext = [before] + p + [after0, after1]
    out = []
    for s in range(SUBLANES):
        acc = cb
        for k in range(CONV_W):
            acc = acc + ext[s + k] * cw[k:k + 1]
        out.append(acc)
    return jnp.concatenate(out, axis=0)


def _sigmoid(x):
    return 0.5 * jnp.tanh(0.5 * x) + 0.5


def _lru_gates(xc, w, bias, sp):
    pre = _dot(xc.astype(BF16), w) + bias
    r = _sigmoid(pre[:, :LRU_HALF])
    i = _sigmoid(pre[:, LRU_HALF:])
    log_a = (-LRU_C * r) * sp
    a = jnp.exp(log_a)
    m = -jnp.tanh(log_a) * (a * a + 1.0)
    mult = jnp.where(m > 0.0, m * lax.rsqrt(m), 0.0)
    return a, mult, i * xc


def _scan_chunk(a, u, h, reverse):
    rows, width = a.shape
    groups = rows // SUBLANES
    a = a.reshape(groups, SUBLANES, width)
    u = u.reshape(groups, SUBLANES, width)
    sub = lax.broadcasted_iota(jnp.int32, a.shape, 1)
    s = 1
    while s < SUBLANES:
        if reverse:
            m = sub < SUBLANES - s
            a_sh = pltpu.roll(a, SUBLANES - s, 1)
            u_sh = pltpu.roll(u, SUBLANES - s, 1)
        else:
            m = sub >= s
            a_sh = pltpu.roll(a, s, 1)
            u_sh = pltpu.roll(u, s, 1)
        u = jnp.where(m, a * u_sh + u, u)
        a = jnp.where(m, a * a_sh, a)
        s *= 2
    outs = [None] * groups
    order = range(groups - 1, -1, -1) if reverse else range(groups)
    for g in order:
        hg = a[g] * h + u[g]
        h = hg[0:1] if reverse else hg[SUBLANES - 1:SUBLANES]
        outs[g] = hg
    return jnp.concatenate(outs, axis=0), h


def _scan_strands(a, u, h, reverse):
    a, u = _planes(a), _planes(u)
    order = list(range(SUBLANES - 1, -1, -1)) if reverse else list(range(SUBLANES))
    for prev_s, s in zip(order[:-1], order[1:]):
        u[s] = a[s] * u[prev_s] + u[s]
        a[s] = a[s] * a[prev_s]
    last = order[-1]
    ends, h_out = _scan_chunk(a[last], u[last], h, reverse)
    carry_in = _shift_up(ends, h) if reverse else _shift_down(ends, h)
    return jnp.concatenate([a[s] * carry_in + u[s] for s in range(SUBLANES)], axis=0), h_out


def _rglru_kernel(xl_ref, xc_ref, gate_ref, cw_ref, cb_ref, wf_ref, wb_ref, bf_ref, bb_ref, lam_ref,
                  o_ref, hf_ref, cl_ref, cc_ref, *, n, ctx_len):
    cw = cw_ref[...]
    cb = cb_ref[...]
    tc = min(LRU_CHUNK, ctx_len)
    tl = min(LRU_CHUNK, n)

    def direction(d, w_ref, b_ref):
        reverse = d == 1
        w = w_ref[0]
        bias = b_ref[0]
        z = -lam_ref[0, pl.ds(d, 1), :]
        sp = jnp.maximum(z, 0.0) + jnp.log1p(jnp.exp(-jnp.abs(z)))
        first_row = ctx_len - 1 if reverse else 0

        def conv(src_ref, cache_ref, t0, rows, total):
            if reverse:
                return cache_ref[pl.ds(t0, rows), :]
            xc = _conv_strands(src_ref, t0, rows, total, cw, cb)
            cache_ref[pl.ds(t0, rows), :] = xc
            return xc

        def ctx_step(c, h):
            cc = (ctx_len // tc - 1 - c) if reverse else c
            t0 = pl.multiple_of(cc * tc, SUBLANES)
            xc = conv(xc_ref, cc_ref, t0, tc, ctx_len)
            a, mult, ix = _lru_gates(xc, w, bias, sp)
            r = lax.broadcasted_iota(jnp.int32, a.shape, 0)
            groups = tc // SUBLANES
            time = (r % groups) * SUBLANES + r // groups + t0
            mult = jnp.where(time == first_row, 1.0, mult)
            _, h = _scan_strands(a, mult * ix, h, reverse)
            return h

        h = lax.fori_loop(0, ctx_len // tc, ctx_step, jnp.zeros((1, LRU_HALF), F32))

        def lat_step(c, h):
            cc = (n // tl - 1 - c) if reverse else c
            t0 = pl.multiple_of(cc * tl, SUBLANES)
            xc = conv(xl_ref, cl_ref, t0, tl, n)
            a, mult, ix = _lru_gates(xc, w, bias, sp)
            hs, h = _scan_strands(a, mult * ix, h, reverse)
            if reverse:
                y = _from_strands(hf_ref[pl.ds(t0, tl), :] + hs) * jax.nn.gelu(gate_ref[0, pl.ds(t0, tl), :])
                o_ref[0, pl.ds(t0, tl), :] = y.astype(BF16)
            else:
                hf_ref[pl.ds(t0, tl), :] = hs
            return h

        lax.fori_loop(0, n // tl, lat_step, h)

    direction(0, wf_ref, bf_ref)
    direction(1, wb_ref, bb_ref)


def _rglru(xl, xc, gate, conv_w, conv_b, w_f, w_b, b_f, b_b, lam):
    bsz, n, _ = xl.shape
    ctx_len = xc.shape[1]
    halves = LRU_WIDTH // LRU_HALF
    tokh = lambda b, hf: (b, 0, hf)
    return pl.pallas_call(
        functools.partial(_rglru_kernel, n=n, ctx_len=ctx_len),
        grid=(bsz, halves),
        in_specs=[pl.BlockSpec((1, n, LRU_HALF), tokh),
                  pl.BlockSpec((1, ctx_len, LRU_HALF), tokh),
                  pl.BlockSpec((1, n, LRU_HALF), tokh),
                  pl.BlockSpec((CONV_W, LRU_HALF), lambda b, hf: (0, hf)),
                  pl.BlockSpec((1, LRU_HALF), lambda b, hf: (0, hf)),
                  pl.BlockSpec((1, LRU_HALF, 2 * LRU_HALF), lambda b, hf: (hf, 0, 0)),
                  pl.BlockSpec((1, LRU_HALF, 2 * LRU_HALF), lambda b, hf: (hf, 0, 0)),
                  pl.BlockSpec((1, 1, 2 * LRU_HALF), lambda b, hf: (hf, 0, 0)),
                  pl.BlockSpec((1, 1, 2 * LRU_HALF), lambda b, hf: (hf, 0, 0)),
                  pl.BlockSpec((1, 2, LRU_HALF), lambda b, hf: (hf, 0, 0))],
        out_specs=pl.BlockSpec((1, n, LRU_HALF), tokh),
        out_shape=jax.ShapeDtypeStruct((bsz, n, LRU_WIDTH), BF16),
        scratch_shapes=[pltpu.VMEM((n, LRU_HALF), F32),
                        pltpu.VMEM((n, LRU_HALF), F32),
                        pltpu.VMEM((ctx_len, LRU_HALF), F32)],
        compiler_params=_cparams(("parallel", "parallel")),
        name="rglru",
    )(xl, xc, gate, conv_w, conv_b, w_f, w_b, b_f, b_b, lam)


def _lru_gate_weights(wa, ba, wi, bi):
    halves = LRU_WIDTH // LRU_HALF
    per = LRU_BLOCKS // halves
    bw = LRU_WIDTH // LRU_BLOCKS

    def dense(w):
        w = w.reshape(halves, per, bw, bw)
        eye = jnp.eye(per, dtype=w.dtype)
        return jnp.einsum("hpij,pq->hpiqj", w, eye).reshape(halves, per * bw, per * bw)

    w_cat = jnp.concatenate([dense(wa), dense(wi)], axis=-1).astype(BF16)
    b_cat = jnp.concatenate([ba.reshape(halves, 1, LRU_HALF), bi.reshape(halves, 1, LRU_HALF)], axis=-1)
    return w_cat, b_cat


ATT_TQ = 256
ATT_TK = 256


def _attn_kernel(q_ref, kc_ref, vc_ref, kl_ref, vl_ref, lq1_ref, lk1_ref, lq2_ref, lk2_ref, sg_ref, o_ref,
                 *, n, tk):
    q = q_ref[0]
    tq = q.shape[0]
    lane = lax.broadcasted_iota(jnp.int32, q.shape, 1)
    zero = jnp.zeros_like(q)
    q2 = jnp.concatenate([jnp.where(lane < HEAD_DIM, q, zero), jnp.where(lane >= HEAD_DIM, q, zero)], axis=0)

    def ext(v):
        ln = lax.broadcasted_iota(jnp.int32, v.shape, 1)
        return jnp.concatenate([v, jnp.where(ln == 0, 1.0, 0.0).astype(BF16)], axis=1)

    def step(k, v, carry):
        m, acc = carry
        s = _dot_nt(q2, k)
        m_new = jnp.maximum(m, jnp.max(s, axis=1, keepdims=True))
        p = jnp.exp2(s - m_new).astype(BF16)
        acc = jnp.exp2(m - m_new) * acc + _dot(p, ext(v))
        return m_new, acc

    carry = (jnp.full((2 * tq, 1), -1e30, F32), jnp.zeros((2 * tq, 2 * V_DIM), F32))
    carry = step(kc_ref[0], vc_ref[0], carry)
    for j in range(n // tk):
        carry = step(kl_ref[0, j * tk:(j + 1) * tk, :], vl_ref[0, j * tk:(j + 1) * tk, :], carry)
    _, acc = carry
    o = acc[:, :V_DIM] / acc[:, V_DIM:V_DIM + 1]
    lam = (jnp.exp(jnp.sum(lq1_ref[...] * lk1_ref[...], keepdims=True))
           - jnp.exp(jnp.sum(lq2_ref[...] * lk2_ref[...], keepdims=True)) + LAM_INIT)
    att = o[:tq] - lam * o[tq:]
    ms = jnp.mean(att * att, axis=-1, keepdims=True)
    y = att * lax.rsqrt(ms + EPS) * sg_ref[...]
    o_ref[0] = (y * (1.0 - LAM_INIT)).astype(BF16)


def _attention(q, kc, vc, kl, vl, lq1, lk1, lq2, lk2, sg):
    bsz, n, _ = q.shape
    ctx_len = kc.shape[1]
    tq = min(ATT_TQ, n)
    tk = min(ATT_TK, n)
    vec = lambda b, h, i: (0, 0)
    return pl.pallas_call(
        functools.partial(_attn_kernel, n=n, tk=tk),
        grid=(bsz, ATT_HEADS, n // tq),
        in_specs=[pl.BlockSpec((1, tq, V_DIM), lambda b, h, i: (b, i, h)),
                  pl.BlockSpec((1, ctx_len, V_DIM), lambda b, h, i: (b, 0, h)),
                  pl.BlockSpec((1, ctx_len, V_DIM), lambda b, h, i: (b, 0, h)),
                  pl.BlockSpec((1, n, V_DIM), lambda b, h, i: (b, 0, h)),
                  pl.BlockSpec((1, n, V_DIM), lambda b, h, i: (b, 0, h)),
                  pl.BlockSpec((1, HEAD_DIM), vec),
                  pl.BlockSpec((1, HEAD_DIM), vec),
                  pl.BlockSpec((1, HEAD_DIM), vec),
                  pl.BlockSpec((1, HEAD_DIM), vec),
                  pl.BlockSpec((1, V_DIM), vec)],
        out_specs=pl.BlockSpec((1, tq, V_DIM), lambda b, h, i: (b, i, h)),
        out_shape=jax.ShapeDtypeStruct((bsz, n, ATT_WIDTH), BF16),
        compiler_params=_cparams(("parallel", "parallel", "arbitrary")),
        name="diff_attention",
    )(q, kc, vc, kl, vl, lq1, lk1, lq2, lk2, sg)


def _outproj_kernel(lru_ref, att_ref, x_ref, g1_ref, shift_ref, scale_ref, n2_ref, wo_ref, wr_ref,
                    x1_ref, h2_ref, lg_ref):
    mix = _dot(lru_ref[0], wo_ref[:LRU_WIDTH, :]) + _dot(att_ref[0], wo_ref[LRU_WIDTH:, :])
    x1 = x_ref[0] + g1_ref[0] * mix
    x1_ref[0] = x1
    ms = jnp.mean(x1 * x1, axis=-1, keepdims=True)
    h2 = x1 * lax.rsqrt(ms + EPS) * n2_ref[...]
    h2 = h2 * (1.0 + scale_ref[0]) + shift_ref[0]
    h2_ref[0] = h2
    lg_ref[0] = _dot_nt(wr_ref[...], h2.astype(BF16))


def _outproj(lru, att, x, g1, shift2, scale2, n2g, wo_bf, wr_bf):
    bsz, n, d = x.shape
    tm = min(512, n)
    tok = lambda b, i: (b, i, 0)
    mod = lambda b, i: (b, 0, 0)
    full = lambda b, i: (0, 0)
    return pl.pallas_call(
        _outproj_kernel,
        grid=(bsz, n // tm),
        in_specs=[pl.BlockSpec((1, tm, LRU_WIDTH), tok),
                  pl.BlockSpec((1, tm, ATT_WIDTH), tok),
                  pl.BlockSpec((1, tm, d), tok),
                  pl.BlockSpec((1, 1, d), mod),
                  pl.BlockSpec((1, 1, d), mod),
                  pl.BlockSpec((1, 1, d), mod),
                  pl.BlockSpec((1, d), full),
                  pl.BlockSpec((d, d), full),
                  pl.BlockSpec((N_EXPERTS, d), full)],
        out_specs=[pl.BlockSpec((1, tm, d), tok),
                   pl.BlockSpec((1, tm, d), tok),
                   pl.BlockSpec((1, N_EXPERTS, tm), lambda b, i: (b, 0, i))],
        out_shape=(jax.ShapeDtypeStruct((bsz, n, d), F32),
                   jax.ShapeDtypeStruct((bsz, n, d), F32),
                   jax.ShapeDtypeStruct((bsz, N_EXPERTS, n), F32)),
        compiler_params=_cparams(("parallel", "parallel")),
        name="outproj_router",
    )(lru, att, x, g1, shift2, scale2, n2g, wo_bf, wr_bf)


TOPK_UNROLL = 4


def _topk_kernel(lg_ref, idx_ref, g_ref, aff_ref, rank_ref, start_ref, tot_ref, *, cap):
    lg = lg_ref[0]
    ne, nb, blk = lg.shape
    ex = jnp.exp(lg - jnp.max(lg, axis=0, keepdims=True))
    aff = ex / jnp.sum(ex, axis=0, keepdims=True)
    aff_ref[...] = aff
    capf = float(cap)

    def bisect(_, c):
        lo, hi = c
        mid = lo + ((hi - lo) >> 1)
        cnt = jnp.sum(jnp.where(aff >= pltpu.bitcast(mid, F32), 1.0, 0.0), axis=(1, 2), keepdims=True)
        ge = cnt >= capf
        return jnp.where(ge, mid, lo), jnp.where(ge, hi, mid)

    lo0 = jnp.zeros((ne, 1, 1), jnp.int32)
    hi0 = jnp.full((ne, 1, 1), 0x3F800001, jnp.int32)
    thr, nxt = lax.fori_loop(0, 31, bisect, (lo0, hi0))
    gt = aff >= pltpu.bitcast(nxt, F32)
    gtf = jnp.where(gt, 1.0, 0.0)
    eqf = jnp.where(jnp.logical_and(aff >= pltpu.bitcast(thr, F32), jnp.logical_not(gt)), 1.0, 0.0)
    need = capf - jnp.sum(gtf, axis=(1, 2), keepdims=True)

    rows = ne * nb
    tri = (lax.broadcasted_iota(jnp.int32, (blk, blk), 0)
           <= lax.broadcasted_iota(jnp.int32, (blk, blk), 1)).astype(BF16)
    r_i = lax.broadcasted_iota(jnp.int32, (rows, rows), 0)
    c_i = lax.broadcasted_iota(jnp.int32, (rows, rows), 1)
    earlier = jnp.logical_and(c_i < r_i, c_i >= (r_i // nb) * nb).astype(BF16)

    def prefix(x3):
        cin = _dot(x3.reshape(rows, blk).astype(BF16), tri)
        tot = jnp.broadcast_to(cin[:, blk - 1:blk], cin.shape)
        start = _dot(earlier, tot.astype(BF16))
        return cin.reshape(x3.shape), start.reshape(x3.shape), tot.reshape(x3.shape)

    cin, start, _ = prefix(eqf)
    sel = gtf + eqf * jnp.where(cin + start <= need, 1.0, 0.0)
    cin, start, tot = prefix(sel)
    rank_ref[...] = cin * sel
    start_ref[...] = start
    tot_ref[...] = tot

    slot = (lax.broadcasted_iota(jnp.int32, (1, cap), 1) + 1).astype(F32)
    row_id = lax.broadcasted_iota(jnp.int32, (nb, 1), 0).astype(F32)
    lane_id = lax.broadcasted_iota(jnp.int32, (blk, 1), 0).astype(F32)

    def per_expert(e):
        a = aff_ref[e]
        st = start_ref[e][:, 0:1]
        in_row = jnp.logical_and(st < slot, slot <= st + tot_ref[e][:, 0:1])
        in_row_bf = jnp.where(in_row, 1.0, 0.0).astype(BF16)
        row_of = jnp.sum(jnp.where(in_row, row_id, 0.0), axis=0, keepdims=True)
        rank_need = slot - jnp.sum(jnp.where(in_row, st, 0.0), axis=0, keepdims=True)

        def pick(x_bf):
            return lax.dot_general(x_bf, in_row_bf, (((0,), (0,)), ((), ())), preferred_element_type=F32)

        hit = pick(rank_ref[e].astype(BF16)) == rank_need
        lane_of = jnp.sum(jnp.where(hit, lane_id, 0.0), axis=0, keepdims=True)
        a_hi = a.astype(BF16)
        a_mid = (a - a_hi.astype(F32)).astype(BF16)
        a_lo = (a - a_hi.astype(F32) - a_mid.astype(F32)).astype(BF16)
        a_sel = (pick(a_hi) + pick(a_mid)) + pick(a_lo)
        idx_ref[0, e] = (row_of * float(blk) + lane_of).astype(jnp.int32)
        g_ref[0, e] = jnp.sum(jnp.where(hit, a_sel, 0.0), axis=0, keepdims=True)

    def expert_group(gi, carry):
        for r in range(TOPK_UNROLL):
            per_expert(gi * TOPK_UNROLL + r)
        return carry

    lax.fori_loop(0, ne // TOPK_UNROLL, expert_group, 0)


def _topk(logits_t, cap):
    bsz, _, n = logits_t.shape
    blk = min(LANES, n)
    nb = n // blk
    slab = pltpu.VMEM((N_EXPERTS, nb, blk), F32)
    return pl.pallas_call(
        functools.partial(_topk_kernel, cap=cap),
        grid=(bsz,),
        in_specs=[pl.BlockSpec((1, N_EXPERTS, nb, blk), lambda b: (b, 0, 0, 0))],
        out_specs=[pl.BlockSpec((1, N_EXPERTS, 1, cap), lambda b: (b, 0, 0, 0)),
                   pl.BlockSpec((1, N_EXPERTS, 1, cap), lambda b: (b, 0, 0, 0))],
        out_shape=(jax.ShapeDtypeStruct((bsz, N_EXPERTS, 1, cap), jnp.int32),
                   jax.ShapeDtypeStruct((bsz, N_EXPERTS, 1, cap), F32)),
        scratch_shapes=[slab, slab, slab, slab],
        compiler_params=_cparams(("parallel",)),
        name="expert_topk",
    )(logits_t.reshape(bsz, N_EXPERTS, nb, blk))


GATHER_UNROLL = 4


EXPERTS_PER_STEP = 1


def _gather_kernel(idx_ref, h_ref, o_ref, *, cap):
    unroll = GATHER_UNROLL if cap % (SUBLANES * GATHER_UNROLL) == 0 else 1
    for g in range(EXPERTS_PER_STEP):
        def body(jj, carry):
            for u in range(unroll):
                j0 = pl.multiple_of((jj * unroll + u) * SUBLANES, SUBLANES)
                rows = [h_ref[0, pl.ds(idx_ref[g, 0, j0 + r], 1), :] for r in range(SUBLANES)]
                o_ref[g, 0, pl.ds(j0, SUBLANES), :] = jnp.concatenate(rows, axis=0).astype(BF16)
            return carry

        lax.fori_loop(0, cap // (SUBLANES * unroll), body, 0)


def _gather(idx3, h2, cap):
    bsz, n, d = h2.shape
    eps = EXPERTS_PER_STEP
    groups = N_EXPERTS // eps
    return pl.pallas_call(
        functools.partial(_gather_kernel, cap=cap),
        grid=(bsz, groups),
        in_specs=[pl.BlockSpec((eps, 1, cap), lambda b, e: (b * groups + e, 0, 0), memory_space=pltpu.SMEM),
                  pl.BlockSpec((1, n, d), lambda b, e: (b, 0, 0))],
        out_specs=pl.BlockSpec((eps, 1, cap, d), lambda b, e: (e, b, 0, 0)),
        out_shape=jax.ShapeDtypeStruct((N_EXPERTS, bsz, cap, d), BF16),
        compiler_params=_cparams(("parallel", "arbitrary")),
        name="expert_gather",
    )(idx3, h2)


FFN_TF = 256


FFN_SUB = 512


def _ffn_kernel(x_ref, wg_ref, wu_ref, wd_ref, g_ref, gate_ref, o_ref, *, cap):
    i = pl.program_id(1)
    f = pl.program_id(2)
    nf = pl.num_programs(2)
    tm = x_ref.shape[1]
    sub = min(FFN_SUB, tm, cap)

    def body(first, last):
        wgu = jnp.concatenate([wg_ref[0].astype(BF16), wu_ref[0].astype(BF16)], axis=1)
        wd = wd_ref[0].astype(BF16)
        tf = wd.shape[0]
        for r in range(tm // sub):
            rows = slice(r * sub, (r + 1) * sub)
            gu = _dot(x_ref[0, rows, :], wgu)
            hg = gu[:, :tf]
            h = (hg * jax.nn.sigmoid(hg)) * gu[:, tf:]
            y = _dot(h.astype(BF16), wd)
            if not first:
                y = o_ref[0, rows, :] + y
            if last:
                sample = (i * tm + r * sub) // cap
                g_col = jnp.transpose(jnp.broadcast_to(g_ref[0, :, rows], (LANES, sub)))[:, 0:1]
                y = y * (g_col * gate_ref[sample])
            o_ref[0, rows, :] = y

    @pl.when(f == 0)
    def _():
        body(True, False)

    @pl.when(jnp.logical_and(f > 0, f < nf - 1))
    def _():
        body(False, False)

    @pl.when(f == nf - 1)
    def _():
        body(False, True)


def _ffn(xe, w_gate, w_up, w_down, g_row, gate2, cap):
    ne, m, d = xe.shape
    dff = w_gate.shape[2]
    tm = min(2048, m)
    tf = FFN_TF
    assert dff // tf > 1 and cap % min(FFN_SUB, tm, cap) == 0
    return pl.pallas_call(
        functools.partial(_ffn_kernel, cap=cap),
        grid=(ne, m // tm, dff // tf),
        in_specs=[pl.BlockSpec((1, tm, d), lambda e, i, f: (e, i, 0)),
                  pl.BlockSpec((1, d, tf), lambda e, i, f: (e, 0, f)),
                  pl.BlockSpec((1, d, tf), lambda e, i, f: (e, 0, f)),
                  pl.BlockSpec((1, tf, d), lambda e, i, f: (e, f, 0)),
                  pl.BlockSpec((1, 1, tm), lambda e, i, f: (e, 0, i)),
                  pl.BlockSpec(gate2.shape, lambda e, i, f: (0, 0, 0))],
        out_specs=pl.BlockSpec((1, tm, d), lambda e, i, f: (e, i, 0)),
        out_shape=jax.ShapeDtypeStruct((ne, m, d), F32),
        compiler_params=_cparams(("parallel", "parallel", "arbitrary")),
        name="expert_ffn",
    )(xe, w_gate, w_up, w_down, g_row, gate2)


def _combine_kernel(idx_ref, x1_hbm, y_ref, o_ref, sem, *, cap):
    b = pl.program_id(0)
    e = pl.program_id(1)

    @pl.when(e == 0)
    def _():
        init = pltpu.make_async_copy(x1_hbm.at[b], o_ref.at[0], sem)
        init.start()
        init.wait()

    sub = lax.broadcasted_iota(jnp.int32, (SUBLANES, o_ref.shape[2]), 0)
    unroll = COMBINE_UNROLL if cap % (SUBLANES * COMBINE_UNROLL) == 0 else 1

    for g in range(EXPERTS_PER_STEP):
        def body(jj, carry):
            for u in range(unroll):
                j0 = pl.multiple_of((jj * unroll + u) * SUBLANES, SUBLANES)
                ys = y_ref[g, 0, pl.ds(j0, SUBLANES), :]
                for r in range(0, SUBLANES, 2):
                    t0 = idx_ref[g, 0, j0 + r]
                    t1 = idx_ref[g, 0, j0 + r + 1]
                    b0 = pl.multiple_of((t0 >> 3) << 3, SUBLANES)
                    b1 = pl.multiple_of((t1 >> 3) << 3, SUBLANES)
                    tile0 = o_ref[0, pl.ds(b0, SUBLANES), :]
                    tile1 = o_ref[0, pl.ds(b1, SUBLANES), :]
                    u0 = jnp.where(sub == (t0 & (SUBLANES - 1)), ys[r:r + 1], 0.0)
                    u1 = jnp.where(sub == (t1 & (SUBLANES - 1)), ys[r + 1:r + 2], 0.0)
                    o_ref[0, pl.ds(b0, SUBLANES), :] = tile0 + u0
                    o_ref[0, pl.ds(b1, SUBLANES), :] = tile1 + u1 + jnp.where(b0 == b1, u0, 0.0)
            return carry

        lax.fori_loop(0, cap // (SUBLANES * unroll), body, 0)


COMBINE_UNROLL = 2


def _combine(idx3, x1, ye, cap):
    bsz, n, d = x1.shape
    eps = EXPERTS_PER_STEP
    groups = N_EXPERTS // eps
    resident = lambda b, e: (b, 0, 0)
    return pl.pallas_call(
        functools.partial(_combine_kernel, cap=cap),
        grid=(bsz, groups),
        in_specs=[pl.BlockSpec((eps, 1, cap), lambda b, e: (b * groups + e, 0, 0), memory_space=pltpu.SMEM),
                  pl.BlockSpec(memory_space=pl.ANY),
                  pl.BlockSpec((eps, 1, cap, d), lambda b, e: (e, b, 0, 0))],
        out_specs=pl.BlockSpec((1, n, d), resident, pipeline_mode=pl.Buffered(1)),
        out_shape=jax.ShapeDtypeStruct((bsz, n, d), F32),
        scratch_shapes=[pltpu.SemaphoreType.DMA],
        compiler_params=_cparams(("parallel", "arbitrary")),
        name="expert_combine",
    )(idx3, x1, ye)


def _rope_tables(n):
    rows = n // GRID_W
    row = jnp.repeat(jnp.arange(rows), GRID_W).astype(F32)
    col = jnp.tile(jnp.arange(GRID_W), rows).astype(F32)
    inv = ROPE_BASE ** (-jnp.arange(ROPE_PAIRS, dtype=F32) / ROPE_PAIRS)
    ang_r = row[:, None] * inv
    ang_c = col[:, None] * inv
    cos64 = jnp.concatenate([jnp.cos(ang_r), jnp.cos(ang_r), jnp.cos(ang_c), jnp.cos(ang_c)], axis=1)
    sin64 = jnp.concatenate([-jnp.sin(ang_r), jnp.sin(ang_r), -jnp.sin(ang_c), jnp.sin(ang_c)], axis=1)
    return jnp.tile(cos64, (1, 2)), jnp.tile(sin64, (1, 2))


def kernel(x, c, ctx, c_ctx, w_ada, b_ada, norm1_g, norm2_g, w_in, conv_w, conv_b, lru_wa, lru_ba, lru_wi,
           lru_bi, lru_lambda, q_norm_g, k_norm_g, lambda_q1, lambda_k1, lambda_q2, lambda_k2, subln_g, w_out,
           w_router, w_gate, w_up, w_down):
    assert w_ada.shape[0] == 1, "single-layer configuration"
    bsz, n, d = x.shape
    ctx_len = ctx.shape[1]
    cap = EC_FACTOR * n // N_EXPERTS

    rows = ((bsz + 1 + SUBLANES - 1) // SUBLANES) * SUBLANES
    cvec = jnp.zeros((rows, d), F32).at[:bsz].set(c).at[bsz].set(c_ctx)
    mod = _adaln(cvec, w_ada[0], b_ada[0][None]).reshape(rows, N_MOD, 1, d)
    mod_l = mod[:bsz]
    mod_c = mod[bsz:bsz + 1]

    w_in_bf = w_in[0].astype(BF16)
    w_out_bf = w_out[0].astype(BF16)
    w_r_bf = w_router[0].T.astype(BF16)
    qg = jnp.tile(q_norm_g[0], QK_WIDTH // HEAD_DIM)[None]
    kg = jnp.tile(k_norm_g[0], QK_WIDTH // HEAD_DIM)[None]
    seg = jnp.arange(MXU_DIM) // HEAD_DIM
    ones_bd = (seg[:, None] == seg[None, :]).astype(BF16)
    cos_t, sin_t = _rope_tables(n)
    cos_c = jnp.ones((ctx_len, LANES), F32)
    sin_c = jnp.zeros((ctx_len, LANES), F32)
    g1 = norm1_g[0][None]

    xl, gate_l, q_l, k_l, v_l = _inproj(x, mod_l[:, 0], mod_l[:, 1], g1, w_in_bf, qg, kg, cos_t, sin_t,
                                        ones_bd, True)
    xc, _, _, k_c, v_c = _inproj(ctx, mod_c[:, 0], mod_c[:, 1], g1, w_in_bf, qg, kg, cos_c, sin_c,
                                 ones_bd, False)

    w_f, b_f = _lru_gate_weights(lru_wa[0, 0], lru_ba[0, 0], lru_wi[0, 0], lru_bi[0, 0])
    w_b, b_b = _lru_gate_weights(lru_wa[0, 1], lru_ba[0, 1], lru_wi[0, 1], lru_bi[0, 1])
    halves = LRU_WIDTH // LRU_HALF
    lam_h = lru_lambda[0].reshape(2, halves, LRU_HALF).transpose(1, 0, 2)
    lru = _rglru(xl, xc, gate_l, conv_w[0], conv_b[0][None], w_f, w_b, b_f, b_b, lam_h)

    att = _attention(q_l, k_c, v_c, k_l, v_l, lambda_q1, lambda_k1, lambda_q2, lambda_k2, subln_g)

    x1, h2, logits = _outproj(lru, att, x, mod_l[:, 2], mod_l[:, 3], mod_l[:, 4], norm2_g[0][None],
                              w_out_bf, w_r_bf)

    idx, gates = _topk(logits, cap)
    idx3 = idx.reshape(bsz * N_EXPERTS, 1, cap)
    g_row = jnp.swapaxes(gates.reshape(bsz, N_EXPERTS, cap), 0, 1).reshape(N_EXPERTS, 1, bsz * cap)
    xe = _gather(idx3, h2, cap)
    ye = _ffn(xe.reshape(N_EXPERTS, bsz * cap, d), w_gate[0], w_up[0], w_down[0], g_row, mod_l[:, 5], cap)
    return _combine(idx3, x1, ye.reshape(N_EXPERTS, bsz, cap, d), cap)
```

```python
import functools

import jax
import jax.numpy as jnp
from jax import lax
from jax.experimental import pallas as pl
from jax.experimental.pallas import tpu as pltpu

F32 = jnp.float32
BF16 = jnp.bfloat16

EPS = 1e-6
GRID_W = 64
LRU_WIDTH = 512
LRU_BLOCKS = 8
LRU_C = 8.0
CONV_W = 4
ATT_HEADS = 4
HEAD_DIM = 64
V_DIM = 2 * HEAD_DIM
QK_WIDTH = ATT_HEADS * 2 * HEAD_DIM
ATT_WIDTH = ATT_HEADS * V_DIM
ROPE_PAIRS = HEAD_DIM // 4
ROPE_BASE = 10000.0
N_EXPERTS = 16
EC_FACTOR = 2
N_MOD = 6
LAM_INIT = 0.2
LOG2_E = 1.4426950408889634

LANES = 128
SUBLANES = 8
MXU_DIM = 256
VMEM_LIMIT = 56 * 1024 * 1024
TOKEN_TILE = 1024


def _cparams(sem):
    return pltpu.CompilerParams(dimension_semantics=sem, vmem_limit_bytes=VMEM_LIMIT)


def _dot(a, b):
    return jnp.dot(a, b, preferred_element_type=F32)


def _dot_nt(a, b):
    return lax.dot_general(a, b, (((1,), (1,)), ((), ())), preferred_element_type=F32)


def _split_bf16(x):
    hi = x.astype(BF16)
    lo = (x - hi.astype(F32)).astype(BF16)
    return hi, lo


def _adaln_kernel(c_ref, w_ref, b_ref, o_ref):
    c = c_ref[...]
    s = c * jax.nn.sigmoid(c)
    s_hi, s_lo = _split_bf16(s)
    w_hi, w_lo = _split_bf16(w_ref[...])
    o_ref[...] = _dot(s_hi, w_hi) + _dot(s_hi, w_lo) + _dot(s_lo, w_hi) + b_ref[...]


def _adaln(cvec, w, b):
    rows, d = cvec.shape
    cols = w.shape[1]
    tn = cols // 4
    return pl.pallas_call(
        _adaln_kernel,
        grid=(cols // tn,),
        in_specs=[pl.BlockSpec((rows, d), lambda j: (0, 0)),
                  pl.BlockSpec((d, tn), lambda j: (0, j)),
                  pl.BlockSpec((1, tn), lambda j: (0, j))],
        out_specs=pl.BlockSpec((rows, tn), lambda j: (0, j)),
        out_shape=jax.ShapeDtypeStruct((rows, cols), F32),
        compiler_params=_cparams(("arbitrary",)),
        name="adaln",
    )(cvec, w, b)


def _swap_halves16(x):
    lane = lax.broadcasted_iota(jnp.int32, x.shape, 1)
    first = (lane % 32) < 16
    return jnp.where(first, pltpu.roll(x, LANES - 16, 1), pltpu.roll(x, 16, 1))


def _qk_norm(t, g, ones_bd):
    outs = []
    for c in range(t.shape[1] // MXU_DIM):
        tc = t[:, c * MXU_DIM:(c + 1) * MXU_DIM]
        hi, lo = _split_bf16(tc * tc)
        ssum = _dot(hi, ones_bd) + _dot(lo, ones_bd)
        outs.append(tc * lax.rsqrt(ssum * (1.0 / HEAD_DIM) + EPS) * g[:, c * MXU_DIM:(c + 1) * MXU_DIM])
    return outs


def _inproj_kernel(x_ref, shift_ref, scale_ref, g1_ref, w_ref, qg_ref, kg_ref, cos_ref, sin_ref, ones_ref,
                   xl_ref, gate_ref, q_ref, k_ref, v_ref, *, use_rope):
    x = x_ref[0]
    ms = jnp.mean(x * x, axis=-1, keepdims=True)
    h = x * lax.rsqrt(ms + EPS) * g1_ref[...]
    h = h * (1.0 + scale_ref[0]) + shift_ref[0]
    p = _dot(h.astype(BF16), w_ref[...])
    o1, o2, o3, o4 = LRU_WIDTH, 2 * LRU_WIDTH, 2 * LRU_WIDTH + QK_WIDTH, 2 * LRU_WIDTH + 2 * QK_WIDTH
    xl_ref[0] = p[:, :o1]
    gate_ref[0] = p[:, o1:o2]
    v_ref[0] = p[:, o4:].astype(BF16)
    ones_bd = ones_ref[...]
    qn = _qk_norm(p[:, o2:o3], qg_ref[...], ones_bd)
    kn = _qk_norm(p[:, o3:o4], kg_ref[...], ones_bd)
    scale = HEAD_DIM ** -0.5 * LOG2_E
    for src, dst, mul in ((qn, q_ref, scale), (kn, k_ref, 1.0)):
        for c, tc in enumerate(src):
            for hh in range(MXU_DIM // LANES):
                th = tc[:, hh * LANES:(hh + 1) * LANES]
                if use_rope:
                    th = th * cos_ref[...] + _swap_halves16(th) * sin_ref[...]
                col = c * MXU_DIM + hh * LANES
                dst[0, :, col:col + LANES] = (th * mul).astype(BF16)


def _inproj(x, shift, scale, g1, w_bf, qg, kg, cos_t, sin_t, ones_bd, use_rope):
    bsz, n, d = x.shape
    tm = min(TOKEN_TILE, n)
    wid = w_bf.shape[1]
    per_b = shift.shape[0] > 1
    mod_map = (lambda b, i: (b, 0, 0)) if per_b else (lambda b, i: (0, 0, 0))
    full = lambda b, i: (0, 0)
    tok = lambda b, i: (b, i, 0)
    out_shapes = (jax.ShapeDtypeStruct((bsz, n, LRU_WIDTH), F32),
                  jax.ShapeDtypeStruct((bsz, n, LRU_WIDTH), F32),
                  jax.ShapeDtypeStruct((bsz, n, QK_WIDTH), BF16),
                  jax.ShapeDtypeStruct((bsz, n, QK_WIDTH), BF16),
                  jax.ShapeDtypeStruct((bsz, n, ATT_WIDTH), BF16))
    return pl.pallas_call(
        functools.partial(_inproj_kernel, use_rope=use_rope),
        grid=(bsz, n // tm),
        in_specs=[pl.BlockSpec((1, tm, d), tok),
                  pl.BlockSpec((1, 1, d), mod_map),
                  pl.BlockSpec((1, 1, d), mod_map),
                  pl.BlockSpec((1, d), full),
                  pl.BlockSpec((d, wid), full),
                  pl.BlockSpec((1, QK_WIDTH), full),
                  pl.BlockSpec((1, QK_WIDTH), full),
                  pl.BlockSpec((tm, LANES), lambda b, i: (i, 0)),
                  pl.BlockSpec((tm, LANES), lambda b, i: (i, 0)),
                  pl.BlockSpec((MXU_DIM, MXU_DIM), full)],
        out_specs=[pl.BlockSpec((1, tm, LRU_WIDTH), tok),
                   pl.BlockSpec((1, tm, LRU_WIDTH), tok),
                   pl.BlockSpec((1, tm, QK_WIDTH), tok),
                   pl.BlockSpec((1, tm, QK_WIDTH), tok),
                   pl.BlockSpec((1, tm, ATT_WIDTH), tok)],
        out_shape=out_shapes,
        compiler_params=_cparams(("parallel", "parallel")),
        name="inproj_rope" if use_rope else "inproj_ctx",
    )(x, shift, scale, g1, w_bf, qg, kg, cos_t, sin_t, ones_bd)


LRU_HALF = LRU_WIDTH // 2
LRU_CHUNK = 512


def _to_strands(x):
    rows, width = x.shape
    return jnp.swapaxes(x.reshape(rows // SUBLANES, SUBLANES, width), 0, 1).reshape(rows, width)


def _from_strands(x):
    rows, width = x.shape
    return jnp.swapaxes(x.reshape(SUBLANES, rows // SUBLANES, width), 0, 1).reshape(rows, width)


def _planes(x):
    g = x.shape[0] // SUBLANES
    return [x[s * g:(s + 1) * g] for s in range(SUBLANES)]


def _shift_down(p, first):
    row = lax.broadcasted_iota(jnp.int32, p.shape, 0)
    return jnp.where(row == 0, first, pltpu.roll(p, 1, 0))


def _shift_up(p, last):
    row = lax.broadcasted_iota(jnp.int32, p.shape, 0)
    return jnp.where(row == p.shape[0] - 1, last, pltpu.roll(p, p.shape[0] - 1, 0))


def _conv_strands(x_ref, t0, rows, total, cw, cb):
    p = _planes(_to_strands(x_ref[0, pl.ds(t0, rows), :]))
    prev_start = pl.multiple_of(jnp.maximum(t0 - SUBLANES, 0), SUBLANES)
    next_start = pl.multiple_of(jnp.minimum(t0 + rows, total - SUBLANES), SUBLANES)
    prev = jnp.where(t0 > 0, x_ref[0, pl.ds(prev_start, SUBLANES), :], 0.0)
    nxt = jnp.where(t0 + rows < total, x_ref[0, pl.ds(next_start, SUBLANES), :], 0.0)
    before = _shift_down(p[SUBLANES - 1], prev[SUBLANES - 1:SUBLANES])
    after0 = _shift_up(p[0], nxt[0:1])
    after1 = _shift_up(p[1], nxt[1:2])
    ext = [before] + p + [after0, after1]
    out = []
    for s in range(SUBLANES):
        acc = cb
        for k in range(CONV_W):
            acc = acc + ext[s + k] * cw[k:k + 1]
        out.append(acc)
    return jnp.concatenate(out, axis=0)


def _sigmoid(x):
    return 0.5 * jnp.tanh(0.5 * x) + 0.5


def _lru_gates(xc, w, bias, sp):
    pre = _dot(xc.astype(BF16), w) + bias
    r = _sigmoid(pre[:, :LRU_HALF])
    i = _sigmoid(pre[:, LRU_HALF:])
    log_a = (-LRU_C * r) * sp
    a = jnp.exp(log_a)
    m = -jnp.tanh(log_a) * (a * a + 1.0)
    mult = jnp.where(m > 0.0, m * lax.rsqrt(m), 0.0)
    return a, mult, i * xc


def _scan_chunk(a, u, h, reverse):
    rows, width = a.shape
    groups = rows // SUBLANES
    a = a.reshape(groups, SUBLANES, width)
    u = u.reshape(groups, SUBLANES, width)
    sub = lax.broadcasted_iota(jnp.int32, a.shape, 1)
    s = 1
    while s < SUBLANES:
        if reverse:
            m = sub < SUBLANES - s
            a_sh = pltpu.roll(a, SUBLANES - s, 1)
            u_sh = pltpu.roll(u, SUBLANES - s, 1)
        else:
            m = sub >= s
            a_sh = pltpu.roll(a, s, 1)
            u_sh = pltpu.roll(u, s, 1)
        u = jnp.where(m, a * u_sh + u, u)
        a = jnp.where(m, a * a_sh, a)
        s *= 2
    outs = [None] * groups
    order = range(groups - 1, -1, -1) if reverse else range(groups)
    for g in order:
        hg = a[g] * h + u[g]
        h = hg[0:1] if reverse else hg[SUBLANES - 1:SUBLANES]
        outs[g] = hg
    return jnp.concatenate(outs, axis=0), h


def _scan_strands(a, u, h, reverse):
    a, u = _planes(a), _planes(u)
    order = list(range(SUBLANES - 1, -1, -1)) if reverse else list(range(SUBLANES))
    for prev_s, s in zip(order[:-1], order[1:]):
        u[s] = a[s] * u[prev_s] + u[s]
        a[s] = a[s] * a[prev_s]
    last = order[-1]
    ends, h_out = _scan_chunk(a[last], u[last], h, reverse)
    carry_in = _shift_up(ends, h) if reverse else _shift_down(ends, h)
    return jnp.concatenate([a[s] * carry_in + u[s] for s in range(SUBLANES)], axis=0), h_out


def _rglru_kernel(xl_ref, xc_ref, gate_ref, cw_ref, cb_ref, wf_ref, wb_ref, bf_ref, bb_ref, lam_ref,
                  o_ref, hf_ref, cl_ref, cc_ref, *, n, ctx_len):
    cw = cw_ref[...]
    cb = cb_ref[...]
    tc = min(LRU_CHUNK, ctx_len)
    tl = min(LRU_CHUNK, n)

    def direction(d, w_ref, b_ref):
        reverse = d == 1
        w = w_ref[0]
        bias = b_ref[0]
        z = -lam_ref[0, pl.ds(d, 1), :]
        sp = jnp.maximum(z, 0.0) + jnp.log1p(jnp.exp(-jnp.abs(z)))
        first_row = ctx_len - 1 if reverse else 0

        def conv(src_ref, cache_ref, t0, rows, total):
            if reverse:
                return cache_ref[pl.ds(t0, rows), :]
            xc = _conv_strands(src_ref, t0, rows, total, cw, cb)
            cache_ref[pl.ds(t0, rows), :] = xc
            return xc

        def ctx_step(c, h):
            cc = (ctx_len // tc - 1 - c) if reverse else c
            t0 = pl.multiple_of(cc * tc, SUBLANES)
            xc = conv(xc_ref, cc_ref, t0, tc, ctx_len)
            a, mult, ix = _lru_gates(xc, w, bias, sp)
            r = lax.broadcasted_iota(jnp.int32, a.shape, 0)
            groups = tc // SUBLANES
            time = (r % groups) * SUBLANES + r // groups + t0
            mult = jnp.where(time == first_row, 1.0, mult)
            _, h = _scan_strands(a, mult * ix, h, reverse)
            return h

        h = lax.fori_loop(0, ctx_len // tc, ctx_step, jnp.zeros((1, LRU_HALF), F32))

        def lat_step(c, h):
            cc = (n // tl - 1 - c) if reverse else c
            t0 = pl.multiple_of(cc * tl, SUBLANES)
            xc = conv(xl_ref, cl_ref, t0, tl, n)
            a, mult, ix = _lru_gates(xc, w, bias, sp)
            hs, h = _scan_strands(a, mult * ix, h, reverse)
            if reverse:
                y = _from_strands(hf_ref[pl.ds(t0, tl), :] + hs) * jax.nn.gelu(gate_ref[0, pl.ds(t0, tl), :])
                o_ref[0, pl.ds(t0, tl), :] = y.astype(BF16)
            else:
                hf_ref[pl.ds(t0, tl), :] = hs
            return h

        lax.fori_loop(0, n // tl, lat_step, h)

    direction(0, wf_ref, bf_ref)
    direction(1, wb_ref, bb_ref)


def _rglru(xl, xc, gate, conv_w, conv_b, w_f, w_b, b_f, b_b, lam):
    bsz, n, _ = xl.shape
    ctx_len = xc.shape[1]
    halves = LRU_WIDTH // LRU_HALF
    tokh = lambda b, hf: (b, 0, hf)
    return pl.pallas_call(
        functools.partial(_rglru_kernel, n=n, ctx_len=ctx_len),
        grid=(bsz, halves),
        in_specs=[pl.BlockSpec((1, n, LRU_HALF), tokh),
                  pl.BlockSpec((1, ctx_len, LRU_HALF), tokh),
                  pl.BlockSpec((1, n, LRU_HALF), tokh),
                  pl.BlockSpec((CONV_W, LRU_HALF), lambda b, hf: (0, hf)),
                  pl.BlockSpec((1, LRU_HALF), lambda b, hf: (0, hf)),
                  pl.BlockSpec((1, LRU_HALF, 2 * LRU_HALF), lambda b, hf: (hf, 0, 0)),
                  pl.BlockSpec((1, LRU_HALF, 2 * LRU_HALF), lambda b, hf: (hf, 0, 0)),
                  pl.BlockSpec((1, 1, 2 * LRU_HALF), lambda b, hf: (hf, 0, 0)),
                  pl.BlockSpec((1, 1, 2 * LRU_HALF), lambda b, hf: (hf, 0, 0)),
                  pl.BlockSpec((1, 2, LRU_HALF), lambda b, hf: (hf, 0, 0))],
        out_specs=pl.BlockSpec((1, n, LRU_HALF), tokh),
        out_shape=jax.ShapeDtypeStruct((bsz, n, LRU_WIDTH), BF16),
        scratch_shapes=[pltpu.VMEM((n, LRU_HALF), F32),
                        pltpu.VMEM((n, LRU_HALF), F32),
                        pltpu.VMEM((ctx_len, LRU_HALF), F32)],
        compiler_params=_cparams(("parallel", "parallel")),
        name="rglru",
    )(xl, xc, gate, conv_w, conv_b, w_f, w_b, b_f, b_b, lam)


def _lru_gate_weights(wa, ba, wi, bi):
    halves = LRU_WIDTH // LRU_HALF
    per = LRU_BLOCKS // halves
    bw = LRU_WIDTH // LRU_BLOCKS

    def dense(w):
        w = w.reshape(halves, per, bw, bw)
        eye = jnp.eye(per, dtype=w.dtype)
        return jnp.einsum("hpij,pq->hpiqj", w, eye).reshape(halves, per * bw, per * bw)

    w_cat = jnp.concatenate([dense(wa), dense(wi)], axis=-1).astype(BF16)
    b_cat = jnp.concatenate([ba.reshape(halves, 1, LRU_HALF), bi.reshape(halves, 1, LRU_HALF)], axis=-1)
    return w_cat, b_cat


ATT_TQ = 256
ATT_TK = 256


def _attn_kernel(q_ref, kc_ref, vc_ref, kl_ref, vl_ref, lq1_ref, lk1_ref, lq2_ref, lk2_ref, sg_ref, o_ref,
                 *, n, tk):
    q = q_ref[0]
    tq = q.shape[0]
    lane = lax.broadcasted_iota(jnp.int32, q.shape, 1)
    zero = jnp.zeros_like(q)
    q2 = jnp.concatenate([jnp.where(lane < HEAD_DIM, q, zero), jnp.where(lane >= HEAD_DIM, q, zero)], axis=0)

    def ext(v):
        ln = lax.broadcasted_iota(jnp.int32, v.shape, 1)
        return jnp.concatenate([v, jnp.where(ln == 0, 1.0, 0.0).astype(BF16)], axis=1)

    def step(k, v, carry):
        m, acc = carry
        s = _dot_nt(q2, k)
        m_new = jnp.maximum(m, jnp.max(s, axis=1, keepdims=True))
        p = jnp.exp2(s - m_new).astype(BF16)
        acc = jnp.exp2(m - m_new) * acc + _dot(p, ext(v))
        return m_new, acc

    carry = (jnp.full((2 * tq, 1), -1e30, F32), jnp.zeros((2 * tq, 2 * V_DIM), F32))
    carry = step(kc_ref[0], vc_ref[0], carry)
    for j in range(n // tk):
        carry = step(kl_ref[0, j * tk:(j + 1) * tk, :], vl_ref[0, j * tk:(j + 1) * tk, :], carry)
    _, acc = carry
    o = acc[:, :V_DIM] / acc[:, V_DIM:V_DIM + 1]
    lam = (jnp.exp(jnp.sum(lq1_ref[...] * lk1_ref[...], keepdims=True))
           - jnp.exp(jnp.sum(lq2_ref[...] * lk2_ref[...], keepdims=True)) + LAM_INIT)
    att = o[:tq] - lam * o[tq:]
    ms = jnp.mean(att * att, axis=-1, keepdims=True)
    y = att * lax.rsqrt(ms + EPS) * sg_ref[...]
    o_ref[0] = (y * (1.0 - LAM_INIT)).astype(BF16)


def _attention(q, kc, vc, kl, vl, lq1, lk1, lq2, lk2, sg):
    bsz, n, _ = q.shape
    ctx_len = kc.shape[1]
    tq = min(ATT_TQ, n)
    tk = min(ATT_TK, n)
    vec = lambda b, h, i: (0, 0)
    return pl.pallas_call(
        functools.partial(_attn_kernel, n=n, tk=tk),
        grid=(bsz, ATT_HEADS, n // tq),
        in_specs=[pl.BlockSpec((1, tq, V_DIM), lambda b, h, i: (b, i, h)),
                  pl.BlockSpec((1, ctx_len, V_DIM), lambda b, h, i: (b, 0, h)),
                  pl.BlockSpec((1, ctx_len, V_DIM), lambda b, h, i: (b, 0, h)),
                  pl.BlockSpec((1, n, V_DIM), lambda b, h, i: (b, 0, h)),
                  pl.BlockSpec((1, n, V_DIM), lambda b, h, i: (b, 0, h)),
                  pl.BlockSpec((1, HEAD_DIM), vec),
                  pl.BlockSpec((1, HEAD_DIM), vec),
                  pl.BlockSpec((1, HEAD_DIM), vec),
                  pl.BlockSpec((1, HEAD_DIM), vec),
                  pl.BlockSpec((1, V_DIM), vec)],
        out_specs=pl.BlockSpec((1, tq, V_DIM), lambda b, h, i: (b, i, h)),
        out_shape=jax.ShapeDtypeStruct((bsz, n, ATT_WIDTH), BF16),
        compiler_params=_cparams(("parallel", "parallel", "arbitrary")),
        name="diff_attention",
    )(q, kc, vc, kl, vl, lq1, lk1, lq2, lk2, sg)


def _outproj_kernel(lru_ref, att_ref, x_ref, g1_ref, shift_ref, scale_ref, n2_ref, wo_ref, wr_ref,
                    x1_ref, h2_ref, lg_ref):
    mix = _dot(lru_ref[0], wo_ref[:LRU_WIDTH, :]) + _dot(att_ref[0], wo_ref[LRU_WIDTH:, :])
    x1 = x_ref[0] + g1_ref[0] * mix
    x1_ref[0] = x1
    ms = jnp.mean(x1 * x1, axis=-1, keepdims=True)
    h2 = x1 * lax.rsqrt(ms + EPS) * n2_ref[...]
    h2 = h2 * (1.0 + scale_ref[0]) + shift_ref[0]
    h2_ref[0] = h2
    lg_ref[0] = _dot_nt(wr_ref[...], h2.astype(BF16))


def _outproj(lru, att, x, g1, shift2, scale2, n2g, wo_bf, wr_bf):
    bsz, n, d = x.shape
    tm = min(TOKEN_TILE, n)
    tok = lambda b, i: (b, i, 0)
    mod = lambda b, i: (b, 0, 0)
    full = lambda b, i: (0, 0)
    return pl.pallas_call(
        _outproj_kernel,
        grid=(bsz, n // tm),
        in_specs=[pl.BlockSpec((1, tm, LRU_WIDTH), tok),
                  pl.BlockSpec((1, tm, ATT_WIDTH), tok),
                  pl.BlockSpec((1, tm, d), tok),
                  pl.BlockSpec((1, 1, d), mod),
                  pl.BlockSpec((1, 1, d), mod),
                  pl.BlockSpec((1, 1, d), mod),
                  pl.BlockSpec((1, d), full),
                  pl.BlockSpec((d, d), full),
                  pl.BlockSpec((N_EXPERTS, d), full)],
        out_specs=[pl.BlockSpec((1, tm, d), tok),
                   pl.BlockSpec((1, tm, d), tok),
                   pl.BlockSpec((1, N_EXPERTS, tm), lambda b, i: (b, 0, i))],
        out_shape=(jax.ShapeDtypeStruct((bsz, n, d), F32),
                   jax.ShapeDtypeStruct((bsz, n, d), F32),
                   jax.ShapeDtypeStruct((bsz, N_EXPERTS, n), F32)),
        compiler_params=_cparams(("parallel", "parallel")),
        name="outproj_router",
    )(lru, att, x, g1, shift2, scale2, n2g, wo_bf, wr_bf)


TOPK_UNROLL = 4


def _topk_kernel(lg_ref, idx_ref, g_ref, aff_ref, rank_ref, start_ref, tot_ref, *, cap):
    lg = lg_ref[0]
    ne, nb, blk = lg.shape
    ex = jnp.exp(lg - jnp.max(lg, axis=0, keepdims=True))
    aff = ex / jnp.sum(ex, axis=0, keepdims=True)
    aff_ref[...] = aff
    capf = float(cap)

    def bisect(_, c):
        lo, hi = c
        mid = lo + ((hi - lo) >> 1)
        cnt = jnp.sum(jnp.where(aff >= pltpu.bitcast(mid, F32), 1.0, 0.0), axis=(1, 2), keepdims=True)
        ge = cnt >= capf
        return jnp.where(ge, mid, lo), jnp.where(ge, hi, mid)

    lo0 = jnp.zeros((ne, 1, 1), jnp.int32)
    hi0 = jnp.full((ne, 1, 1), 0x3F800001, jnp.int32)
    thr, nxt = lax.fori_loop(0, 31, bisect, (lo0, hi0))
    gt = aff >= pltpu.bitcast(nxt, F32)
    gtf = jnp.where(gt, 1.0, 0.0)
    eqf = jnp.where(jnp.logical_and(aff >= pltpu.bitcast(thr, F32), jnp.logical_not(gt)), 1.0, 0.0)
    need = capf - jnp.sum(gtf, axis=(1, 2), keepdims=True)

    rows = ne * nb
    tri = (lax.broadcasted_iota(jnp.int32, (blk, blk), 0)
           <= lax.broadcasted_iota(jnp.int32, (blk, blk), 1)).astype(BF16)
    r_i = lax.broadcasted_iota(jnp.int32, (rows, rows), 0)
    c_i = lax.broadcasted_iota(jnp.int32, (rows, rows), 1)
    earlier = jnp.logical_and(c_i < r_i, c_i >= (r_i // nb) * nb).astype(BF16)

    def prefix(x3):
        cin = _dot(x3.reshape(rows, blk).astype(BF16), tri)
        tot = jnp.broadcast_to(cin[:, blk - 1:blk], cin.shape)
        start = _dot(earlier, tot.astype(BF16))
        return cin.reshape(x3.shape), start.reshape(x3.shape), tot.reshape(x3.shape)

    cin, start, _ = prefix(eqf)
    sel = gtf + eqf * jnp.where(cin + start <= need, 1.0, 0.0)
    cin, start, tot = prefix(sel)
    rank_ref[...] = cin * sel
    start_ref[...] = start
    tot_ref[...] = tot

    slot = (lax.broadcasted_iota(jnp.int32, (1, cap), 1) + 1).astype(F32)
    row_id = lax.broadcasted_iota(jnp.int32, (nb, 1), 0).astype(F32)
    lane_id = lax.broadcasted_iota(jnp.int32, (blk, 1), 0).astype(F32)

    def per_expert(e):
        a = aff_ref[e]
        st = start_ref[e][:, 0:1]
        in_row = jnp.logical_and(st < slot, slot <= st + tot_ref[e][:, 0:1])
        in_row_bf = jnp.where(in_row, 1.0, 0.0).astype(BF16)
        row_of = jnp.sum(jnp.where(in_row, row_id, 0.0), axis=0, keepdims=True)
        rank_need = slot - jnp.sum(jnp.where(in_row, st, 0.0), axis=0, keepdims=True)

        def pick(x_bf):
            return lax.dot_general(x_bf, in_row_bf, (((0,), (0,)), ((), ())), preferred_element_type=F32)

        hit = pick(rank_ref[e].astype(BF16)) == rank_need
        lane_of = jnp.sum(jnp.where(hit, lane_id, 0.0), axis=0, keepdims=True)
        a_hi = a.astype(BF16)
        a_mid = (a - a_hi.astype(F32)).astype(BF16)
        a_lo = (a - a_hi.astype(F32) - a_mid.astype(F32)).astype(BF16)
        a_sel = (pick(a_hi) + pick(a_mid)) + pick(a_lo)
        idx_ref[0, e] = (row_of * float(blk) + lane_of).astype(jnp.int32)
        g_ref[0, e] = jnp.sum(jnp.where(hit, a_sel, 0.0), axis=0, keepdims=True)

    def expert_group(gi, carry):
        for r in range(TOPK_UNROLL):
            per_expert(gi * TOPK_UNROLL + r)
        return carry

    lax.fori_loop(0, ne // TOPK_UNROLL, expert_group, 0)


def _topk(logits_t, cap):
    bsz, _, n = logits_t.shape
    blk = min(LANES, n)
    nb = n // blk
    slab = pltpu.VMEM((N_EXPERTS, nb, blk), F32)
    return pl.pallas_call(
        functools.partial(_topk_kernel, cap=cap),
        grid=(bsz,),
        in_specs=[pl.BlockSpec((1, N_EXPERTS, nb, blk), lambda b: (b, 0, 0, 0))],
        out_specs=[pl.BlockSpec((1, N_EXPERTS, 1, cap), lambda b: (b, 0, 0, 0)),
                   pl.BlockSpec((1, N_EXPERTS, 1, cap), lambda b: (b, 0, 0, 0))],
        out_shape=(jax.ShapeDtypeStruct((bsz, N_EXPERTS, 1, cap), jnp.int32),
                   jax.ShapeDtypeStruct((bsz, N_EXPERTS, 1, cap), F32)),
        scratch_shapes=[slab, slab, slab, slab],
        compiler_params=_cparams(("parallel",)),
        name="expert_topk",
    )(logits_t.reshape(bsz, N_EXPERTS, nb, blk))


GATHER_UNROLL = 4


EXPERTS_PER_STEP = 1


def _gather_kernel(idx_ref, h_ref, o_ref, *, cap):
    unroll = GATHER_UNROLL if cap % (SUBLANES * GATHER_UNROLL) == 0 else 1
    for g in range(EXPERTS_PER_STEP):
        def body(jj, carry):
            for u in range(unroll):
                j0 = pl.multiple_of((jj * unroll + u) * SUBLANES, SUBLANES)
                rows = [h_ref[0, pl.ds(idx_ref[g, 0, j0 + r], 1), :] for r in range(SUBLANES)]
                o_ref[g, 0, pl.ds(j0, SUBLANES), :] = jnp.concatenate(rows, axis=0).astype(BF16)
            return carry

        lax.fori_loop(0, cap // (SUBLANES * unroll), body, 0)


def _gather(idx3, h2, cap):
    bsz, n, d = h2.shape
    eps = EXPERTS_PER_STEP
    groups = N_EXPERTS // eps
    return pl.pallas_call(
        functools.partial(_gather_kernel, cap=cap),
        grid=(bsz, groups),
        in_specs=[pl.BlockSpec((eps, 1, cap), lambda b, e: (b * groups + e, 0, 0), memory_space=pltpu.SMEM),
                  pl.BlockSpec((1, n, d), lambda b, e: (b, 0, 0))],
        out_specs=pl.BlockSpec((eps, 1, cap, d), lambda b, e: (e, b, 0, 0)),
        out_shape=jax.ShapeDtypeStruct((N_EXPERTS, bsz, cap, d), BF16),
        compiler_params=_cparams(("parallel", "arbitrary")),
        name="expert_gather",
    )(idx3, h2)


FFN_TF = 256


FFN_SUB = 512


def _ffn_kernel(x_ref, wg_ref, wu_ref, wd_ref, g_ref, gate_ref, o_ref, *, cap):
    i = pl.program_id(1)
    f = pl.program_id(2)
    nf = pl.num_programs(2)
    tm = x_ref.shape[1]
    sub = min(FFN_SUB, tm, cap)

    def body(first, last):
        wgu = jnp.concatenate([wg_ref[0].astype(BF16), wu_ref[0].astype(BF16)], axis=1)
        wd = wd_ref[0].astype(BF16)
        tf = wd.shape[0]
        for r in range(tm // sub):
            rows = slice(r * sub, (r + 1) * sub)
            gu = _dot(x_ref[0, rows, :], wgu)
            hg = gu[:, :tf]
            h = (hg * jax.nn.sigmoid(hg)) * gu[:, tf:]
            y = _dot(h.astype(BF16), wd)
            if not first:
                y = o_ref[0, rows, :] + y
            if last:
                sample = (i * tm + r * sub) // cap
                g_col = jnp.transpose(jnp.broadcast_to(g_ref[0, :, rows], (LANES, sub)))[:, 0:1]
                y = y * (g_col * gate_ref[sample])
            o_ref[0, rows, :] = y

    @pl.when(f == 0)
    def _():
        body(True, False)

    @pl.when(jnp.logical_and(f > 0, f < nf - 1))
    def _():
        body(False, False)

    @pl.when(f == nf - 1)
    def _():
        body(False, True)


def _ffn(xe, w_gate, w_up, w_down, g_row, gate2, cap):
    ne, m, d = xe.shape
    dff = w_gate.shape[2]
    tm = min(2048, m)
    tf = FFN_TF
    assert dff // tf > 1 and cap % min(FFN_SUB, tm, cap) == 0
    return pl.pallas_call(
        functools.partial(_ffn_kernel, cap=cap),
        grid=(ne, m // tm, dff // tf),
        in_specs=[pl.BlockSpec((1, tm, d), lambda e, i, f: (e, i, 0)),
                  pl.BlockSpec((1, d, tf), lambda e, i, f: (e, 0, f)),
                  pl.BlockSpec((1, d, tf), lambda e, i, f: (e, 0, f)),
                  pl.BlockSpec((1, tf, d), lambda e, i, f: (e, f, 0)),
                  pl.BlockSpec((1, 1, tm), lambda e, i, f: (e, 0, i)),
                  pl.BlockSpec(gate2.shape, lambda e, i, f: (0, 0, 0))],
        out_specs=pl.BlockSpec((1, tm, d), lambda e, i, f: (e, i, 0)),
        out_shape=jax.ShapeDtypeStruct((ne, m, d), F32),
        compiler_params=_cparams(("parallel", "parallel", "arbitrary")),
        name="expert_ffn",
    )(xe, w_gate, w_up, w_down, g_row, gate2)


def _combine_kernel(idx_ref, x1_hbm, y_ref, o_ref, sem, *, cap):
    b = pl.program_id(0)
    e = pl.program_id(1)

    @pl.when(e == 0)
    def _():
        init = pltpu.make_async_copy(x1_hbm.at[b], o_ref.at[0], sem)
        init.start()
        init.wait()

    sub = lax.broadcasted_iota(jnp.int32, (SUBLANES, o_ref.shape[2]), 0)
    unroll = COMBINE_UNROLL if cap % (SUBLANES * COMBINE_UNROLL) == 0 else 1

    for g in range(EXPERTS_PER_STEP):
        def body(jj, carry):
            for u in range(unroll):
                j0 = pl.multiple_of((jj * unroll + u) * SUBLANES, SUBLANES)
                ys = y_ref[g, 0, pl.ds(j0, SUBLANES), :]
                for r in range(0, SUBLANES, 2):
                    t0 = idx_ref[g, 0, j0 + r]
                    t1 = idx_ref[g, 0, j0 + r + 1]
                    b0 = pl.multiple_of((t0 >> 3) << 3, SUBLANES)
                    b1 = pl.multiple_of((t1 >> 3) << 3, SUBLANES)
                    tile0 = o_ref[0, pl.ds(b0, SUBLANES), :]
                    tile1 = o_ref[0, pl.ds(b1, SUBLANES), :]
                    u0 = jnp.where(sub == (t0 & (SUBLANES - 1)), ys[r:r + 1], 0.0)
                    u1 = jnp.where(sub == (t1 & (SUBLANES - 1)), ys[r + 1:r + 2], 0.0)
                    o_ref[0, pl.ds(b0, SUBLANES), :] = tile0 + u0
                    o_ref[0, pl.ds(b1, SUBLANES), :] = tile1 + u1 + jnp.where(b0 == b1, u0, 0.0)
            return carry

        lax.fori_loop(0, cap // (SUBLANES * unroll), body, 0)


COMBINE_UNROLL = 2


def _combine(idx3, x1, ye, cap):
    bsz, n, d = x1.shape
    eps = EXPERTS_PER_STEP
    groups = N_EXPERTS // eps
    resident = lambda b, e: (b, 0, 0)
    return pl.pallas_call(
        functools.partial(_combine_kernel, cap=cap),
        grid=(bsz, groups),
        in_specs=[pl.BlockSpec((eps, 1, cap), lambda b, e: (b * groups + e, 0, 0), memory_space=pltpu.SMEM),
                  pl.BlockSpec(memory_space=pl.ANY),
                  pl.BlockSpec((eps, 1, cap, d), lambda b, e: (e, b, 0, 0))],
        out_specs=pl.BlockSpec((1, n, d), resident, pipeline_mode=pl.Buffered(1)),
        out_shape=jax.ShapeDtypeStruct((bsz, n, d), F32),
        scratch_shapes=[pltpu.SemaphoreType.DMA],
        compiler_params=_cparams(("parallel", "arbitrary")),
        name="expert_combine",
    )(idx3, x1, ye)


def _rope_tables(n):
    rows = n // GRID_W
    row = jnp.repeat(jnp.arange(rows), GRID_W).astype(F32)
    col = jnp.tile(jnp.arange(GRID_W), rows).astype(F32)
    inv = ROPE_BASE ** (-jnp.arange(ROPE_PAIRS, dtype=F32) / ROPE_PAIRS)
    ang_r = row[:, None] * inv
    ang_c = col[:, None] * inv
    cos64 = jnp.concatenate([jnp.cos(ang_r), jnp.cos(ang_r), jnp.cos(ang_c), jnp.cos(ang_c)], axis=1)
    sin64 = jnp.concatenate([-jnp.sin(ang_r), jnp.sin(ang_r), -jnp.sin(ang_c), jnp.sin(ang_c)], axis=1)
    return jnp.tile(cos64, (1, 2)), jnp.tile(sin64, (1, 2))


def kernel(x, c, ctx, c_ctx, w_ada, b_ada, norm1_g, norm2_g, w_in, conv_w, conv_b, lru_wa, lru_ba, lru_wi,
           lru_bi, lru_lambda, q_norm_g, k_norm_g, lambda_q1, lambda_k1, lambda_q2, lambda_k2, subln_g, w_out,
           w_router, w_gate, w_up, w_down):
    assert w_ada.shape[0] == 1, "single-layer configuration"
    bsz, n, d = x.shape
    ctx_len = ctx.shape[1]
    cap = EC_FACTOR * n // N_EXPERTS

    rows = ((bsz + 1 + SUBLANES - 1) // SUBLANES) * SUBLANES
    cvec = jnp.zeros((rows, d), F32).at[:bsz].set(c).at[bsz].set(c_ctx)
    mod = _adaln(cvec, w_ada[0], b_ada[0][None]).reshape(rows, N_MOD, 1, d)
    mod_l = mod[:bsz]
    mod_c = mod[bsz:bsz + 1]

    w_in_bf = w_in[0].astype(BF16)
    w_out_bf = w_out[0].astype(BF16)
    w_r_bf = w_router[0].T.astype(BF16)
    qg = jnp.tile(q_norm_g[0], QK_WIDTH // HEAD_DIM)[None]
    kg = jnp.tile(k_norm_g[0], QK_WIDTH // HEAD_DIM)[None]
    seg = jnp.arange(MXU_DIM) // HEAD_DIM
    ones_bd = (seg[:, None] == seg[None, :]).astype(BF16)
    cos_t, sin_t = _rope_tables(n)
    cos_c = jnp.ones((ctx_len, LANES), F32)
    sin_c = jnp.zeros((ctx_len, LANES), F32)
    g1 = norm1_g[0][None]

    xl, gate_l, q_l, k_l, v_l = _inproj(x, mod_l[:, 0], mod_l[:, 1], g1, w_in_bf, qg, kg, cos_t, sin_t,
                                        ones_bd, True)
    xc, _, _, k_c, v_c = _inproj(ctx, mod_c[:, 0], mod_c[:, 1], g1, w_in_bf, qg, kg, cos_c, sin_c,
                                 ones_bd, False)

    w_f, b_f = _lru_gate_weights(lru_wa[0, 0], lru_ba[0, 0], lru_wi[0, 0], lru_bi[0, 0])
    w_b, b_b = _lru_gate_weights(lru_wa[0, 1], lru_ba[0, 1], lru_wi[0, 1], lru_bi[0, 1])
    halves = LRU_WIDTH // LRU_HALF
    lam_h = lru_lambda[0].reshape(2, halves, LRU_HALF).transpose(1, 0, 2)
    lru = _rglru(xl, xc, gate_l, conv_w[0], conv_b[0][None], w_f, w_b, b_f, b_b, lam_h)

    att = _attention(q_l, k_c, v_c, k_l, v_l, lambda_q1, lambda_k1, lambda_q2, lambda_k2, subln_g)

    x1, h2, logits = _outproj(lru, att, x, mod_l[:, 2], mod_l[:, 3], mod_l[:, 4], norm2_g[0][None],
                              w_out_bf, w_r_bf)

    idx, gates = _topk(logits, cap)
    idx3 = idx.reshape(bsz * N_EXPERTS, 1, cap)
    g_row = jnp.swapaxes(gates.reshape(bsz, N_EXPERTS, cap), 0, 1).reshape(N_EXPERTS, 1, bsz * cap)
    xe = _gather(idx3, h2, cap)
    ye = _ffn(xe.reshape(N_EXPERTS, bsz * cap, d), w_gate[0], w_up[0], w_down[0], g_row, mod_l[:, 5], cap)
    return _combine(idx3, x1, ye.reshape(N_EXPERTS, bsz, cap, d), cap)
```

```python
import functools

import jax
import jax.numpy as jnp
from jax import lax
from jax.experimental import pallas as pl
from jax.experimental.pallas import tpu as pltpu

F32 = jnp.float32
BF16 = jnp.bfloat16

EPS = 1e-6
GRID_W = 64
LRU_WIDTH = 512
LRU_BLOCKS = 8
LRU_C = 8.0
CONV_W = 4
ATT_HEADS = 4
HEAD_DIM = 64
V_DIM = 2 * HEAD_DIM
QK_WIDTH = ATT_HEADS * 2 * HEAD_DIM
ATT_WIDTH = ATT_HEADS * V_DIM
ROPE_PAIRS = HEAD_DIM // 4
ROPE_BASE = 10000.0
N_EXPERTS = 16
EC_FACTOR = 2
N_MOD = 6
LAM_INIT = 0.2
LOG2_E = 1.4426950408889634

LANES = 128
SUBLANES = 8
MXU_DIM = 256
VMEM_LIMIT = 56 * 1024 * 1024
TOKEN_TILE = 1024


def _cparams(sem):
    return pltpu.CompilerParams(dimension_semantics=sem, vmem_limit_bytes=VMEM_LIMIT)


def _dot(a, b):
    return jnp.dot(a, b, preferred_element_type=F32)


def _dot_nt(a, b):
    return lax.dot_general(a, b, (((1,), (1,)), ((), ())), preferred_element_type=F32)


def _split_bf16(x):
    hi = x.astype(BF16)
    lo = (x - hi.astype(F32)).astype(BF16)
    return hi, lo


def _adaln_kernel(c_ref, w_ref, b_ref, o_ref):
    c = c_ref[...]
    s = c * jax.nn.sigmoid(c)
    s_hi, s_lo = _split_bf16(s)
    w_hi, w_lo = _split_bf16(w_ref[...])
    o_ref[...] = _dot(s_hi, w_hi) + _dot(s_hi, w_lo) + _dot(s_lo, w_hi) + b_ref[...]


def _adaln(cvec, w, b):
    rows, d = cvec.shape
    cols = w.shape[1]
    tn = cols // 4
    return pl.pallas_call(
        _adaln_kernel,
        grid=(cols // tn,),
        in_specs=[pl.BlockSpec((rows, d), lambda j: (0, 0)),
                  pl.BlockSpec((d, tn), lambda j: (0, j)),
                  pl.BlockSpec((1, tn), lambda j: (0, j))],
        out_specs=pl.BlockSpec((rows, tn), lambda j: (0, j)),
        out_shape=jax.ShapeDtypeStruct((rows, cols), F32),
        compiler_params=_cparams(("arbitrary",)),
        name="adaln",
    )(cvec, w, b)


def _swap_halves16(x):
    lane = lax.broadcasted_iota(jnp.int32, x.shape, 1)
    first = (lane % 32) < 16
    return jnp.where(first, pltpu.roll(x, LANES - 16, 1), pltpu.roll(x, 16, 1))


def _qk_norm(t, g, ones_bd):
    outs = []
    for c in range(t.shape[1] // MXU_DIM):
        tc = t[:, c * MXU_DIM:(c + 1) * MXU_DIM]
        hi, lo = _split_bf16(tc * tc)
        ssum = _dot(hi, ones_bd) + _dot(lo, ones_bd)
        outs.append(tc * lax.rsqrt(ssum * (1.0 / HEAD_DIM) + EPS) * g[:, c * MXU_DIM:(c + 1) * MXU_DIM])
    return outs


def _inproj_kernel(x_ref, shift_ref, scale_ref, g1_ref, w_ref, qg_ref, kg_ref, cos_ref, sin_ref, ones_ref,
                   xl_ref, gate_ref, q_ref, k_ref, v_ref, *, use_rope):
    x = x_ref[0]
    ms = jnp.mean(x * x, axis=-1, keepdims=True)
    h = x * lax.rsqrt(ms + EPS) * g1_ref[...]
    h = h * (1.0 + scale_ref[0]) + shift_ref[0]
    p = _dot(h.astype(BF16), w_ref[...])
    o1, o2, o3, o4 = LRU_WIDTH, 2 * LRU_WIDTH, 2 * LRU_WIDTH + QK_WIDTH, 2 * LRU_WIDTH + 2 * QK_WIDTH
    xl_ref[0] = p[:, :o1]
    gate_ref[0] = p[:, o1:o2]
    v_ref[0] = p[:, o4:].astype(BF16)
    ones_bd = ones_ref[...]
    qn = _qk_norm(p[:, o2:o3], qg_ref[...], ones_bd)
    kn = _qk_norm(p[:, o3:o4], kg_ref[...], ones_bd)
    scale = HEAD_DIM ** -0.5 * LOG2_E
    for src, dst, mul in ((qn, q_ref, scale), (kn, k_ref, 1.0)):
        for c, tc in enumerate(src):
            for hh in range(MXU_DIM // LANES):
                th = tc[:, hh * LANES:(hh + 1) * LANES]
                if use_rope:
                    th = th * cos_ref[...] + _swap_halves16(th) * sin_ref[...]
                col = c * MXU_DIM + hh * LANES
                dst[0, :, col:col + LANES] = (th * mul).astype(BF16)


def _inproj(x, shift, scale, g1, w_bf, qg, kg, cos_t, sin_t, ones_bd, use_rope):
    bsz, n, d = x.shape
    tm = min(TOKEN_TILE, n)
    wid = w_bf.shape[1]
    per_b = shift.shape[0] > 1
    mod_map = (lambda b, i: (b, 0, 0)) if per_b else (lambda b, i: (0, 0, 0))
    full = lambda b, i: (0, 0)
    tok = lambda b, i: (b, i, 0)
    out_shapes = (jax.ShapeDtypeStruct((bsz, n, LRU_WIDTH), F32),
                  jax.ShapeDtypeStruct((bsz, n, LRU_WIDTH), F32),
                  jax.ShapeDtypeStruct((bsz, n, QK_WIDTH), BF16),
                  jax.ShapeDtypeStruct((bsz, n, QK_WIDTH), BF16),
                  jax.ShapeDtypeStruct((bsz, n, ATT_WIDTH), BF16))
    return pl.pallas_call(
        functools.partial(_inproj_kernel, use_rope=use_rope),
        grid=(bsz, n // tm),
        in_specs=[pl.BlockSpec((1, tm, d), tok),
                  pl.BlockSpec((1, 1, d), mod_map),
                  pl.BlockSpec((1, 1, d), mod_map),
                  pl.BlockSpec((1, d), full),
                  pl.BlockSpec((d, wid), full),
                  pl.BlockSpec((1, QK_WIDTH), full),
                  pl.BlockSpec((1, QK_WIDTH), full),
                  pl.BlockSpec((tm, LANES), lambda b, i: (i, 0)),
                  pl.BlockSpec((tm, LANES), lambda b, i: (i, 0)),
                  pl.BlockSpec((MXU_DIM, MXU_DIM), full)],
        out_specs=[pl.BlockSpec((1, tm, LRU_WIDTH), tok),
                   pl.BlockSpec((1, tm, LRU_WIDTH), tok),
                   pl.BlockSpec((1, tm, QK_WIDTH), tok),
                   pl.BlockSpec((1, tm, QK_WIDTH), tok),
                   pl.BlockSpec((1, tm, ATT_WIDTH), tok)],
        out_shape=out_shapes,
        compiler_params=_cparams(("parallel", "parallel")),
        name="inproj_rope" if use_rope else "inproj_ctx",
    )(x, shift, scale, g1, w_bf, qg, kg, cos_t, sin_t, ones_bd)


LRU_HALF = LRU_WIDTH // 2
LRU_CHUNK = 512


def _to_strands(x):
    rows, width = x.shape
    return jnp.swapaxes(x.reshape(rows // SUBLANES, SUBLANES, width), 0, 1).reshape(rows, width)


def _from_strands(x):
    rows, width = x.shape
    return jnp.swapaxes(x.reshape(SUBLANES, rows // SUBLANES, width), 0, 1).reshape(rows, width)


def _planes(x):
    g = x.shape[0] // SUBLANES
    return [x[s * g:(s + 1) * g] for s in range(SUBLANES)]


def _shift_down(p, first):
    row = lax.broadcasted_iota(jnp.int32, p.shape, 0)
    return jnp.where(row == 0, first, pltpu.roll(p, 1, 0))


def _shift_up(p, last):
    row = lax.broadcasted_iota(jnp.int32, p.shape, 0)
    return jnp.where(row == p.shape[0] - 1, last, pltpu.roll(p, p.shape[0] - 1, 0))


def _conv_strands(x_ref, t0, rows, total, cw, cb):
    p = _planes(_to_strands(x_ref[0, pl.ds(t0, rows), :]))
    prev_start = pl.multiple_of(jnp.maximum(t0 - SUBLANES, 0), SUBLANES)
    next_start = pl.multiple_of(jnp.minimum(t0 + rows, total - SUBLANES), SUBLANES)
    prev = jnp.where(t0 > 0, x_ref[0, pl.ds(prev_start, SUBLANES), :], 0.0)
    nxt = jnp.where(t0 + rows < total, x_ref[0, pl.ds(next_start, SUBLANES), :], 0.0)
    before = _shift_down(p[SUBLANES - 1], prev[SUBLANES - 1:SUBLANES])
    after0 = _shift_up(p[0], nxt[0:1])
    after1 = _shift_up(p[1], nxt[1:2])
    ext = [before] + p + [after0, after1]
    out = []
    for s in range(SUBLANES):
        acc = cb
        for k in range(CONV_W):
            acc = acc + ext[s + k] * cw[k:k + 1]
        out.append(acc)
    return jnp.concatenate(out, axis=0)


def _sigmoid(x):
    return 0.5 * jnp.tanh(0.5 * x) + 0.5


def _lru_gates(xc, w, bias, sp):
    pre = _dot(xc.astype(BF16), w) + bias
    r = _sigmoid(pre[:, :LRU_HALF])
    i = _sigmoid(pre[:, LRU_HALF:])
    log_a = (-LRU_C * r) * sp
    a = jnp.exp(log_a)
    m = -jnp.tanh(log_a) * (a * a + 1.0)
    mult = jnp.where(m > 0.0, m * lax.rsqrt(m), 0.0)
    return a, mult, i * xc


def _scan_chunk(a, u, h, reverse):
    rows, width = a.shape
    groups = rows // SUBLANES
    a = a.reshape(groups, SUBLANES, width)
    u = u.reshape(groups, SUBLANES, width)
    sub = lax.broadcasted_iota(jnp.int32, a.shape, 1)
    s = 1
    while s < SUBLANES:
        if reverse:
            m = sub < SUBLANES - s
            a_sh = pltpu.roll(a, SUBLANES - s, 1)
            u_sh = pltpu.roll(u, SUBLANES - s, 1)
        else:
            m = sub >= s
            a_sh = pltpu.roll(a, s, 1)
            u_sh = pltpu.roll(u, s, 1)
        u = jnp.where(m, a * u_sh + u, u)
        a = jnp.where(m, a * a_sh, a)
        s *= 2
    outs = [None] * groups
    order = range(groups - 1, -1, -1) if reverse else range(groups)
    for g in order:
        hg = a[g] * h + u[g]
        h = hg[0:1] if reverse else hg[SUBLANES - 1:SUBLANES]
        outs[g] = hg
    return jnp.concatenate(outs, axis=0), h


def _scan_strands(a, u, h, reverse):
    a, u = _planes(a), _planes(u)
    order = list(range(SUBLANES - 1, -1, -1)) if reverse else list(range(SUBLANES))
    for prev_s, s in zip(order[:-1], order[1:]):
        u[s] = a[s] * u[prev_s] + u[s]
        a[s] = a[s] * a[prev_s]
    last = order[-1]
    ends, h_out = _scan_chunk(a[last], u[last], h, reverse)
    carry_in = _shift_up(ends, h) if reverse else _shift_down(ends, h)
    return jnp.concatenate([a[s] * carry_in + u[s] for s in range(SUBLANES)], axis=0), h_out


def _rglru_kernel(xl_ref, xc_ref, gate_ref, cw_ref, cb_ref, wf_ref, wb_ref, bf_ref, bb_ref, lam_ref,
                  o_ref, hf_ref, cl_ref, cc_ref, *, n, ctx_len):
    cw = cw_ref[...]
    cb = cb_ref[...]
    tc = min(LRU_CHUNK, ctx_len)
    tl = min(LRU_CHUNK, n)

    def direction(d, w_ref, b_ref):
        reverse = d == 1
        w = w_ref[0]
        bias = b_ref[0]
        z = -lam_ref[0, pl.ds(d, 1), :]
        sp = jnp.maximum(z, 0.0) + jnp.log1p(jnp.exp(-jnp.abs(z)))
        first_row = ctx_len - 1 if reverse else 0

        def conv(src_ref, cache_ref, t0, rows, total):
            if reverse:
                return cache_ref[pl.ds(t0, rows), :]
            xc = _conv_strands(src_ref, t0, rows, total, cw, cb)
            cache_ref[pl.ds(t0, rows), :] = xc
            return xc

        def ctx_step(c, h):
            cc = (ctx_len // tc - 1 - c) if reverse else c
            t0 = pl.multiple_of(cc * tc, SUBLANES)
            xc = conv(xc_ref, cc_ref, t0, tc, ctx_len)
            a, mult, ix = _lru_gates(xc, w, bias, sp)
            r = lax.broadcasted_iota(jnp.int32, a.shape, 0)
            groups = tc // SUBLANES
            time = (r % groups) * SUBLANES + r // groups + t0
            mult = jnp.where(time == first_row, 1.0, mult)
            _, h = _scan_strands(a, mult * ix, h, reverse)
            return h

        h = lax.fori_loop(0, ctx_len // tc, ctx_step, jnp.zeros((1, LRU_HALF), F32))

        def lat_step(c, h):
            cc = (n // tl - 1 - c) if reverse else c
            t0 = pl.multiple_of(cc * tl, SUBLANES)
            xc = conv(xl_ref, cl_ref, t0, tl, n)
            a, mult, ix = _lru_gates(xc, w, bias, sp)
            hs, h = _scan_strands(a, mult * ix, h, reverse)
            if reverse:
                y = _from_strands(hf_ref[pl.ds(t0, tl), :] + hs) * jax.nn.gelu(gate_ref[0, pl.ds(t0, tl), :])
                o_ref[0, pl.ds(t0, tl), :] = y.astype(BF16)
            else:
                hf_ref[pl.ds(t0, tl), :] = hs
            return h

        lax.fori_loop(0, n // tl, lat_step, h)

    direction(0, wf_ref, bf_ref)
    direction(1, wb_ref, bb_ref)


def _rglru(xl, xc, gate, conv_w, conv_b, w_f, w_b, b_f, b_b, lam):
    bsz, n, _ = xl.shape
    ctx_len = xc.shape[1]
    halves = LRU_WIDTH // LRU_HALF
    tokh = lambda b, hf: (b, 0, hf)
    return pl.pallas_call(
        functools.partial(_rglru_kernel, n=n, ctx_len=ctx_len),
        grid=(bsz, halves),
        in_specs=[pl.BlockSpec((1, n, LRU_HALF), tokh),
                  pl.BlockSpec((1, ctx_len, LRU_HALF), tokh),
                  pl.BlockSpec((1, n, LRU_HALF), tokh),
                  pl.BlockSpec((CONV_W, LRU_HALF), lambda b, hf: (0, hf)),
                  pl.BlockSpec((1, LRU_HALF), lambda b, hf: (0, hf)),
                  pl.BlockSpec((1, LRU_HALF, 2 * LRU_HALF), lambda b, hf: (hf, 0, 0)),
                  pl.BlockSpec((1, LRU_HALF, 2 * LRU_HALF), lambda b, hf: (hf, 0, 0)),
                  pl.BlockSpec((1, 1, 2 * LRU_HALF), lambda b, hf: (hf, 0, 0)),
                  pl.BlockSpec((1, 1, 2 * LRU_HALF), lambda b, hf: (hf, 0, 0)),
                  pl.BlockSpec((1, 2, LRU_HALF), lambda b, hf: (hf, 0, 0))],
        out_specs=pl.BlockSpec((1, n, LRU_HALF), tokh),
        out_shape=jax.ShapeDtypeStruct((bsz, n, LRU_WIDTH), BF16),
        scratch_shapes=[pltpu.VMEM((n, LRU_HALF), F32),
                        pltpu.VMEM((n, LRU_HALF), F32),
                        pltpu.VMEM((ctx_len, LRU_HALF), F32)],
        compiler_params=_cparams(("parallel", "parallel")),
        name="rglru",
    )(xl, xc, gate, conv_w, conv_b, w_f, w_b, b_f, b_b, lam)


def _lru_gate_weights(wa, ba, wi, bi):
    halves = LRU_WIDTH // LRU_HALF
    per = LRU_BLOCKS // halves
    bw = LRU_WIDTH // LRU_BLOCKS

    def dense(w):
        w = w.reshape(halves, per, bw, bw)
        eye = jnp.eye(per, dtype=w.dtype)
        return jnp.einsum("hpij,pq->hpiqj", w, eye).reshape(halves, per * bw, per * bw)

    w_cat = jnp.concatenate([dense(wa), dense(wi)], axis=-1).astype(BF16)
    b_cat = jnp.concatenate([ba.reshape(halves, 1, LRU_HALF), bi.reshape(halves, 1, LRU_HALF)], axis=-1)
    return w_cat, b_cat


ATT_TQ = 256
ATT_TK = 256


def _attn_kernel(q_ref, kc_ref, vc_ref, kl_ref, vl_ref, lq1_ref, lk1_ref, lq2_ref, lk2_ref, sg_ref, o_ref,
                 *, n, tk):
    q = q_ref[0]
    tq = q.shape[0]
    lane = lax.broadcasted_iota(jnp.int32, q.shape, 1)
    zero = jnp.zeros_like(q)
    q2 = jnp.concatenate([jnp.where(lane < HEAD_DIM, q, zero), jnp.where(lane >= HEAD_DIM, q, zero)], axis=0)

    def ext(v):
        ln = lax.broadcasted_iota(jnp.int32, v.shape, 1)
        return jnp.concatenate([v, jnp.where(ln == 0, 1.0, 0.0).astype(BF16)], axis=1)

    def step(k, v, carry):
        m, acc = carry
        s = _dot_nt(q2, k)
        m_new = jnp.maximum(m, jnp.max(s, axis=1, keepdims=True))
        p = jnp.exp2(s - m_new).astype(BF16)
        acc = jnp.exp2(m - m_new) * acc + _dot(p, ext(v))
        return m_new, acc

    carry = (jnp.full((2 * tq, 1), -1e30, F32), jnp.zeros((2 * tq, 2 * V_DIM), F32))
    carry = step(kc_ref[0], vc_ref[0], carry)
    for j in range(n // tk):
        carry = step(kl_ref[0, j * tk:(j + 1) * tk, :], vl_ref[0, j * tk:(j + 1) * tk, :], carry)
    _, acc = carry
    o = acc[:, :V_DIM] / acc[:, V_DIM:V_DIM + 1]
    lam = (jnp.exp(jnp.sum(lq1_ref[...] * lk1_ref[...], keepdims=True))
           - jnp.exp(jnp.sum(lq2_ref[...] * lk2_ref[...], keepdims=True)) + LAM_INIT)
    att = o[:tq] - lam * o[tq:]
    ms = jnp.mean(att * att, axis=-1, keepdims=True)
    y = att * lax.rsqrt(ms + EPS) * sg_ref[...]
    o_ref[0] = (y * (1.0 - LAM_INIT)).astype(BF16)


def _attention(q, kc, vc, kl, vl, lq1, lk1, lq2, lk2, sg):
    bsz, n, _ = q.shape
    ctx_len = kc.shape[1]
    tq = min(ATT_TQ, n)
    tk = min(ATT_TK, n)
    vec = lambda b, h, i: (0, 0)
    return pl.pallas_call(
        functools.partial(_attn_kernel, n=n, tk=tk),
        grid=(bsz, ATT_HEADS, n // tq),
        in_specs=[pl.BlockSpec((1, tq, V_DIM), lambda b, h, i: (b, i, h)),
                  pl.BlockSpec((1, ctx_len, V_DIM), lambda b, h, i: (b, 0, h)),
                  pl.BlockSpec((1, ctx_len, V_DIM), lambda b, h, i: (b, 0, h)),
                  pl.BlockSpec((1, n, V_DIM), lambda b, h, i: (b, 0, h)),
                  pl.BlockSpec((1, n, V_DIM), lambda b, h, i: (b, 0, h)),
                  pl.BlockSpec((1, HEAD_DIM), vec),
                  pl.BlockSpec((1, HEAD_DIM), vec),
                  pl.BlockSpec((1, HEAD_DIM), vec),
                  pl.BlockSpec((1, HEAD_DIM), vec),
                  pl.BlockSpec((1, V_DIM), vec)],
        out_specs=pl.BlockSpec((1, tq, V_DIM), lambda b, h, i: (b, i, h)),
        out_shape=jax.ShapeDtypeStruct((bsz, n, ATT_WIDTH), BF16),
        compiler_params=_cparams(("parallel", "parallel", "arbitrary")),
        name="diff_attention",
    )(q, kc, vc, kl, vl, lq1, lk1, lq2, lk2, sg)


def _outproj_kernel(lru_ref, att_ref, x_ref, g1_ref, shift_ref, scale_ref, n2_ref, wo_ref, wr_ref,
                    x1_ref, h2_ref, lg_ref):
    mix = _dot(lru_ref[0], wo_ref[:LRU_WIDTH, :]) + _dot(att_ref[0], wo_ref[LRU_WIDTH:, :])
    x1 = x_ref[0] + g1_ref[0] * mix
    x1_ref[0] = x1
    ms = jnp.mean(x1 * x1, axis=-1, keepdims=True)
    h2 = x1 * lax.rsqrt(ms + EPS) * n2_ref[...]
    h2 = h2 * (1.0 + scale_ref[0]) + shift_ref[0]
    h2_ref[0] = h2
    lg_ref[0] = _dot_nt(wr_ref[...], h2.astype(BF16))


def _outproj(lru, att, x, g1, shift2, scale2, n2g, wo_bf, wr_bf):
    bsz, n, d = x.shape
    tm = min(TOKEN_TILE, n)
    tok = lambda b, i: (b, i, 0)
    mod = lambda b, i: (b, 0, 0)
    full = lambda b, i: (0, 0)
    return pl.pallas_call(
        _outproj_kernel,
        grid=(bsz, n // tm),
        in_specs=[pl.BlockSpec((1, tm, LRU_WIDTH), tok),
                  pl.BlockSpec((1, tm, ATT_WIDTH), tok),
                  pl.BlockSpec((1, tm, d), tok),
                  pl.BlockSpec((1, 1, d), mod),
                  pl.BlockSpec((1, 1, d), mod),
                  pl.BlockSpec((1, 1, d), mod),
                  pl.BlockSpec((1, d), full),
                  pl.BlockSpec((d, d), full),
                  pl.BlockSpec((N_EXPERTS, d), full)],
        out_specs=[pl.BlockSpec((1, tm, d), tok),
                   pl.BlockSpec((1, tm, d), tok),
                   pl.BlockSpec((1, N_EXPERTS, tm), lambda b, i: (b, 0, i))],
        out_shape=(jax.ShapeDtypeStruct((bsz, n, d), F32),
                   jax.ShapeDtypeStruct((bsz, n, d), F32),
                   jax.ShapeDtypeStruct((bsz, N_EXPERTS, n), F32)),
        compiler_params=_cparams(("parallel", "parallel")),
        name="outproj_router",
    )(lru, att, x, g1, shift2, scale2, n2g, wo_bf, wr_bf)


TOPK_UNROLL = 4


def _topk_kernel(lg_ref, idx_ref, g_ref, aff_ref, rank_ref, start_ref, tot_ref, *, cap):
    lg = lg_ref[0]
    ne, nb, blk = lg.shape
    ex = jnp.exp(lg - jnp.max(lg, axis=0, keepdims=True))
    aff = ex / jnp.sum(ex, axis=0, keepdims=True)
    aff_ref[...] = aff
    capf = float(cap)

    def bisect(_, c):
        lo, hi = c
        mid = lo + ((hi - lo) >> 1)
        cnt = jnp.sum(jnp.where(aff >= pltpu.bitcast(mid, F32), 1.0, 0.0), axis=(1, 2), keepdims=True)
        ge = cnt >= capf
        return jnp.where(ge, mid, lo), jnp.where(ge, hi, mid)

    lo0 = jnp.zeros((ne, 1, 1), jnp.int32)
    hi0 = jnp.full((ne, 1, 1), 0x3F800001, jnp.int32)
    thr, nxt = lax.fori_loop(0, 31, bisect, (lo0, hi0))
    gt = aff >= pltpu.bitcast(nxt, F32)
    gtf = jnp.where(gt, 1.0, 0.0)
    eqf = jnp.where(jnp.logical_and(aff >= pltpu.bitcast(thr, F32), jnp.logical_not(gt)), 1.0, 0.0)
    need = capf - jnp.sum(gtf, axis=(1, 2), keepdims=True)

    rows = ne * nb
    tri = (lax.broadcasted_iota(jnp.int32, (blk, blk), 0)
           <= lax.broadcasted_iota(jnp.int32, (blk, blk), 1)).astype(BF16)
    r_i = lax.broadcasted_iota(jnp.int32, (rows, rows), 0)
    c_i = lax.broadcasted_iota(jnp.int32, (rows, rows), 1)
    earlier = jnp.logical_and(c_i < r_i, c_i >= (r_i // nb) * nb).astype(BF16)

    def prefix(x3):
        cin = _dot(x3.reshape(rows, blk).astype(BF16), tri)
        tot = jnp.broadcast_to(cin[:, blk - 1:blk], cin.shape)
        start = _dot(earlier, tot.astype(BF16))
        return cin.reshape(x3.shape), start.reshape(x3.shape), tot.reshape(x3.shape)

    cin, start, _ = prefix(eqf)
    sel = gtf + eqf * jnp.where(cin + start <= need, 1.0, 0.0)
    cin, start, tot = prefix(sel)
    rank_ref[...] = cin * sel
    start_ref[...] = start
    tot_ref[...] = tot

    slot = (lax.broadcasted_iota(jnp.int32, (1, cap), 1) + 1).astype(F32)
    row_id = lax.broadcasted_iota(jnp.int32, (nb, 1), 0).astype(F32)
    lane_id = lax.broadcasted_iota(jnp.int32, (blk, 1), 0).astype(F32)

    def per_expert(e):
        a = aff_ref[e]
        st = start_ref[e][:, 0:1]
        in_row = jnp.logical_and(st < slot, slot <= st + tot_ref[e][:, 0:1])
        in_row_bf = jnp.where(in_row, 1.0, 0.0).astype(BF16)
        row_of = jnp.sum(jnp.where(in_row, row_id, 0.0), axis=0, keepdims=True)
        rank_need = slot - jnp.sum(jnp.where(in_row, st, 0.0), axis=0, keepdims=True)

        def pick(x_bf):
            return lax.dot_general(x_bf, in_row_bf, (((0,), (0,)), ((), ())), preferred_element_type=F32)

        hit = pick(rank_ref[e].astype(BF16)) == rank_need
        lane_of = jnp.sum(jnp.where(hit, lane_id, 0.0), axis=0, keepdims=True)
        a_hi = a.astype(BF16)
        a_mid = (a - a_hi.astype(F32)).astype(BF16)
        a_lo = (a - a_hi.astype(F32) - a_mid.astype(F32)).astype(BF16)
        a_sel = (pick(a_hi) + pick(a_mid)) + pick(a_lo)
        idx_ref[0, e] = (row_of * float(blk) + lane_of).astype(jnp.int32)
        g_ref[0, e] = jnp.sum(jnp.where(hit, a_sel, 0.0), axis=0, keepdims=True)

    def expert_group(gi, carry):
        for r in range(TOPK_UNROLL):
            per_expert(gi * TOPK_UNROLL + r)
        return carry

    lax.fori_loop(0, ne // TOPK_UNROLL, expert_group, 0)


def _topk(logits_t, cap):
    bsz, _, n = logits_t.shape
    blk = min(LANES, n)
    nb = n // blk
    slab = pltpu.VMEM((N_EXPERTS, nb, blk), F32)
    return pl.pallas_call(
        functools.partial(_topk_kernel, cap=cap),
        grid=(bsz,),
        in_specs=[pl.BlockSpec((1, N_EXPERTS, nb, blk), lambda b: (b, 0, 0, 0))],
        out_specs=[pl.BlockSpec((1, N_EXPERTS, 1, cap), lambda b: (b, 0, 0, 0)),
                   pl.BlockSpec((1, N_EXPERTS, 1, cap), lambda b: (b, 0, 0, 0))],
        out_shape=(jax.ShapeDtypeStruct((bsz, N_EXPERTS, 1, cap), jnp.int32),
                   jax.ShapeDtypeStruct((bsz, N_EXPERTS, 1, cap), F32)),
        scratch_shapes=[slab, slab, slab, slab],
        compiler_params=_cparams(("parallel",)),
        name="expert_topk",
    )(logits_t.reshape(bsz, N_EXPERTS, nb, blk))


GATHER_UNROLL = 8


def _gather_kernel(idx_ref, h_ref, o_ref, *, cap):
    unroll = GATHER_UNROLL if cap % (SUBLANES * GATHER_UNROLL) == 0 else 1

    def body(jj, carry):
        for u in range(unroll):
            j0 = pl.multiple_of((jj * unroll + u) * SUBLANES, SUBLANES)
            rows = [h_ref[0, pl.ds(idx_ref[0, 0, j0 + r], 1), :] for r in range(SUBLANES)]
            o_ref[0, 0, pl.ds(j0, SUBLANES), :] = jnp.concatenate(rows, axis=0).astype(BF16)
        return carry

    lax.fori_loop(0, cap // (SUBLANES * unroll), body, 0)


def _gather(idx3, h2, cap):
    bsz, n, d = h2.shape
    return pl.pallas_call(
        functools.partial(_gather_kernel, cap=cap),
        grid=(bsz, N_EXPERTS),
        in_specs=[pl.BlockSpec((1, 1, cap), lambda b, e: (b * N_EXPERTS + e, 0, 0), memory_space=pltpu.SMEM),
                  pl.BlockSpec((1, n, d), lambda b, e: (b, 0, 0))],
        out_specs=pl.BlockSpec((1, 1, cap, d), lambda b, e: (e, b, 0, 0)),
        out_shape=jax.ShapeDtypeStruct((N_EXPERTS, bsz, cap, d), BF16),
        compiler_params=_cparams(("parallel", "arbitrary")),
        name="expert_gather",
    )(idx3, h2)


FFN_TF = 256


FFN_SUB = 512


def _ffn_kernel(x_ref, wg_ref, wu_ref, wd_ref, g_ref, gate_ref, o_ref, *, cap):
    i = pl.program_id(1)
    f = pl.program_id(2)
    nf = pl.num_programs(2)
    tm = x_ref.shape[1]
    sub = min(FFN_SUB, tm, cap)

    def body(first, last):
        wgu = jnp.concatenate([wg_ref[0].astype(BF16), wu_ref[0].astype(BF16)], axis=1)
        wd = wd_ref[0].astype(BF16)
        tf = wd.shape[0]
        for r in range(tm // sub):
            rows = slice(r * sub, (r + 1) * sub)
            gu = _dot(x_ref[0, rows, :], wgu)
            hg = gu[:, :tf]
            h = (hg * jax.nn.sigmoid(hg)) * gu[:, tf:]
            y = _dot(h.astype(BF16), wd)
            if not first:
                y = o_ref[0, rows, :] + y
            if last:
                sample = (i * tm + r * sub) // cap
                g_col = jnp.transpose(jnp.broadcast_to(g_ref[0, :, rows], (LANES, sub)))[:, 0:1]
                y = y * (g_col * gate_ref[sample])
            o_ref[0, rows, :] = y

    @pl.when(f == 0)
    def _():
        body(True, False)

    @pl.when(jnp.logical_and(f > 0, f < nf - 1))
    def _():
        body(False, False)

    @pl.when(f == nf - 1)
    def _():
        body(False, True)


def _ffn(xe, w_gate, w_up, w_down, g_row, gate2, cap):
    ne, m, d = xe.shape
    dff = w_gate.shape[2]
    tm = min(2048, m)
    tf = FFN_TF
    assert dff // tf > 1 and cap % min(FFN_SUB, tm, cap) == 0
    return pl.pallas_call(
        functools.partial(_ffn_kernel, cap=cap),
        grid=(ne, m // tm, dff // tf),
        in_specs=[pl.BlockSpec((1, tm, d), lambda e, i, f: (e, i, 0)),
                  pl.BlockSpec((1, d, tf), lambda e, i, f: (e, 0, f)),
                  pl.BlockSpec((1, d, tf), lambda e, i, f: (e, 0, f)),
                  pl.BlockSpec((1, tf, d), lambda e, i, f: (e, f, 0)),
                  pl.BlockSpec((1, 1, tm), lambda e, i, f: (e, 0, i)),
                  pl.BlockSpec(gate2.shape, lambda e, i, f: (0, 0, 0))],
        out_specs=pl.BlockSpec((1, tm, d), lambda e, i, f: (e, i, 0)),
        out_shape=jax.ShapeDtypeStruct((ne, m, d), F32),
        compiler_params=_cparams(("parallel", "parallel", "arbitrary")),
        name="expert_ffn",
    )(xe, w_gate, w_up, w_down, g_row, gate2)


COMBINE_UNROLL = 4


def _combine_kernel(idx_ref, x1_hbm, y_ref, o_ref, sem, *, cap):
    b = pl.program_id(0)
    e = pl.program_id(1)

    @pl.when(e == 0)
    def _():
        init = pltpu.make_async_copy(x1_hbm.at[b], o_ref.at[0], sem)
        init.start()
        init.wait()

    sub = lax.broadcasted_iota(jnp.int32, (SUBLANES, o_ref.shape[2]), 0)
    unroll = COMBINE_UNROLL if cap % (SUBLANES * COMBINE_UNROLL) == 0 else 1

    def body(jj, carry):
        for u in range(unroll):
            j0 = pl.multiple_of((jj * unroll + u) * SUBLANES, SUBLANES)
            ys = y_ref[0, 0, pl.ds(j0, SUBLANES), :]
            for r in range(0, SUBLANES, 2):
                t0 = idx_ref[0, 0, j0 + r]
                t1 = idx_ref[0, 0, j0 + r + 1]
                b0 = pl.multiple_of((t0 >> 3) << 3, SUBLANES)
                b1 = pl.multiple_of((t1 >> 3) << 3, SUBLANES)
                tile0 = o_ref[0, pl.ds(b0, SUBLANES), :]
                tile1 = o_ref[0, pl.ds(b1, SUBLANES), :]
                u0 = jnp.where(sub == (t0 & (SUBLANES - 1)), ys[r:r + 1], 0.0)
                u1 = jnp.where(sub == (t1 & (SUBLANES - 1)), ys[r + 1:r + 2], 0.0)
                o_ref[0, pl.ds(b0, SUBLANES), :] = tile0 + u0
                o_ref[0, pl.ds(b1, SUBLANES), :] = tile1 + u1 + jnp.where(b0 == b1, u0, 0.0)
        return carry

    lax.fori_loop(0, cap // (SUBLANES * unroll), body, 0)


def _combine(idx3, x1, ye, cap):
    bsz, n, d = x1.shape
    resident = lambda b, e: (b, 0, 0)
    return pl.pallas_call(
        functools.partial(_combine_kernel, cap=cap),
        grid=(bsz, N_EXPERTS),
        in_specs=[pl.BlockSpec((1, 1, cap), lambda b, e: (b * N_EXPERTS + e, 0, 0), memory_space=pltpu.SMEM),
                  pl.BlockSpec(memory_space=pl.ANY),
                  pl.BlockSpec((1, 1, cap, d), lambda b, e: (e, b, 0, 0))],
        out_specs=pl.BlockSpec((1, n, d), resident, pipeline_mode=pl.Buffered(1)),
        out_shape=jax.ShapeDtypeStruct((bsz, n, d), F32),
        scratch_shapes=[pltpu.SemaphoreType.DMA],
        compiler_params=_cparams(("parallel", "arbitrary")),
        name="expert_combine",
    )(idx3, x1, ye)


def _rope_tables(n):
    rows = n // GRID_W
    row = jnp.repeat(jnp.arange(rows), GRID_W).astype(F32)
    col = jnp.tile(jnp.arange(GRID_W), rows).astype(F32)
    inv = ROPE_BASE ** (-jnp.arange(ROPE_PAIRS, dtype=F32) / ROPE_PAIRS)
    ang_r = row[:, None] * inv
    ang_c = col[:, None] * inv
    cos64 = jnp.concatenate([jnp.cos(ang_r), jnp.cos(ang_r), jnp.cos(ang_c), jnp.cos(ang_c)], axis=1)
    sin64 = jnp.concatenate([-jnp.sin(ang_r), jnp.sin(ang_r), -jnp.sin(ang_c), jnp.sin(ang_c)], axis=1)
    return jnp.tile(cos64, (1, 2)), jnp.tile(sin64, (1, 2))


def kernel(x, c, ctx, c_ctx, w_ada, b_ada, norm1_g, norm2_g, w_in, conv_w, conv_b, lru_wa, lru_ba, lru_wi,
           lru_bi, lru_lambda, q_norm_g, k_norm_g, lambda_q1, lambda_k1, lambda_q2, lambda_k2, subln_g, w_out,
           w_router, w_gate, w_up, w_down):
    assert w_ada.shape[0] == 1, "single-layer configuration"
    bsz, n, d = x.shape
    ctx_len = ctx.shape[1]
    cap = EC_FACTOR * n // N_EXPERTS

    rows = ((bsz + 1 + SUBLANES - 1) // SUBLANES) * SUBLANES
    cvec = jnp.zeros((rows, d), F32).at[:bsz].set(c).at[bsz].set(c_ctx)
    mod = _adaln(cvec, w_ada[0], b_ada[0][None]).reshape(rows, N_MOD, 1, d)
    mod_l = mod[:bsz]
    mod_c = mod[bsz:bsz + 1]

    w_in_bf = w_in[0].astype(BF16)
    w_out_bf = w_out[0].astype(BF16)
    w_r_bf = w_router[0].T.astype(BF16)
    qg = jnp.tile(q_norm_g[0], QK_WIDTH // HEAD_DIM)[None]
    kg = jnp.tile(k_norm_g[0], QK_WIDTH // HEAD_DIM)[None]
    seg = jnp.arange(MXU_DIM) // HEAD_DIM
    ones_bd = (seg[:, None] == seg[None, :]).astype(BF16)
    cos_t, sin_t = _rope_tables(n)
    cos_c = jnp.ones((ctx_len, LANES), F32)
    sin_c = jnp.zeros((ctx_len, LANES), F32)
    g1 = norm1_g[0][None]

    xl, gate_l, q_l, k_l, v_l = _inproj(x, mod_l[:, 0], mod_l[:, 1], g1, w_in_bf, qg, kg, cos_t, sin_t,
                                        ones_bd, True)
    xc, _, _, k_c, v_c = _inproj(ctx, mod_c[:, 0], mod_c[:, 1], g1, w_in_bf, qg, kg, cos_c, sin_c,
                                 ones_bd, False)

    w_f, b_f = _lru_gate_weights(lru_wa[0, 0], lru_ba[0, 0], lru_wi[0, 0], lru_bi[0, 0])
    w_b, b_b = _lru_gate_weights(lru_wa[0, 1], lru_ba[0, 1], lru_wi[0, 1], lru_bi[0, 1])
    halves = LRU_WIDTH // LRU_HALF
    lam_h = lru_lambda[0].reshape(2, halves, LRU_HALF).transpose(1, 0, 2)
    lru = _rglru(xl, xc, gate_l, conv_w[0], conv_b[0][None], w_f, w_b, b_f, b_b, lam_h)

    att = _attention(q_l, k_c, v_c, k_l, v_l, lambda_q1, lambda_k1, lambda_q2, lambda_k2, subln_g)

    x1, h2, logits = _outproj(lru, att, x, mod_l[:, 2], mod_l[:, 3], mod_l[:, 4], norm2_g[0][None],
                              w_out_bf, w_r_bf)

    idx, gates = _topk(logits, cap)
    idx3 = idx.reshape(bsz * N_EXPERTS, 1, cap)
    g_row = jnp.swapaxes(gates.reshape(bsz, N_EXPERTS, cap), 0, 1).reshape(N_EXPERTS, 1, bsz * cap)
    xe = _gather(idx3, h2, cap)
    ye = _ffn(xe.reshape(N_EXPERTS, bsz * cap, d), w_gate[0], w_up[0], w_down[0], g_row, mod_l[:, 5], cap)
    return _combine(idx3, x1, ye.reshape(N_EXPERTS, bsz, cap, d), cap)
```

```python
import functools

import jax
import jax.numpy as jnp
from jax import lax
from jax.experimental import pallas as pl
from jax.experimental.pallas import tpu as pltpu

F32 = jnp.float32
BF16 = jnp.bfloat16

EPS = 1e-6
GRID_W = 64
LRU_WIDTH = 512
LRU_BLOCKS = 8
LRU_C = 8.0
CONV_W = 4
ATT_HEADS = 4
HEAD_DIM = 64
V_DIM = 2 * HEAD_DIM
QK_WIDTH = ATT_HEADS * 2 * HEAD_DIM
ATT_WIDTH = ATT_HEADS * V_DIM
ROPE_PAIRS = HEAD_DIM // 4
ROPE_BASE = 10000.0
N_EXPERTS = 16
EC_FACTOR = 2
N_MOD = 6
LAM_INIT = 0.2
LOG2_E = 1.4426950408889634

LANES = 128
SUBLANES = 8
MXU_DIM = 256
VMEM_LIMIT = 56 * 1024 * 1024
TOKEN_TILE = 1024


def _cparams(sem):
    return pltpu.CompilerParams(dimension_semantics=sem, vmem_limit_bytes=VMEM_LIMIT)


def _dot(a, b):
    return jnp.dot(a, b, preferred_element_type=F32)


def _dot_nt(a, b):
    return lax.dot_general(a, b, (((1,), (1,)), ((), ())), preferred_element_type=F32)


def _split_bf16(x):
    hi = x.astype(BF16)
    lo = (x - hi.astype(F32)).astype(BF16)
    return hi, lo


def _adaln_kernel(c_ref, w_ref, b_ref, o_ref):
    c = c_ref[...]
    s = c * jax.nn.sigmoid(c)
    s_hi, s_lo = _split_bf16(s)
    w_hi, w_lo = _split_bf16(w_ref[...])
    o_ref[...] = _dot(s_hi, w_hi) + _dot(s_hi, w_lo) + _dot(s_lo, w_hi) + b_ref[...]


def _adaln(cvec, w, b):
    rows, d = cvec.shape
    cols = w.shape[1]
    tn = cols // 4
    return pl.pallas_call(
        _adaln_kernel,
        grid=(cols // tn,),
        in_specs=[pl.BlockSpec((rows, d), lambda j: (0, 0)),
                  pl.BlockSpec((d, tn), lambda j: (0, j)),
                  pl.BlockSpec((1, tn), lambda j: (0, j))],
        out_specs=pl.BlockSpec((rows, tn), lambda j: (0, j)),
        out_shape=jax.ShapeDtypeStruct((rows, cols), F32),
        compiler_params=_cparams(("arbitrary",)),
        name="adaln",
    )(cvec, w, b)


def _swap_halves16(x):
    lane = lax.broadcasted_iota(jnp.int32, x.shape, 1)
    first = (lane % 32) < 16
    return jnp.where(first, pltpu.roll(x, LANES - 16, 1), pltpu.roll(x, 16, 1))


def _qk_norm(t, g, ones_bd):
    outs = []
    for c in range(t.shape[1] // MXU_DIM):
        tc = t[:, c * MXU_DIM:(c + 1) * MXU_DIM]
        hi, lo = _split_bf16(tc * tc)
        ssum = _dot(hi, ones_bd) + _dot(lo, ones_bd)
        outs.append(tc * lax.rsqrt(ssum * (1.0 / HEAD_DIM) + EPS) * g[:, c * MXU_DIM:(c + 1) * MXU_DIM])
    return outs


def _inproj_kernel(x_ref, shift_ref, scale_ref, g1_ref, w_ref, qg_ref, kg_ref, cos_ref, sin_ref, ones_ref,
                   xl_ref, gate_ref, q_ref, k_ref, v_ref, *, use_rope):
    x = x_ref[0]
    ms = jnp.mean(x * x, axis=-1, keepdims=True)
    h = x * lax.rsqrt(ms + EPS) * g1_ref[...]
    h = h * (1.0 + scale_ref[0]) + shift_ref[0]
    p = _dot(h.astype(BF16), w_ref[...])
    o1, o2, o3, o4 = LRU_WIDTH, 2 * LRU_WIDTH, 2 * LRU_WIDTH + QK_WIDTH, 2 * LRU_WIDTH + 2 * QK_WIDTH
    xl_ref[0] = p[:, :o1]
    gate_ref[0] = p[:, o1:o2]
    v_ref[0] = p[:, o4:].astype(BF16)
    ones_bd = ones_ref[...]
    qn = _qk_norm(p[:, o2:o3], qg_ref[...], ones_bd)
    kn = _qk_norm(p[:, o3:o4], kg_ref[...], ones_bd)
    scale = HEAD_DIM ** -0.5 * LOG2_E
    for src, dst, mul in ((qn, q_ref, scale), (kn, k_ref, 1.0)):
        for c, tc in enumerate(src):
            for hh in range(MXU_DIM // LANES):
                th = tc[:, hh * LANES:(hh + 1) * LANES]
                if use_rope:
                    th = th * cos_ref[...] + _swap_halves16(th) * sin_ref[...]
                col = c * MXU_DIM + hh * LANES
                dst[0, :, col:col + LANES] = (th * mul).astype(BF16)


def _inproj(x, shift, scale, g1, w_bf, qg, kg, cos_t, sin_t, ones_bd, use_rope):
    bsz, n, d = x.shape
    tm = min(TOKEN_TILE, n)
    wid = w_bf.shape[1]
    per_b = shift.shape[0] > 1
    mod_map = (lambda b, i: (b, 0, 0)) if per_b else (lambda b, i: (0, 0, 0))
    full = lambda b, i: (0, 0)
    tok = lambda b, i: (b, i, 0)
    out_shapes = (jax.ShapeDtypeStruct((bsz, n, LRU_WIDTH), F32),
                  jax.ShapeDtypeStruct((bsz, n, LRU_WIDTH), F32),
                  jax.ShapeDtypeStruct((bsz, n, QK_WIDTH), BF16),
                  jax.ShapeDtypeStruct((bsz, n, QK_WIDTH), BF16),
                  jax.ShapeDtypeStruct((bsz, n, ATT_WIDTH), BF16))
    return pl.pallas_call(
        functools.partial(_inproj_kernel, use_rope=use_rope),
        grid=(bsz, n // tm),
        in_specs=[pl.BlockSpec((1, tm, d), tok),
                  pl.BlockSpec((1, 1, d), mod_map),
                  pl.BlockSpec((1, 1, d), mod_map),
                  pl.BlockSpec((1, d), full),
                  pl.BlockSpec((d, wid), full),
                  pl.BlockSpec((1, QK_WIDTH), full),
                  pl.BlockSpec((1, QK_WIDTH), full),
                  pl.BlockSpec((tm, LANES), lambda b, i: (i, 0)),
                  pl.BlockSpec((tm, LANES), lambda b, i: (i, 0)),
                  pl.BlockSpec((MXU_DIM, MXU_DIM), full)],
        out_specs=[pl.BlockSpec((1, tm, LRU_WIDTH), tok),
                   pl.BlockSpec((1, tm, LRU_WIDTH), tok),
                   pl.BlockSpec((1, tm, QK_WIDTH), tok),
                   pl.BlockSpec((1, tm, QK_WIDTH), tok),
                   pl.BlockSpec((1, tm, ATT_WIDTH), tok)],
        out_shape=out_shapes,
        compiler_params=_cparams(("parallel", "parallel")),
        name="inproj_rope" if use_rope else "inproj_ctx",
    )(x, shift, scale, g1, w_bf, qg, kg, cos_t, sin_t, ones_bd)


LRU_HALF = LRU_WIDTH // 2
LRU_CHUNK = 512


def _to_strands(x):
    rows, width = x.shape
    return jnp.swapaxes(x.reshape(rows // SUBLANES, SUBLANES, width), 0, 1).reshape(rows, width)


def _from_strands(x):
    rows, width = x.shape
    return jnp.swapaxes(x.reshape(SUBLANES, rows // SUBLANES, width), 0, 1).reshape(rows, width)


def _planes(x):
    g = x.shape[0] // SUBLANES
    return [x[s * g:(s + 1) * g] for s in range(SUBLANES)]


def _shift_down(p, first):
    row = lax.broadcasted_iota(jnp.int32, p.shape, 0)
    return jnp.where(row == 0, first, pltpu.roll(p, 1, 0))


def _shift_up(p, last):
    row = lax.broadcasted_iota(jnp.int32, p.shape, 0)
    return jnp.where(row == p.shape[0] - 1, last, pltpu.roll(p, p.shape[0] - 1, 0))


def _conv_strands(x_ref, t0, rows, total, cw, cb):
    p = _planes(_to_strands(x_ref[0, pl.ds(t0, rows), :]))
    prev_start = pl.multiple_of(jnp.maximum(t0 - SUBLANES, 0), SUBLANES)
    next_start = pl.multiple_of(jnp.minimum(t0 + rows, total - SUBLANES), SUBLANES)
    prev = jnp.where(t0 > 0, x_ref[0, pl.ds(prev_start, SUBLANES), :], 0.0)
    nxt = jnp.where(t0 + rows < total, x_ref[0, pl.ds(next_start, SUBLANES), :], 0.0)
    before = _shift_down(p[SUBLANES - 1], prev[SUBLANES - 1:SUBLANES])
    after0 = _shift_up(p[0], nxt[0:1])
    after1 = _shift_up(p[1], nxt[1:2])
    ext = [before] + p + [after0, after1]
    out = []
    for s in range(SUBLANES):
        acc = cb
        for k in range(CONV_W):
            acc = acc + ext[s + k] * cw[k:k + 1]
        out.append(acc)
    return jnp.concatenate(out, axis=0)


def _sigmoid(x):
    return 0.5 * jnp.tanh(0.5 * x) + 0.5


def _lru_gates(xc, w, bias, sp):
    pre = _dot(xc.astype(BF16), w) + bias
    r = _sigmoid(pre[:, :LRU_HALF])
    i = _sigmoid(pre[:, LRU_HALF:])
    log_a = (-LRU_C * r) * sp
    a = jnp.exp(log_a)
    m = -jnp.tanh(log_a) * (a * a + 1.0)
    mult = jnp.where(m > 0.0, m * lax.rsqrt(m), 0.0)
    return a, mult, i * xc


def _scan_chunk(a, u, h, reverse):
    rows, width = a.shape
    groups = rows // SUBLANES
    a = a.reshape(groups, SUBLANES, width)
    u = u.reshape(groups, SUBLANES, width)
    sub = lax.broadcasted_iota(jnp.int32, a.shape, 1)
    s = 1
    while s < SUBLANES:
        if reverse:
            m = sub < SUBLANES - s
            a_sh = pltpu.roll(a, SUBLANES - s, 1)
            u_sh = pltpu.roll(u, SUBLANES - s, 1)
        else:
            m = sub >= s
            a_sh = pltpu.roll(a, s, 1)
            u_sh = pltpu.roll(u, s, 1)
        u = jnp.where(m, a * u_sh + u, u)
        a = jnp.where(m, a * a_sh, a)
        s *= 2
    outs = [None] * groups
    order = range(groups - 1, -1, -1) if reverse else range(groups)
    for g in order:
        hg = a[g] * h + u[g]
        h = hg[0:1] if reverse else hg[SUBLANES - 1:SUBLANES]
        outs[g] = hg
    return jnp.concatenate(outs, axis=0), h


def _scan_strands(a, u, h, reverse):
    a, u = _planes(a), _planes(u)
    order = list(range(SUBLANES - 1, -1, -1)) if reverse else list(range(SUBLANES))
    for prev_s, s in zip(order[:-1], order[1:]):
        u[s] = a[s] * u[prev_s] + u[s]
        a[s] = a[s] * a[prev_s]
    last = order[-1]
    ends, h_out = _scan_chunk(a[last], u[last], h, reverse)
    carry_in = _shift_up(ends, h) if reverse else _shift_down(ends, h)
    return jnp.concatenate([a[s] * carry_in + u[s] for s in range(SUBLANES)], axis=0), h_out


def _rglru_kernel(xl_ref, xc_ref, gate_ref, cw_ref, cb_ref, wf_ref, wb_ref, bf_ref, bb_ref, lam_ref,
                  o_ref, hf_ref, cl_ref, cc_ref, *, n, ctx_len):
    cw = cw_ref[...]
    cb = cb_ref[...]
    tc = min(LRU_CHUNK, ctx_len)
    tl = min(LRU_CHUNK, n)

    def direction(d, w_ref, b_ref):
        reverse = d == 1
        w = w_ref[0]
        bias = b_ref[0]
        z = -lam_ref[0, pl.ds(d, 1), :]
        sp = jnp.maximum(z, 0.0) + jnp.log1p(jnp.exp(-jnp.abs(z)))
        first_row = ctx_len - 1 if reverse else 0

        def conv(src_ref, cache_ref, t0, rows, total):
            if reverse:
                return cache_ref[pl.ds(t0, rows), :]
            xc = _conv_strands(src_ref, t0, rows, total, cw, cb)
            cache_ref[pl.ds(t0, rows), :] = xc
            return xc

        def ctx_step(c, h):
            cc = (ctx_len // tc - 1 - c) if reverse else c
            t0 = pl.multiple_of(cc * tc, SUBLANES)
            xc = conv(xc_ref, cc_ref, t0, tc, ctx_len)
            a, mult, ix = _lru_gates(xc, w, bias, sp)
            r = lax.broadcasted_iota(jnp.int32, a.shape, 0)
            groups = tc // SUBLANES
            time = (r % groups) * SUBLANES + r // groups + t0
            mult = jnp.where(time == first_row, 1.0, mult)
            _, h = _scan_strands(a, mult * ix, h, reverse)
            return h

        h = lax.fori_loop(0, ctx_len // tc, ctx_step, jnp.zeros((1, LRU_HALF), F32))

        def lat_step(c, h):
            cc = (n // tl - 1 - c) if reverse else c
            t0 = pl.multiple_of(cc * tl, SUBLANES)
            xc = conv(xl_ref, cl_ref, t0, tl, n)
            a, mult, ix = _lru_gates(xc, w, bias, sp)
            hs, h = _scan_strands(a, mult * ix, h, reverse)
            if reverse:
                y = _from_strands(hf_ref[pl.ds(t0, tl), :] + hs) * jax.nn.gelu(gate_ref[0, pl.ds(t0, tl), :])
                o_ref[0, pl.ds(t0, tl), :] = y.astype(BF16)
            else:
                hf_ref[pl.ds(t0, tl), :] = hs
            return h

        lax.fori_loop(0, n // tl, lat_step, h)

    direction(0, wf_ref, bf_ref)
    direction(1, wb_ref, bb_ref)


def _rglru(xl, xc, gate, conv_w, conv_b, w_f, w_b, b_f, b_b, lam):
    bsz, n, _ = xl.shape
    ctx_len = xc.shape[1]
    halves = LRU_WIDTH // LRU_HALF
    tokh = lambda b, hf: (b, 0, hf)
    return pl.pallas_call(
        functools.partial(_rglru_kernel, n=n, ctx_len=ctx_len),
        grid=(bsz, halves),
        in_specs=[pl.BlockSpec((1, n, LRU_HALF), tokh),
                  pl.BlockSpec((1, ctx_len, LRU_HALF), tokh),
                  pl.BlockSpec((1, n, LRU_HALF), tokh),
                  pl.BlockSpec((CONV_W, LRU_HALF), lambda b, hf: (0, hf)),
                  pl.BlockSpec((1, LRU_HALF), lambda b, hf: (0, hf)),
                  pl.BlockSpec((1, LRU_HALF, 2 * LRU_HALF), lambda b, hf: (hf, 0, 0)),
                  pl.BlockSpec((1, LRU_HALF, 2 * LRU_HALF), lambda b, hf: (hf, 0, 0)),
                  pl.BlockSpec((1, 1, 2 * LRU_HALF), lambda b, hf: (hf, 0, 0)),
                  pl.BlockSpec((1, 1, 2 * LRU_HALF), lambda b, hf: (hf, 0, 0)),
                  pl.BlockSpec((1, 2, LRU_HALF), lambda b, hf: (hf, 0, 0))],
        out_specs=pl.BlockSpec((1, n, LRU_HALF), tokh),
        out_shape=jax.ShapeDtypeStruct((bsz, n, LRU_WIDTH), BF16),
        scratch_shapes=[pltpu.VMEM((n, LRU_HALF), F32),
                        pltpu.VMEM((n, LRU_HALF), F32),
                        pltpu.VMEM((ctx_len, LRU_HALF), F32)],
        compiler_params=_cparams(("parallel", "parallel")),
        name="rglru",
    )(xl, xc, gate, conv_w, conv_b, w_f, w_b, b_f, b_b, lam)


def _lru_gate_weights(wa, ba, wi, bi):
    halves = LRU_WIDTH // LRU_HALF
    per = LRU_BLOCKS // halves
    bw = LRU_WIDTH // LRU_BLOCKS

    def dense(w):
        w = w.reshape(halves, per, bw, bw)
        eye = jnp.eye(per, dtype=w.dtype)
        return jnp.einsum("hpij,pq->hpiqj", w, eye).reshape(halves, per * bw, per * bw)

    w_cat = jnp.concatenate([dense(wa), dense(wi)], axis=-1).astype(BF16)
    b_cat = jnp.concatenate([ba.reshape(halves, 1, LRU_HALF), bi.reshape(halves, 1, LRU_HALF)], axis=-1)
    return w_cat, b_cat


ATT_TQ = 256
ATT_TK = 256


def _attn_kernel(q_ref, kc_ref, vc_ref, kl_ref, vl_ref, lq1_ref, lk1_ref, lq2_ref, lk2_ref, sg_ref, o_ref,
                 *, n, tk):
    q = q_ref[0]
    tq = q.shape[0]
    lane = lax.broadcasted_iota(jnp.int32, q.shape, 1)
    zero = jnp.zeros_like(q)
    q2 = jnp.concatenate([jnp.where(lane < HEAD_DIM, q, zero), jnp.where(lane >= HEAD_DIM, q, zero)], axis=0)

    def ext(v):
        ln = lax.broadcasted_iota(jnp.int32, v.shape, 1)
        return jnp.concatenate([v, jnp.where(ln == 0, 1.0, 0.0).astype(BF16)], axis=1)

    def step(k, v, carry):
        m, acc = carry
        s = _dot_nt(q2, k)
        m_new = jnp.maximum(m, jnp.max(s, axis=1, keepdims=True))
        p = jnp.exp2(s - m_new).astype(BF16)
        acc = jnp.exp2(m - m_new) * acc + _dot(p, ext(v))
        return m_new, acc

    carry = (jnp.full((2 * tq, 1), -1e30, F32), jnp.zeros((2 * tq, 2 * V_DIM), F32))
    carry = step(kc_ref[0], vc_ref[0], carry)
    for j in range(n // tk):
        carry = step(kl_ref[0, j * tk:(j + 1) * tk, :], vl_ref[0, j * tk:(j + 1) * tk, :], carry)
    _, acc = carry
    o = acc[:, :V_DIM] / acc[:, V_DIM:V_DIM + 1]
    lam = (jnp.exp(jnp.sum(lq1_ref[...] * lk1_ref[...], keepdims=True))
           - jnp.exp(jnp.sum(lq2_ref[...] * lk2_ref[...], keepdims=True)) + LAM_INIT)
    att = o[:tq] - lam * o[tq:]
    ms = jnp.mean(att * att, axis=-1, keepdims=True)
    y = att * lax.rsqrt(ms + EPS) * sg_ref[...]
    o_ref[0] = (y * (1.0 - LAM_INIT)).astype(BF16)


def _attention(q, kc, vc, kl, vl, lq1, lk1, lq2, lk2, sg):
    bsz, n, _ = q.shape
    ctx_len = kc.shape[1]
    tq = min(ATT_TQ, n)
    tk = min(ATT_TK, n)
    vec = lambda b, h, i: (0, 0)
    return pl.pallas_call(
        functools.partial(_attn_kernel, n=n, tk=tk),
        grid=(bsz, ATT_HEADS, n // tq),
        in_specs=[pl.BlockSpec((1, tq, V_DIM), lambda b, h, i: (b, i, h)),
                  pl.BlockSpec((1, ctx_len, V_DIM), lambda b, h, i: (b, 0, h)),
                  pl.BlockSpec((1, ctx_len, V_DIM), lambda b, h, i: (b, 0, h)),
                  pl.BlockSpec((1, n, V_DIM), lambda b, h, i: (b, 0, h)),
                  pl.BlockSpec((1, n, V_DIM), lambda b, h, i: (b, 0, h)),
                  pl.BlockSpec((1, HEAD_DIM), vec),
                  pl.BlockSpec((1, HEAD_DIM), vec),
                  pl.BlockSpec((1, HEAD_DIM), vec),
                  pl.BlockSpec((1, HEAD_DIM), vec),
                  pl.BlockSpec((1, V_DIM), vec)],
        out_specs=pl.BlockSpec((1, tq, V_DIM), lambda b, h, i: (b, i, h)),
        out_shape=jax.ShapeDtypeStruct((bsz, n, ATT_WIDTH), BF16),
        compiler_params=_cparams(("parallel", "parallel", "arbitrary")),
        name="diff_attention",
    )(q, kc, vc, kl, vl, lq1, lk1, lq2, lk2, sg)


def _outproj_kernel(lru_ref, att_ref, x_ref, g1_ref, shift_ref, scale_ref, n2_ref, wo_ref, wr_ref,
                    x1_ref, h2_ref, lg_ref):
    mix = _dot(lru_ref[0], wo_ref[:LRU_WIDTH, :]) + _dot(att_ref[0], wo_ref[LRU_WIDTH:, :])
    x1 = x_ref[0] + g1_ref[0] * mix
    x1_ref[0] = x1
    ms = jnp.mean(x1 * x1, axis=-1, keepdims=True)
    h2 = x1 * lax.rsqrt(ms + EPS) * n2_ref[...]
    h2 = h2 * (1.0 + scale_ref[0]) + shift_ref[0]
    h2_ref[0] = h2
    lg_ref[0] = _dot_nt(wr_ref[...], h2.astype(BF16))


def _outproj(lru, att, x, g1, shift2, scale2, n2g, wo_bf, wr_bf):
    bsz, n, d = x.shape
    tm = min(TOKEN_TILE, n)
    tok = lambda b, i: (b, i, 0)
    mod = lambda b, i: (b, 0, 0)
    full = lambda b, i: (0, 0)
    return pl.pallas_call(
        _outproj_kernel,
        grid=(bsz, n // tm),
        in_specs=[pl.BlockSpec((1, tm, LRU_WIDTH), tok),
                  pl.BlockSpec((1, tm, ATT_WIDTH), tok),
                  pl.BlockSpec((1, tm, d), tok),
                  pl.BlockSpec((1, 1, d), mod),
                  pl.BlockSpec((1, 1, d), mod),
                  pl.BlockSpec((1, 1, d), mod),
                  pl.BlockSpec((1, d), full),
                  pl.BlockSpec((d, d), full),
                  pl.BlockSpec((N_EXPERTS, d), full)],
        out_specs=[pl.BlockSpec((1, tm, d), tok),
                   pl.BlockSpec((1, tm, d), tok),
                   pl.BlockSpec((1, N_EXPERTS, tm), lambda b, i: (b, 0, i))],
        out_shape=(jax.ShapeDtypeStruct((bsz, n, d), F32),
                   jax.ShapeDtypeStruct((bsz, n, d), F32),
                   jax.ShapeDtypeStruct((bsz, N_EXPERTS, n), F32)),
        compiler_params=_cparams(("parallel", "parallel")),
        name="outproj_router",
    )(lru, att, x, g1, shift2, scale2, n2g, wo_bf, wr_bf)


TOPK_UNROLL = 4


def _topk_kernel(lg_ref, idx_ref, g_ref, aff_ref, rank_ref, start_ref, tot_ref, *, cap):
    lg = lg_ref[0]
    ne, nb, blk = lg.shape
    ex = jnp.exp(lg - jnp.max(lg, axis=0, keepdims=True))
    aff = ex / jnp.sum(ex, axis=0, keepdims=True)
    aff_ref[...] = aff
    capf = float(cap)

    def bisect(_, c):
        lo, hi = c
        mid = lo + ((hi - lo) >> 1)
        cnt = jnp.sum(jnp.where(aff >= pltpu.bitcast(mid, F32), 1.0, 0.0), axis=(1, 2), keepdims=True)
        ge = cnt >= capf
        return jnp.where(ge, mid, lo), jnp.where(ge, hi, mid)

    lo0 = jnp.zeros((ne, 1, 1), jnp.int32)
    hi0 = jnp.full((ne, 1, 1), 0x3F800001, jnp.int32)
    thr, nxt = lax.fori_loop(0, 31, bisect, (lo0, hi0))
    gt = aff >= pltpu.bitcast(nxt, F32)
    gtf = jnp.where(gt, 1.0, 0.0)
    eqf = jnp.where(jnp.logical_and(aff >= pltpu.bitcast(thr, F32), jnp.logical_not(gt)), 1.0, 0.0)
    need = capf - jnp.sum(gtf, axis=(1, 2), keepdims=True)

    rows = ne * nb
    tri = (lax.broadcasted_iota(jnp.int32, (blk, blk), 0)
           <= lax.broadcasted_iota(jnp.int32, (blk, blk), 1)).astype(BF16)
    r_i = lax.broadcasted_iota(jnp.int32, (rows, rows), 0)
    c_i = lax.broadcasted_iota(jnp.int32, (rows, rows), 1)
    earlier = jnp.logical_and(c_i < r_i, c_i >= (r_i // nb) * nb).astype(BF16)

    def prefix(x3):
        cin = _dot(x3.reshape(rows, blk).astype(BF16), tri)
        tot = jnp.broadcast_to(cin[:, blk - 1:blk], cin.shape)
        start = _dot(earlier, tot.astype(BF16))
        return cin.reshape(x3.shape), start.reshape(x3.shape), tot.reshape(x3.shape)

    cin, start, _ = prefix(eqf)
    sel = gtf + eqf * jnp.where(cin + start <= need, 1.0, 0.0)
    cin, start, tot = prefix(sel)
    rank_ref[...] = cin * sel
    start_ref[...] = start
    tot_ref[...] = tot

    slot = (lax.broadcasted_iota(jnp.int32, (1, cap), 1) + 1).astype(F32)
    row_id = lax.broadcasted_iota(jnp.int32, (nb, 1), 0).astype(F32)
    lane_id = lax.broadcasted_iota(jnp.int32, (blk, 1), 0).astype(F32)

    def per_expert(e):
        a = aff_ref[e]
        st = start_ref[e][:, 0:1]
        in_row = jnp.logical_and(st < slot, slot <= st + tot_ref[e][:, 0:1])
        in_row_bf = jnp.where(in_row, 1.0, 0.0).astype(BF16)
        row_of = jnp.sum(jnp.where(in_row, row_id, 0.0), axis=0, keepdims=True)
        rank_need = slot - jnp.sum(jnp.where(in_row, st, 0.0), axis=0, keepdims=True)

        def pick(x_bf):
            return lax.dot_general(x_bf, in_row_bf, (((0,), (0,)), ((), ())), preferred_element_type=F32)

        hit = pick(rank_ref[e].astype(BF16)) == rank_need
        lane_of = jnp.sum(jnp.where(hit, lane_id, 0.0), axis=0, keepdims=True)
        a_hi = a.astype(BF16)
        a_mid = (a - a_hi.astype(F32)).astype(BF16)
        a_lo = (a - a_hi.astype(F32) - a_mid.astype(F32)).astype(BF16)
        a_sel = (pick(a_hi) + pick(a_mid)) + pick(a_lo)
        idx_ref[0, e] = (row_of * float(blk) + lane_of).astype(jnp.int32)
        g_ref[0, e] = jnp.sum(jnp.where(hit, a_sel, 0.0), axis=0, keepdims=True)

    def expert_group(gi, carry):
        for r in range(TOPK_UNROLL):
            per_expert(gi * TOPK_UNROLL + r)
        return carry

    lax.fori_loop(0, ne // TOPK_UNROLL, expert_group, 0)


def _topk(logits_t, cap):
    bsz, _, n = logits_t.shape
    blk = min(LANES, n)
    nb = n // blk
    slab = pltpu.VMEM((N_EXPERTS, nb, blk), F32)
    return pl.pallas_call(
        functools.partial(_topk_kernel, cap=cap),
        grid=(bsz,),
        in_specs=[pl.BlockSpec((1, N_EXPERTS, nb, blk), lambda b: (b, 0, 0, 0))],
        out_specs=[pl.BlockSpec((1, N_EXPERTS, 1, cap), lambda b: (b, 0, 0, 0)),
                   pl.BlockSpec((1, N_EXPERTS, 1, cap), lambda b: (b, 0, 0, 0))],
        out_shape=(jax.ShapeDtypeStruct((bsz, N_EXPERTS, 1, cap), jnp.int32),
                   jax.ShapeDtypeStruct((bsz, N_EXPERTS, 1, cap), F32)),
        scratch_shapes=[slab, slab, slab, slab],
        compiler_params=_cparams(("parallel",)),
        name="expert_topk",
    )(logits_t.reshape(bsz, N_EXPERTS, nb, blk))


GATHER_UNROLL = 8


def _gather_kernel(idx_ref, h_ref, o_ref, *, cap):
    unroll = GATHER_UNROLL if cap % (SUBLANES * GATHER_UNROLL) == 0 else 1

    def body(jj, carry):
        for u in range(unroll):
            j0 = pl.multiple_of((jj * unroll + u) * SUBLANES, SUBLANES)
            rows = [h_ref[0, pl.ds(idx_ref[0, 0, j0 + r], 1), :] for r in range(SUBLANES)]
            o_ref[0, 0, pl.ds(j0, SUBLANES), :] = jnp.concatenate(rows, axis=0).astype(BF16)
        return carry

    lax.fori_loop(0, cap // (SUBLANES * unroll), body, 0)


def _gather(idx3, h2, cap):
    bsz, n, d = h2.shape
    return pl.pallas_call(
        functools.partial(_gather_kernel, cap=cap),
        grid=(bsz, N_EXPERTS),
        in_specs=[pl.BlockSpec((1, 1, cap), lambda b, e: (b * N_EXPERTS + e, 0, 0), memory_space=pltpu.SMEM),
                  pl.BlockSpec((1, n, d), lambda b, e: (b, 0, 0))],
        out_specs=pl.BlockSpec((1, 1, cap, d), lambda b, e: (e, b, 0, 0)),
        out_shape=jax.ShapeDtypeStruct((N_EXPERTS, bsz, cap, d), BF16),
        compiler_params=_cparams(("parallel", "arbitrary")),
        name="expert_gather",
    )(idx3, h2)


FFN_TF = 256


FFN_SUB = 512


def _ffn_kernel(x_ref, wg_ref, wu_ref, wd_ref, g_ref, gate_ref, o_ref, *, cap):
    i = pl.program_id(1)
    f = pl.program_id(2)
    nf = pl.num_programs(2)
    tm = x_ref.shape[1]
    sub = min(FFN_SUB, tm, cap)

    def body(first, last):
        wgu = jnp.concatenate([wg_ref[0].astype(BF16), wu_ref[0].astype(BF16)], axis=1)
        wd = wd_ref[0].astype(BF16)
        tf = wd.shape[0]
        for r in range(tm // sub):
            rows = slice(r * sub, (r + 1) * sub)
            gu = _dot(x_ref[0, rows, :], wgu)
            hg = gu[:, :tf]
            h = (hg * jax.nn.sigmoid(hg)) * gu[:, tf:]
            y = _dot(h.astype(BF16), wd)
            if not first:
                y = o_ref[0, rows, :] + y
            if last:
                sample = (i * tm + r * sub) // cap
                g_col = jnp.transpose(jnp.broadcast_to(g_ref[0, :, rows], (LANES, sub)))[:, 0:1]
                y = y * (g_col * gate_ref[sample])
            o_ref[0, rows, :] = y

    @pl.when(f == 0)
    def _():
        body(True, False)

    @pl.when(jnp.logical_and(f > 0, f < nf - 1))
    def _():
        body(False, False)

    @pl.when(f == nf - 1)
    def _():
        body(False, True)


def _ffn(xe, w_gate, w_up, w_down, g_row, gate2, cap):
    ne, m, d = xe.shape
    dff = w_gate.shape[2]
    tm = min(2048, m)
    tf = FFN_TF
    assert dff // tf > 1 and cap % min(FFN_SUB, tm, cap) == 0
    return pl.pallas_call(
        functools.partial(_ffn_kernel, cap=cap),
        grid=(ne, m // tm, dff // tf),
        in_specs=[pl.BlockSpec((1, tm, d), lambda e, i, f: (e, i, 0)),
                  pl.BlockSpec((1, d, tf), lambda e, i, f: (e, 0, f)),
                  pl.BlockSpec((1, d, tf), lambda e, i, f: (e, 0, f)),
                  pl.BlockSpec((1, tf, d), lambda e, i, f: (e, f, 0)),
                  pl.BlockSpec((1, 1, tm), lambda e, i, f: (e, 0, i)),
                  pl.BlockSpec(gate2.shape, lambda e, i, f: (0, 0, 0))],
        out_specs=pl.BlockSpec((1, tm, d), lambda e, i, f: (e, i, 0)),
        out_shape=jax.ShapeDtypeStruct((ne, m, d), F32),
        compiler_params=_cparams(("parallel", "parallel", "arbitrary")),
        name="expert_ffn",
    )(xe, w_gate, w_up, w_down, g_row, gate2)


COMBINE_UNROLL = 4


def _combine_kernel(idx_ref, x1_hbm, y_ref, o_ref, sem, *, cap):
    b = pl.program_id(0)
    e = pl.program_id(1)

    @pl.when(e == 0)
    def _():
        init = pltpu.make_async_copy(x1_hbm.at[b], o_ref.at[0], sem)
        init.start()
        init.wait()

    sub = lax.broadcasted_iota(jnp.int32, (SUBLANES, o_ref.shape[2]), 0)
    unroll = COMBINE_UNROLL if cap % (SUBLANES * COMBINE_UNROLL) == 0 else 1

    def body(jj, carry):
        for u in range(unroll):
            j0 = pl.multiple_of((jj * unroll + u) * SUBLANES, SUBLANES)
            ys = y_ref[0, 0, pl.ds(j0, SUBLANES), :]
            for r in range(0, SUBLANES, 2):
                t0 = idx_ref[0, 0, j0 + r]
                t1 = idx_ref[0, 0, j0 + r + 1]
                b0 = pl.multiple_of((t0 >> 3) << 3, SUBLANES)
                b1 = pl.multiple_of((t1 >> 3) << 3, SUBLANES)
                tile0 = o_ref[0, pl.ds(b0, SUBLANES), :]
                tile1 = o_ref[0, pl.ds(b1, SUBLANES), :]
                u0 = jnp.where(sub == (t0 & (SUBLANES - 1)), ys[r:r + 1], 0.0)
                u1 = jnp.where(sub == (t1 & (SUBLANES - 1)), ys[r + 1:r + 2], 0.0)
                o_ref[0, pl.ds(b0, SUBLANES), :] = tile0 + u0
                o_ref[0, pl.ds(b1, SUBLANES), :] = tile1 + u1 + jnp.where(b0 == b1, u0, 0.0)
        return carry

    lax.fori_loop(0, cap // (SUBLANES * unroll), body, 0)


def _combine(idx3, x1, ye, cap):
    bsz, n, d = x1.shape
    resident = lambda b, e: (b, 0, 0)
    return pl.pallas_call(
        functools.partial(_combine_kernel, cap=cap),
        grid=(bsz, N_EXPERTS),
        in_specs=[pl.BlockSpec((1, 1, cap), lambda b, e: (b * N_EXPERTS + e, 0, 0), memory_space=pltpu.SMEM),
                  pl.BlockSpec(memory_space=pl.ANY),
                  pl.BlockSpec((1, 1, cap, d), lambda b, e: (e, b, 0, 0))],
        out_specs=pl.BlockSpec((1, n, d), resident),
        out_shape=jax.ShapeDtypeStruct((bsz, n, d), F32),
        scratch_shapes=[pltpu.SemaphoreType.DMA],
        compiler_params=_cparams(("parallel", "arbitrary")),
        name="expert_combine",
    )(idx3, x1, ye)


def _rope_tables(n):
    rows = n // GRID_W
    row = jnp.repeat(jnp.arange(rows), GRID_W).astype(F32)
    col = jnp.tile(jnp.arange(GRID_W), rows).astype(F32)
    inv = ROPE_BASE ** (-jnp.arange(ROPE_PAIRS, dtype=F32) / ROPE_PAIRS)
    ang_r = row[:, None] * inv
    ang_c = col[:, None] * inv
    cos64 = jnp.concatenate([jnp.cos(ang_r), jnp.cos(ang_r), jnp.cos(ang_c), jnp.cos(ang_c)], axis=1)
    sin64 = jnp.concatenate([-jnp.sin(ang_r), jnp.sin(ang_r), -jnp.sin(ang_c), jnp.sin(ang_c)], axis=1)
    return jnp.tile(cos64, (1, 2)), jnp.tile(sin64, (1, 2))


def kernel(x, c, ctx, c_ctx, w_ada, b_ada, norm1_g, norm2_g, w_in, conv_w, conv_b, lru_wa, lru_ba, lru_wi,
           lru_bi, lru_lambda, q_norm_g, k_norm_g, lambda_q1, lambda_k1, lambda_q2, lambda_k2, subln_g, w_out,
           w_router, w_gate, w_up, w_down):
    assert w_ada.shape[0] == 1, "single-layer configuration"
    bsz, n, d = x.shape
    ctx_len = ctx.shape[1]
    cap = EC_FACTOR * n // N_EXPERTS

    rows = ((bsz + 1 + SUBLANES - 1) // SUBLANES) * SUBLANES
    cvec = jnp.zeros((rows, d), F32).at[:bsz].set(c).at[bsz].set(c_ctx)
    mod = _adaln(cvec, w_ada[0], b_ada[0][None]).reshape(rows, N_MOD, 1, d)
    mod_l = mod[:bsz]
    mod_c = mod[bsz:bsz + 1]

    w_in_bf = w_in[0].astype(BF16)
    w_out_bf = w_out[0].astype(BF16)
    w_r_bf = w_router[0].T.astype(BF16)
    qg = jnp.tile(q_norm_g[0], QK_WIDTH // HEAD_DIM)[None]
    kg = jnp.tile(k_norm_g[0], QK_WIDTH // HEAD_DIM)[None]
    seg = jnp.arange(MXU_DIM) // HEAD_DIM
    ones_bd = (seg[:, None] == seg[None, :]).astype(BF16)
    cos_t, sin_t = _rope_tables(n)
    cos_c = jnp.ones((ctx_len, LANES), F32)
    sin_c = jnp.zeros((ctx_len, LANES), F32)
    g1 = norm1_g[0][None]

    xl, gate_l, q_l, k_l, v_l = _inproj(x, mod_l[:, 0], mod_l[:, 1], g1, w_in_bf, qg, kg, cos_t, sin_t,
                                        ones_bd, True)
    xc, _, _, k_c, v_c = _inproj(ctx, mod_c[:, 0], mod_c[:, 1], g1, w_in_bf, qg, kg, cos_c, sin_c,
                                 ones_bd, False)

    w_f, b_f = _lru_gate_weights(lru_wa[0, 0], lru_ba[0, 0], lru_wi[0, 0], lru_bi[0, 0])
    w_b, b_b = _lru_gate_weights(lru_wa[0, 1], lru_ba[0, 1], lru_wi[0, 1], lru_bi[0, 1])
    halves = LRU_WIDTH // LRU_HALF
    lam_h = lru_lambda[0].reshape(2, halves, LRU_HALF).transpose(1, 0, 2)
    lru = _rglru(xl, xc, gate_l, conv_w[0], conv_b[0][None], w_f, w_b, b_f, b_b, lam_h)

    att = _attention(q_l, k_c, v_c, k_l, v_l, lambda_q1, lambda_k1, lambda_q2, lambda_k2, subln_g)

    x1, h2, logits = _outproj(lru, att, x, mod_l[:, 2], mod_l[:, 3], mod_l[:, 4], norm2_g[0][None],
                              w_out_bf, w_r_bf)

    idx, gates = _topk(logits, cap)
    idx3 = idx.reshape(bsz * N_EXPERTS, 1, cap)
    g_row = jnp.swapaxes(gates.reshape(bsz, N_EXPERTS, cap), 0, 1).reshape(N_EXPERTS, 1, bsz * cap)
    xe = _gather(idx3, h2, cap)
    ye = _ffn(xe.reshape(N_EXPERTS, bsz * cap, d), w_gate[0], w_up[0], w_down[0], g_row, mod_l[:, 5], cap)
    return _combine(idx3, x1, ye.reshape(N_EXPERTS, bsz, cap, d), cap)
```
